```python
import jax, jax.numpy as jnp
from jax import lax
import numpy as np

D_MODEL = 1024
BATCH = 16
SEQ = 2048
DEPTH = 4

GRID_W = 64
CTX_LEN = 256
EPS = 1e-6
ROPE_BASE = 10000.0
BLOCK = 128
A_HEADS = 8
A_KV_HEADS = 2
A_HEAD_DIM = 64
A_GROUP = A_HEADS // A_KV_HEADS
WINDOW = 128
A_Q_W = A_HEADS * A_HEAD_DIM
A_KV_W = A_KV_HEADS * A_HEAD_DIM
B_HEADS = 8
B_NOPE_DIM = 64
B_ROPE_DIM = 32
B_V_DIM = 64
B_Q_RANK = 256
B_KV_RANK = 128
B_OUT_W = B_HEADS * B_V_DIM
N_EXPERTS = 16
N_GROUPS = 4
EXPERTS_PER_GROUP = N_EXPERTS // N_GROUPS
TOP_K = 2
D_EXPERT = 512
IN_WIDTHS = (A_Q_W, A_KV_W, A_KV_W, B_Q_RANK, B_KV_RANK, B_ROPE_DIM, D_MODEL, D_MODEL)
IN_OFFSETS = tuple(int(v) for v in np.cumsum(IN_WIDTHS)[:-1])
D_IN = sum(IN_WIDTHS)

kernel_name = 'hybrid_swa_mla_grouped_moe_diffusion_trunk'


def rmsnorm(x, g):
    xf = x.astype(jnp.float32)
    y = xf * lax.rsqrt(jnp.mean(xf * xf, axis=-1, keepdims=True) + EPS)
    return (y * g.astype(jnp.float32)).astype(x.dtype)


def modulate(h, shift, scale):
    return h * (1 + scale) + shift


def rope_tables(pos, dim):
    inv = ROPE_BASE ** (-jnp.arange(0, dim, 2, dtype=jnp.float32) / dim)
    ang = pos.astype(jnp.float32)[:, None] * inv[None, :]
    return jnp.cos(ang), jnp.sin(ang)


def rope_1d(x, cos, sin):
    half = x.shape[-1] // 2
    x1, x2 = x[..., :half], x[..., half:]
    c = cos[:, None, :].astype(x.dtype)
    s = sin[:, None, :].astype(x.dtype)
    return jnp.concatenate([x1 * c - x2 * s, x1 * s + x2 * c], axis=-1)


def rope_2d(x, rows, cols):
    d = x.shape[-1] // 2
    cr, sr = rope_tables(rows, d)
    cc, sc = rope_tables(cols, d)
    return jnp.concatenate([rope_1d(x[..., :d], cr, sr), rope_1d(x[..., d:], cc, sc)], axis=-1)


def softmax_with_sink(logits, sink):
    m = jnp.maximum(jnp.max(logits, axis=-1, keepdims=True), sink)
    e = jnp.exp(logits - m)
    return e / (jnp.sum(e, axis=-1, keepdims=True) + jnp.exp(sink - m))


def project(h, w_in, q_norm, w_uq, kv_norm, w_ukv):
    B, L, _ = h.shape
    q_a, k_a, v_a, c_q, c_kv, k_r, g_a, g_b = jnp.split(h @ w_in, IN_OFFSETS, axis=-1)
    q_a = q_a.reshape(B, L, A_HEADS, A_HEAD_DIM)
    k_a = k_a.reshape(B, L, A_KV_HEADS, A_HEAD_DIM)
    v_a = v_a.reshape(B, L, A_KV_HEADS, A_HEAD_DIM)
    q_b = (rmsnorm(c_q, q_norm) @ w_uq).reshape(B, L, B_HEADS, B_NOPE_DIM + B_ROPE_DIM)
    kv_b = (rmsnorm(c_kv, kv_norm) @ w_ukv).reshape(B, L, B_HEADS, B_NOPE_DIM + B_V_DIM)
    q_n, q_r = q_b[..., :B_NOPE_DIM], q_b[..., B_NOPE_DIM:]
    k_n, v_b = kv_b[..., :B_NOPE_DIM], kv_b[..., B_NOPE_DIM:]
    return q_a, k_a, v_a, q_n, q_r, k_n, k_r, v_b, g_a, g_b


def window_gqa_latent(q, k, v, k_ctx, v_ctx, sink):
    B, S = q.shape[:2]
    nb = S // BLOCK
    scale = A_HEAD_DIM ** -0.5
    pad = ((0, 0), (BLOCK, BLOCK), (0, 0), (0, 0))
    kp, vp = jnp.pad(k, pad), jnp.pad(v, pad)
    s_sink = sink.astype(jnp.float32).reshape(A_KV_HEADS, A_GROUP)[None, :, :, None, None]
    qb = jnp.moveaxis(q.reshape(B, nb, BLOCK, A_KV_HEADS, A_GROUP, A_HEAD_DIM), 1, 0)

    def one_block(args):
        n, q_n = args
        k_b = lax.dynamic_slice_in_dim(kp, n * BLOCK, 3 * BLOCK, axis=1)
        v_b = lax.dynamic_slice_in_dim(vp, n * BLOCK, 3 * BLOCK, axis=1)
        qi = n * BLOCK + jnp.arange(BLOCK)
        kj = (n - 1) * BLOCK + jnp.arange(3 * BLOCK)
        valid = (kj[None, :] >= 0) & (kj[None, :] < S) & (jnp.abs(qi[:, None] - kj[None, :]) <= WINDOW)
        s_b = jnp.einsum('bqkgd,bjkd->bkgqj', q_n, k_b).astype(jnp.float32) * scale
        s_b = jnp.where(valid, s_b, -jnp.inf)
        s_c = jnp.einsum('bqkgd,bckd->bkgqc', q_n, k_ctx).astype(jnp.float32) * scale
        p = softmax_with_sink(jnp.concatenate([s_b, s_c], axis=-1), s_sink).astype(v.dtype)
        return (jnp.einsum('bkgqj,bjkd->bqkgd', p[..., :3 * BLOCK], v_b)
                + jnp.einsum('bkgqc,bckd->bqkgd', p[..., 3 * BLOCK:], v_ctx))

    o = lax.map(one_block, (jnp.arange(nb), qb))
    return jnp.moveaxis(o, 0, 1).reshape(B, S, A_Q_W)


def gqa_context(q, k, v, sink):
    B, C = q.shape[:2]
    qg = q.reshape(B, C, A_KV_HEADS, A_GROUP, A_HEAD_DIM)
    s = jnp.einsum('bqkgd,bckd->bkgqc', qg, k).astype(jnp.float32) * (A_HEAD_DIM ** -0.5)
    s_sink = sink.astype(jnp.float32).reshape(A_KV_HEADS, A_GROUP)[None, :, :, None, None]
    p = softmax_with_sink(s, s_sink).astype(v.dtype)
    return jnp.einsum('bkgqc,bckd->bqkgd', p, v).reshape(B, C, A_Q_W)


def mla_scores(q_n, q_r, k_n, k_r):
    s = jnp.einsum('bqhd,bkhd->bhqk', q_n, k_n) + jnp.einsum('bqhr,bkr->bhqk', q_r, k_r)
    return s.astype(jnp.float32) * ((B_NOPE_DIM + B_ROPE_DIM) ** -0.5)


def mla_latent(q_n, q_r, k_n, k_r, v, k_n_ctx, k_r_ctx, v_ctx):
    B, S = q_n.shape[:2]
    nb = S // BLOCK
    kn_all = jnp.concatenate([k_n_ctx, k_n], axis=1)
    kr_all = jnp.concatenate([k_r_ctx, k_r], axis=1)
    v_all = jnp.concatenate([v_ctx, v], axis=1)

    def blocks(t):
        return jnp.moveaxis(t.reshape(B, nb, BLOCK, *t.shape[2:]), 1, 0)

    def one_block(args):
        qn_b, qr_b = args
        p = jax.nn.softmax(mla_scores(qn_b, qr_b, kn_all, kr_all), axis=-1).astype(v_all.dtype)
        return jnp.einsum('bhqk,bkhd->bqhd', p, v_all)

    o = lax.map(one_block, (blocks(q_n), blocks(q_r)))
    return jnp.moveaxis(o, 0, 1).reshape(B, S, B_OUT_W)


def mla_context(q_n, q_r, k_n, k_r, v):
    B, C = q_n.shape[:2]
    p = jax.nn.softmax(mla_scores(q_n, q_r, k_n, k_r), axis=-1).astype(v.dtype)
    return jnp.einsum('bhqk,bkhd->bqhd', p, v).reshape(B, C, B_OUT_W)


def merge(y_a, y_b, g_a, g_b, w_o_a, w_o_b, w_out):
    return (jax.nn.sigmoid(g_a) * (y_a @ w_o_a) + jax.nn.sigmoid(g_b) * (y_b @ w_o_b)) @ w_out


def mixer(h_lat, h_ctx, rows, cols, need_ctx, w_in, sink, q_norm, w_uq, kv_norm, w_ukv, w_o_a, w_o_b, w_out):
    qa, ka, va, qn, qr, kn, kr, vb, ga, gb = project(h_lat, w_in, q_norm, w_uq, kv_norm, w_ukv)
    qa_c, ka_c, va_c, qn_c, qr_c, kn_c, kr_c, vb_c, ga_c, gb_c = project(h_ctx, w_in, q_norm, w_uq, kv_norm, w_ukv)
    qa = rope_2d(qa, rows, cols)
    ka = rope_2d(ka, rows, cols)
    qr = rope_2d(qr, rows, cols)
    kr = rope_2d(kr[:, :, None, :], rows, cols)[:, :, 0, :]
    y_a = window_gqa_latent(qa, ka, va, ka_c, va_c, sink)
    y_b = mla_latent(qn, qr, kn, kr, vb, kn_c, kr_c, vb_c)
    out_lat = merge(y_a, y_b, ga, gb, w_o_a, w_o_b, w_out)
    if not need_ctx:
        return out_lat, None
    y_a_c = gqa_context(qa_c, ka_c, va_c, sink)
    y_b_c = mla_context(qn_c, qr_c, kn_c, kr_c, vb_c)
    out_ctx = merge(y_a_c, y_b_c, ga_c, gb_c, w_o_a, w_o_b, w_out)
    return out_lat, out_ctx


def grouped_moe(h, w_router, router_bias, w_gate, w_up, w_down):
    T = h.shape[0]
    scores = jax.nn.sigmoid(h.astype(jnp.float32) @ w_router.astype(jnp.float32))
    sel = scores + router_bias.astype(jnp.float32)
    grp_score = lax.top_k(sel.reshape(T, N_GROUPS, EXPERTS_PER_GROUP), TOP_K)[0].sum(-1)
    best = jnp.argmax(grp_score, axis=-1)
    in_group = (jnp.arange(N_EXPERTS) // EXPERTS_PER_GROUP)[None, :] == best[:, None]
    _, idx = lax.top_k(jnp.where(in_group, sel, -jnp.inf), TOP_K)
    w = jnp.take_along_axis(scores, idx, axis=-1)
    w = w / jnp.sum(w, axis=-1, keepdims=True)
    gates = jnp.sum(jax.nn.one_hot(idx, N_EXPERTS, dtype=jnp.float32) * w[..., None], axis=1)

    def expert(acc, xs):
        wg, wu, wd, g = xs
        y = (jax.nn.silu(h @ wg) * (h @ wu)) @ wd
        return acc + g[:, None].astype(h.dtype) * y, None

    out, _ = lax.scan(expert, jnp.zeros_like(h), (w_gate, w_up, w_down, gates.T))
    return out


def setup_inputs(seed: int = 0) -> dict:
    key = jax.random.key(seed)
    ks = jax.random.split(key, 24)
    D = D_MODEL

    def nrm(k, shape, scale):
        return jax.random.normal(k, shape, jnp.float32) * scale

    return {
        'x': nrm(ks[0], (BATCH, SEQ, D), 1.0),
        'c': nrm(ks[1], (BATCH, D), 1.0),
        'ctx': nrm(ks[2], (BATCH, CTX_LEN, D), 1.0),
        'c_ctx': nrm(ks[3], (D,), 1.0),
        'w_mod': nrm(ks[4], (DEPTH, D, 6 * D), 0.5 * D ** -0.5),
        'b_mod': nrm(ks[5], (DEPTH, 6 * D), 0.01),
        'norm_mix': 1.0 + nrm(ks[6], (DEPTH, D), 0.02),
        'norm_ffn': 1.0 + nrm(ks[7], (DEPTH, D), 0.02),
        'w_in': nrm(ks[8], (DEPTH, D, D_IN), D ** -0.5),
        'attn_sink': nrm(ks[9], (DEPTH, A_HEADS), 0.5),
        'mla_q_norm': 1.0 + nrm(ks[10], (DEPTH, B_Q_RANK), 0.02),
        'w_uq': nrm(ks[11], (DEPTH, B_Q_RANK, B_HEADS * (B_NOPE_DIM + B_ROPE_DIM)), B_Q_RANK ** -0.5),
        'mla_kv_norm': 1.0 + nrm(ks[12], (DEPTH, B_KV_RANK), 0.02),
        'w_ukv': nrm(ks[13], (DEPTH, B_KV_RANK, B_HEADS * (B_NOPE_DIM + B_V_DIM)), B_KV_RANK ** -0.5),
        'w_o_a': nrm(ks[14], (DEPTH, A_Q_W, D), A_Q_W ** -0.5),
        'w_o_b': nrm(ks[15], (DEPTH, B_OUT_W, D), B_OUT_W ** -0.5),
        'w_out': nrm(ks[16], (DEPTH, D, D), D ** -0.5),
        'w_router': nrm(ks[17], (D, N_EXPERTS), D ** -0.5),
        'router_bias': nrm(ks[18], (N_EXPERTS,), 0.01),
        'w_expert_gate': nrm(ks[19], (DEPTH, N_EXPERTS, D, D_EXPERT), D ** -0.5),
        'w_expert_up': nrm(ks[20], (DEPTH, N_EXPERTS, D, D_EXPERT), D ** -0.5),
        'w_expert_down': nrm(ks[21], (DEPTH, N_EXPERTS, D_EXPERT, D), D_EXPERT ** -0.5),
        'final_norm': 1.0 + nrm(ks[22], (D,), 0.02),
    }


def reference(x, c, ctx, c_ctx, w_mod, b_mod, norm_mix, norm_ffn, w_in, attn_sink, mla_q_norm, w_uq,
              mla_kv_norm, w_ukv, w_o_a, w_o_b, w_out, w_router, router_bias, w_expert_gate, w_expert_up,
              w_expert_down, final_norm):
    B, S, D = x.shape
    ROWS = S // GRID_W
    rows = jnp.repeat(jnp.arange(ROWS), GRID_W)
    cols = jnp.tile(jnp.arange(GRID_W), ROWS)
    silu_c = jax.nn.silu(c)
    silu_c_ctx = jax.nn.silu(c_ctx)
    xc = ctx
    for l in range(DEPTH):
        need_ctx = l < DEPTH - 1
        sh1, sc1, g1, sh2, sc2, g2 = [m[:, None, :] for m in jnp.split(silu_c @ w_mod[l] + b_mod[l], 6, axis=-1)]
        sh1c, sc1c, g1c, sh2c, sc2c, g2c = jnp.split(silu_c_ctx @ w_mod[l] + b_mod[l], 6, axis=-1)
        h_lat = modulate(rmsnorm(x, norm_mix[l]), sh1, sc1)
        h_ctx = modulate(rmsnorm(xc, norm_mix[l]), sh1c, sc1c)
        y_lat, y_ctx = mixer(h_lat, h_ctx, rows, cols, need_ctx, w_in[l], attn_sink[l], mla_q_norm[l], w_uq[l],
                             mla_kv_norm[l], w_ukv[l], w_o_a[l], w_o_b[l], w_out[l])
        x = x + g1 * y_lat
        h_lat = modulate(rmsnorm(x, norm_ffn[l]), sh2, sc2).reshape(B * S, D)
        if need_ctx:
            xc = xc + g1c * y_ctx
            h_ctx = modulate(rmsnorm(xc, norm_ffn[l]), sh2c, sc2c).reshape(-1, D)
            f = grouped_moe(jnp.concatenate([h_lat, h_ctx], axis=0), w_router, router_bias,
                            w_expert_gate[l], w_expert_up[l], w_expert_down[l])
            x = x + g2 * f[:B * S].reshape(B, S, D)
            xc = xc + g2c * f[B * S:].reshape(xc.shape)
        else:
            f = grouped_moe(h_lat, w_router, router_bias, w_expert_gate[l], w_expert_up[l], w_expert_down[l])
            x = x + g2 * f.reshape(B, S, D)
    return rmsnorm(x, final_norm)
```

```python
import functools

import jax
import jax.numpy as jnp
from jax import lax
from jax.experimental import pallas as pl
from jax.experimental.pallas import tpu as pltpu

F32 = jnp.float32
BF16 = jnp.bfloat16

EPS = 1e-6
ROPE_BASE = 10000.0
GRID_W = 64
BLOCK = 128
A_HEADS, A_KV_HEADS, A_HEAD_DIM = 8, 2, 64
A_GROUP = A_HEADS // A_KV_HEADS
WINDOW = 128
B_HEADS, B_NOPE, B_ROPE, B_V = 8, 64, 32, 64
B_Q_RANK, B_KV_RANK = 256, 128
N_EXPERTS, N_GROUPS = 16, 4
EPG = N_EXPERTS // N_GROUPS
D_EXPERT = 512
PAIRS = ((0, 1), (0, 2), (0, 3), (1, 2), (1, 3), (2, 3))
N_BUCKETS = N_GROUPS * len(PAIRS)

LANES = 128
TM = 256
TQ = 256
TE = 256
MOD_ROWS = 24
META_W = LANES
NEG = -1e30
VMEM_LIMIT = 52 * 1024 * 1024

W1_QA, W1_KA, W1_VA, W1_CQ, W1_CKV, W1_KR = 0, 512, 768, 1024, 1280, 1408
W1_COLS = 1536


def _params(sem, vmem=VMEM_LIMIT):
    return pltpu.CompilerParams(dimension_semantics=sem, vmem_limit_bytes=vmem)


def _dot(a, b):
    return jnp.dot(a, b, preferred_element_type=F32)


def _dot_nt(a, b):
    return lax.dot_general(a, b, (((1,), (1,)), ((), ())), preferred_element_type=F32)


def _rms(x, g):
    return x * lax.rsqrt(jnp.mean(x * x, axis=-1, keepdims=True) + EPS) * g


def _lane_tile(t, width):
    return jnp.concatenate([t] * (width // LANES), axis=1)


def _rope(t, tab_ref, shift):
    w = t.shape[-1]
    up = pltpu.roll(t, w - shift, 1)
    dn = pltpu.roll(t, shift, 1)
    return (t * _lane_tile(tab_ref[0], w) + up * _lane_tile(tab_ref[1], w)
            + dn * _lane_tile(tab_ref[2], w))


def _mod_kernel(c_ref, w_ref, b_ref, o_ref):
    c = c_ref[...]
    a = (c * jax.nn.sigmoid(c)).astype(BF16)
    o_ref[0, 0] = _dot(a, w_ref[0].astype(BF16)) + b_ref[0, 0]


def _mod_call(c_all, w_mod, b_mod):
    depth, d, _ = w_mod.shape
    return pl.pallas_call(
        _mod_kernel,
        grid=(depth, 6),
        in_specs=[pl.BlockSpec((MOD_ROWS, d), lambda l, k: (0, 0)),
                  pl.BlockSpec((1, d, d), lambda l, k: (l, 0, k)),
                  pl.BlockSpec((1, 1, 1, d), lambda l, k: (l, k, 0, 0))],
        out_specs=pl.BlockSpec((1, 1, MOD_ROWS, d), lambda l, k: (l, k, 0, 0)),
        out_shape=jax.ShapeDtypeStruct((depth, 6, MOD_ROWS, d), F32),
        compiler_params=_params(("arbitrary", "arbitrary")),
        name="mod",
    )(c_all, w_mod, b_mod.reshape(depth, 6, 1, d))


def _proj_kernel(*refs, has_f):
    if has_f:
        x_ref, f_ref, modp_ref, refs = refs[0], refs[1], refs[2], refs[3:]
    else:
        x_ref, refs = refs[0], refs[1:]
    (mod_ref, nw_ref, ta_ref, tb_ref, w1_ref, wg_ref, qn_ref, wuq_ref, kvn_ref, wukn_ref, wuv_ref, rp_ref), refs = refs[:12], refs[12:]
    if has_f:
        xo_ref, refs = refs[0], refs[1:]
    qa_ref, ka_ref, va_ref, qb_ref, kb_ref, vb_ref, ga_ref, gb_ref = refs

    x = x_ref[...]
    if has_f:
        x = x + modp_ref[0, 5:6, :] * f_ref[...]
        xo_ref[...] = x
    h = (_rms(x, nw_ref[...]) * (1.0 + mod_ref[0, 1:2, :]) + mod_ref[0, 0:1, :]).astype(BF16)

    t = _dot(h, w1_ref[...])
    qa_ref[...] = (_rope(t[:, W1_QA:W1_KA], ta_ref, 16) * (A_HEAD_DIM ** -0.5)).astype(BF16)
    ka_ref[...] = _rope(t[:, W1_KA:W1_VA], ta_ref, 16).astype(BF16)
    va_ref[...] = t[:, W1_VA:W1_CQ].astype(BF16)

    cq = _rms(t[:, W1_CQ:W1_CKV], qn_ref[...]).astype(BF16)
    qb = _rope(_dot(cq, wuq_ref[...]), tb_ref, 8)
    qb_ref[...] = (qb * ((B_NOPE + B_ROPE) ** -0.5)).astype(BF16)

    ckv = _rms(t[:, W1_CKV:W1_KR], kvn_ref[...]).astype(BF16)
    vb_ref[...] = _dot(ckv, wuv_ref[...]).astype(BF16)
    kr = t[:, W1_KR:W1_COLS]
    kr_hi = kr.astype(BF16)
    kr_lo = (kr - kr_hi.astype(F32)).astype(BF16)
    kb = _dot(ckv, wukn_ref[...]) + _dot(kr_hi, rp_ref[...]) + _dot(kr_lo, rp_ref[...])
    kb_ref[...] = _rope(kb, tb_ref, 8).astype(BF16)

    g = jax.nn.sigmoid(_dot(h, wg_ref[...]))
    d = ga_ref.shape[-1]
    ga_ref[...] = g[:, :d].astype(BF16)
    gb_ref[...] = g[:, d:].astype(BF16)


def _proj_call(x, f, modp, mod, nw, tab_a, tab_b, w, *, n_lat, seq):
    n, d = x.shape
    nt = n // TM
    n_lat_t = n_lat // TM
    tps = seq // TM
    ctx_row = n_lat // seq
    has_f = f is not None

    def tok(i):
        return (i, 0)

    def modi(i):
        return (jnp.where(i < n_lat_t, i // tps, ctx_row), 0, 0)

    def tabi(i):
        return (0, jnp.where(i < n_lat_t, i % tps, tps), 0)

    def full(a):
        return pl.BlockSpec(a.shape, lambda i: (0,) * a.ndim)

    mod_spec = pl.BlockSpec((1, 8, d), modi)
    in_specs = [pl.BlockSpec((TM, d), tok)]
    args = [x]
    if has_f:
        in_specs += [pl.BlockSpec((TM, d), tok), mod_spec]
        args += [f, modp]
    in_specs += [mod_spec, full(nw), pl.BlockSpec((3, TM, LANES), tabi), pl.BlockSpec((3, TM, LANES), tabi)]
    args += [mod, nw, tab_a, tab_b]
    for k in ("w1", "wg", "qn", "wuq", "kvn", "wukn", "wuv", "rp"):
        in_specs.append(full(w[k]))
        args.append(w[k])

    widths = (512, 256, 256, 1024, 1024, 512, d, d)
    out_shape = [jax.ShapeDtypeStruct((n, wd), BF16) for wd in widths]
    out_specs = [pl.BlockSpec((TM, wd), tok) for wd in widths]
    if has_f:
        out_shape.insert(0, jax.ShapeDtypeStruct((n, d), F32))
        out_specs.insert(0, pl.BlockSpec((TM, d), tok))
    return pl.pallas_call(
        functools.partial(_proj_kernel, has_f=has_f),
        grid=(nt,), in_specs=in_specs, out_specs=out_specs, out_shape=out_shape,
        compiler_params=_params(("arbitrary",)),
        name="proj",
    )(*args)


def _window_kernel(sink_ref, q_ref, kp_ref, kc_ref, kn_ref, kx_ref, vp_ref, vc_ref, vn_ref, vx_ref, o_ref, *, nb):
    n = pl.program_id(1)
    rows = A_GROUP * BLOCK
    lo = lax.broadcasted_iota(jnp.int32, (BLOCK, LANES), 1) < A_HEAD_DIM
    r = lax.broadcasted_iota(jnp.int32, (rows, 3 * BLOCK), 0) % BLOCK
    c = lax.broadcasted_iota(jnp.int32, (rows, 3 * BLOCK), 1)
    dist = c - r
    valid = ((dist >= BLOCK - WINDOW) & (dist <= BLOCK + WINDOW) & (n < nb)
             & ((c >= BLOCK) | (n > 0)) & ((c < 2 * BLOCK) | (n < nb - 1)))
    row_head = lax.broadcasted_iota(jnp.int32, (rows, 1), 0) // BLOCK
    q = q_ref[...]
    zero = jnp.zeros((BLOCK, LANES), BF16)
    for k in range(A_KV_HEADS):
        ksl = slice(k * LANES, (k + 1) * LANES)
        parts = []
        for j in range(A_GROUP // 2):
            blk = q[:, k * 2 * LANES + j * LANES:k * 2 * LANES + (j + 1) * LANES]
            parts += [jnp.where(lo, blk, zero), jnp.where(lo, zero, blk)]
        qs = jnp.concatenate(parts, axis=0)
        k_lat = jnp.concatenate([kp_ref[:, ksl], kc_ref[:, ksl], kn_ref[:, ksl]], axis=0)
        v_lat = jnp.concatenate([vp_ref[:, ksl], vc_ref[:, ksl], vn_ref[:, ksl]], axis=0)
        s_l = jnp.where(valid, _dot_nt(qs, k_lat), NEG)
        s_x = _dot_nt(qs, kx_ref[:, ksl])
        sink = jnp.zeros((rows, 1), F32)
        for g in range(A_GROUP):
            sink = jnp.where(row_head == g, sink_ref[k * A_GROUP + g], sink)
        m = jnp.maximum(jnp.maximum(jnp.max(s_l, axis=-1, keepdims=True), jnp.max(s_x, axis=-1, keepdims=True)), sink)
        e_l = jnp.exp(s_l - m)
        e_x = jnp.exp(s_x - m)
        den = jnp.sum(e_l, axis=-1, keepdims=True) + jnp.sum(e_x, axis=-1, keepdims=True) + jnp.exp(sink - m)
        o = (_dot(e_l.astype(BF16), v_lat) + _dot(e_x.astype(BF16), vx_ref[:, ksl])) / den
        for j in range(A_GROUP // 2):
            even = o[(2 * j) * BLOCK:(2 * j + 1) * BLOCK]
            odd = o[(2 * j + 1) * BLOCK:(2 * j + 2) * BLOCK]
            o_ref[:, k * 2 * LANES + j * LANES:k * 2 * LANES + (j + 1) * LANES] = jnp.where(lo, even, odd).astype(BF16)


def _window_call(sink, qa, ka, va, *, batch, seq, ctx_len, need_ctx):
    n = qa.shape[0] if need_ctx else batch * seq
    nb = seq // BLOCK
    ncb = ctx_len // BLOCK
    lat_blocks = batch * nb
    nq = nb + (ncb if need_ctx else 0)

    def qi(b, i):
        return (jnp.where(i < nb, b * nb + i, lat_blocks + b * ncb + (i - nb)), 0)

    def ki(off):
        return lambda b, i: (b * nb + jnp.clip(i + off, 0, nb - 1), 0)

    def xi(b, i):
        return (batch * seq // ctx_len + b, 0)

    kv_specs = [pl.BlockSpec((BLOCK, 2 * LANES), ki(-1)), pl.BlockSpec((BLOCK, 2 * LANES), ki(0)),
                pl.BlockSpec((BLOCK, 2 * LANES), ki(1)), pl.BlockSpec((ctx_len, 2 * LANES), xi)]
    return pl.pallas_call(
        functools.partial(_window_kernel, nb=nb),
        grid=(batch, nq),
        in_specs=[pl.BlockSpec(memory_space=pltpu.SMEM), pl.BlockSpec((BLOCK, 4 * LANES), qi)] + kv_specs + kv_specs,
        out_specs=pl.BlockSpec((BLOCK, 4 * LANES), qi),
        out_shape=jax.ShapeDtypeStruct((n, 4 * LANES), BF16),
        compiler_params=_params(("arbitrary", "arbitrary")),
        name="window",
    )(sink, qa, ka, ka, ka, ka, va, va, va, va)


def _mla_kernel(q_ref, kl_ref, kx_ref, vl_ref, vx_ref, o_ref, *, nql):
    j = pl.program_id(2)
    lo = lax.broadcasted_iota(jnp.int32, (TQ, LANES), 1) < B_V

    def head(hh, with_lat):
        sl = slice(hh * LANES, (hh + 1) * LANES)
        q = q_ref[:, sl]
        s_x = _dot_nt(q, kx_ref[:, sl])
        m = jnp.max(s_x, axis=-1, keepdims=True)
        if with_lat:
            s_l = _dot_nt(q, kl_ref[:, sl])
            m = jnp.maximum(m, jnp.max(s_l, axis=-1, keepdims=True))
        e_x = jnp.exp(s_x - m)
        den = jnp.sum(e_x, axis=-1, keepdims=True)
        o = _dot(e_x.astype(BF16), vx_ref[...])
        if with_lat:
            e_l = jnp.exp(s_l - m)
            den = den + jnp.sum(e_l, axis=-1, keepdims=True)
            o = o + _dot(e_l.astype(BF16), vl_ref[...])
        return o / den

    def run(with_lat):
        o_ref[...] = jnp.where(lo, head(0, with_lat), head(1, with_lat)).astype(BF16)

    @pl.when(j < nql)
    def _():
        run(True)

    @pl.when(j >= nql)
    def _():
        run(False)


def _mla_call(qb, kb, vb, *, batch, seq, ctx_len, need_ctx):
    n = qb.shape[0] if need_ctx else batch * seq
    assert ctx_len == TQ
    nql = seq // TQ
    nq = nql + (1 if need_ctx else 0)
    ctx0 = batch * seq // ctx_len
    pairs = B_HEADS // 2

    def qi(b, p, j):
        return (jnp.where(j < nql, b * nql + j, ctx0 + b), p)

    return pl.pallas_call(
        functools.partial(_mla_kernel, nql=nql),
        grid=(batch, pairs, nq),
        in_specs=[pl.BlockSpec((TQ, 2 * LANES), qi),
                  pl.BlockSpec((seq, 2 * LANES), lambda b, p, j: (b, p)),
                  pl.BlockSpec((ctx_len, 2 * LANES), lambda b, p, j: (ctx0 + b, p)),
                  pl.BlockSpec((seq, LANES), lambda b, p, j: (b, p)),
                  pl.BlockSpec((ctx_len, LANES), lambda b, p, j: (ctx0 + b, p))],
        out_specs=pl.BlockSpec((TQ, LANES), qi),
        out_shape=jax.ShapeDtypeStruct((n, pairs * LANES), BF16),
        compiler_params=_params(("arbitrary", "arbitrary", "arbitrary")),
        name="mla",
    )(qb, kb, kb, vb, vb)


def _top2(v):
    i1 = jnp.zeros_like(v[0])
    m1 = v[0]
    for i in range(1, EPG):
        u = v[i] > m1
        i1 = jnp.where(u, float(i), i1)
        m1 = jnp.where(u, v[i], m1)
    i2 = jnp.zeros_like(v[0])
    m2 = jnp.full_like(v[0], -jnp.inf)
    for i in range(EPG):
        cand = jnp.where(i1 == float(i), -jnp.inf, v[i])
        u = cand > m2
        i2 = jnp.where(u, float(i), i2)
        m2 = jnp.where(u, cand, m2)
    return i1, i2, m1, m2


def _pick(idx, vals):
    out = vals[0]
    for i in range(1, len(vals)):
        out = jnp.where(idx == float(i), vals[i], out)
    return out


def _merge_kernel(x_ref, ya_ref, yb_ref, ga_ref, gb_ref, mod_ref, nw_ref, woa_ref, wob_ref, wout_ref, wr_ref, rb_ref,
                  xo_ref, h_ref):
    d = x_ref.shape[-1]
    a = _dot(ya_ref[...], woa_ref[...])
    b = _dot(yb_ref[...], wob_ref[...])
    mix = (ga_ref[...].astype(F32) * a + gb_ref[...].astype(F32) * b).astype(BF16)
    x = x_ref[...] + mod_ref[0, 2:3, :] * _dot(mix, wout_ref[...])
    xo_ref[...] = x
    h = _rms(x, nw_ref[...]) * (1.0 + mod_ref[0, 4:5, :]) + mod_ref[0, 3:4, :]
    h_ref[:, :d] = h

    logits = jnp.dot(h, wr_ref[...], precision=lax.Precision.HIGHEST, preferred_element_type=F32)
    sc = jax.nn.sigmoid(logits.T[:N_EXPERTS, :])
    sel = sc + rb_ref[...]
    sel_rows = [sel[e:e + 1, :] for e in range(N_EXPERTS)]
    sc_rows = [sc[e:e + 1, :] for e in range(N_EXPERTS)]
    best = jnp.zeros_like(sel_rows[0])
    best_v = None
    for g in range(N_GROUPS):
        _, _, m1, m2 = _top2(sel_rows[g * EPG:(g + 1) * EPG])
        gv = m1 + m2
        if best_v is None:
            best_v = gv
        else:
            u = gv > best_v
            best = jnp.where(u, float(g), best)
            best_v = jnp.where(u, gv, best_v)
    sel_g = [_pick(best, [sel_rows[g * EPG + i] for g in range(N_GROUPS)]) for i in range(EPG)]
    sc_g = [_pick(best, [sc_rows[g * EPG + i] for g in range(N_GROUPS)]) for i in range(EPG)]
    i1, i2, _, _ = _top2(sel_g)
    s1 = _pick(i1, sc_g)
    s2 = _pick(i2, sc_g)
    tot = s1 + s2
    first_low = i1 < i2
    e_lo = jnp.where(first_low, i1, i2)
    e_hi = jnp.where(first_low, i2, i1)
    w_lo = jnp.where(first_low, s1, s2) / tot
    w_hi = jnp.where(first_low, s2, s1) / tot
    pid = jnp.where(e_lo == 0.0, e_hi - 1.0, jnp.where(e_lo == 1.0, e_hi + 1.0, 5.0))
    bucket = best * float(len(PAIRS)) + pid
    t = sel.shape[1]
    meta = jnp.concatenate([w_lo, w_hi, bucket, jnp.zeros((META_W - 3, t), F32)], axis=0)
    h_ref[:, d:] = meta.T


def _merge_call(x, ya, yb, ga, gb, mod, nw, w, wr, rb, *, n_rows, n_lat, seq):
    d = x.shape[1]
    nt = n_rows // TM
    n_lat_t = n_lat // TM
    tps = seq // TM
    ctx_row = n_lat // seq

    def tok(i):
        return (i, 0)

    def full(a):
        return pl.BlockSpec(a.shape, lambda i: (0,) * a.ndim)

    return pl.pallas_call(
        _merge_kernel,
        grid=(nt,),
        in_specs=[pl.BlockSpec((TM, d), tok), pl.BlockSpec((TM, 512), tok), pl.BlockSpec((TM, 512), tok),
                  pl.BlockSpec((TM, d), tok), pl.BlockSpec((TM, d), tok),
                  pl.BlockSpec((1, 8, d), lambda i: (jnp.where(i < n_lat_t, i // tps, ctx_row), 0, 0)),
                  full(nw), full(w["woa"]), full(w["wob"]), full(w["wout"]), full(wr), full(rb)],
        out_specs=[pl.BlockSpec((TM, d), tok), pl.BlockSpec((TM, d + META_W), tok)],
        out_shape=[jax.ShapeDtypeStruct((n_rows, d), F32), jax.ShapeDtypeStruct((n_rows, d + META_W), F32)],
        compiler_params=_params(("arbitrary",)),
        name="merge",
    )(x, ya, yb, ga, gb, mod, nw, w["woa"], w["wob"], w["wout"], wr, rb)


def _moe_kernel(tok_ref, e1_ref, e2_ref, nv_ref, h_hbm, wg1_ref, wu1_ref, wd1_ref, wg2_ref, wu2_ref, wd2_ref,
                f_hbm, hbuf, obuf, gsem, ssem, *, n_tok, n_tiles):
    t = pl.program_id(0)
    slot = t % 2
    d = obuf.shape[-1]

    def gather_row(tile, s, r):
        tk = jnp.minimum(tok_ref[tile * TE + r], n_tok - 1)
        return pltpu.make_async_copy(h_hbm.at[pl.ds(tk, 1)], hbuf.at[s, pl.ds(r, 1)], gsem.at[s])

    def scatter_row(tile, s, r):
        return pltpu.make_async_copy(obuf.at[s, pl.ds(r, 1)], f_hbm.at[pl.ds(tok_ref[tile * TE + r], 1)], ssem.at[s])

    def start_gather(tile, s):
        def body(r, carry):
            gather_row(tile, s, r).start()
            return carry
        lax.fori_loop(0, TE, body, 0, unroll=8)

    def wait_gather(s):
        pltpu.make_async_copy(h_hbm.at[pl.ds(0, TE)], hbuf.at[s], gsem.at[s]).wait()

    def start_scatter(tile, s):
        def body(r, carry):
            scatter_row(tile, s, r).start()
            return carry
        lax.fori_loop(0, TE, body, 0, unroll=8)

    def wait_scatter(s):
        pltpu.make_async_copy(obuf.at[s], f_hbm.at[pl.ds(0, TE)], ssem.at[s]).wait()

    live = nv_ref[t] > 0
    next_live = jnp.logical_and(t + 1 < n_tiles, nv_ref[jnp.minimum(t + 1, n_tiles - 1)] > 0)

    @pl.when(t == 0)
    def _():
        obuf[0] = jnp.zeros(obuf.shape[1:], F32)
        for half in range(2):
            fill = pltpu.make_async_copy(obuf.at[0], f_hbm.at[pl.ds(n_tok + half * TE, TE)], ssem.at[0])
            fill.start()
            fill.wait()

    @pl.when(jnp.logical_and(live, t == 0))
    def _():
        start_gather(0, 0)

    @pl.when(jnp.logical_and(live, next_live))
    def _():
        start_gather(t + 1, 1 - slot)

    @pl.when(live)
    def _():
        wait_gather(slot)

    @pl.when(jnp.logical_and(live, t >= 2))
    def _():
        wait_scatter(slot)

    @pl.when(live)
    def _():
        hx = hbuf[slot]
        h = hx[:, :d].astype(BF16)
        w_lo = hx[:, d:d + 1]
        w_hi = hx[:, d + 1:d + 2]

        def expert(wg_ref, wu_ref, wd_ref):
            hg = _dot(h, wg_ref[0, 0])
            hu = _dot(h, wu_ref[0, 0])
            act = (hg * jax.nn.sigmoid(hg) * hu).astype(BF16)
            return _dot(act, wd_ref[0, 0])

        obuf[slot] = w_lo * expert(wg1_ref, wu1_ref, wd1_ref) + w_hi * expert(wg2_ref, wu2_ref, wd2_ref)
        start_scatter(t, slot)

    @pl.when(jnp.logical_and(live, jnp.logical_not(next_live)))
    def _():
        wait_scatter(slot)

        @pl.when(t >= 1)
        def _():
            wait_scatter(1 - slot)


def _moe_call(tok, e1, e2, nv, hext, wg, wu, wd, *, layer, n_tok):
    d = wg.shape[2]
    de = wg.shape[3]
    n_tiles = e1.shape[0]

    def wspec(shape, which):
        return pl.BlockSpec((1, 1) + shape, lambda t, tok, e1, e2, nv: (layer, (e1, e2)[which][t], 0, 0))

    grid_spec = pltpu.PrefetchScalarGridSpec(
        num_scalar_prefetch=4,
        grid=(n_tiles,),
        in_specs=[pl.BlockSpec(memory_space=pl.ANY),
                  wspec((d, de), 0), wspec((d, de), 0), wspec((de, d), 0),
                  wspec((d, de), 1), wspec((d, de), 1), wspec((de, d), 1)],
        out_specs=pl.BlockSpec(memory_space=pl.ANY),
        scratch_shapes=[pltpu.VMEM((2, TE, d + META_W), F32), pltpu.VMEM((2, TE, d), F32),
                        pltpu.SemaphoreType.DMA((2,)), pltpu.SemaphoreType.DMA((2,))],
    )
    return pl.pallas_call(
        functools.partial(_moe_kernel, n_tok=n_tok, n_tiles=n_tiles),
        grid_spec=grid_spec,
        out_shape=jax.ShapeDtypeStruct((n_tok + 2 * TE, d), F32),
        compiler_params=_params(("arbitrary",)),
        name="moe",
    )(tok, e1, e2, nv, hext, wg, wu, wd, wg, wu, wd)


def _route_tables(bucket, n_tok):
    n_slots = n_tok + N_BUCKETS * TE
    n_tiles = n_slots // TE
    order = jnp.argsort(bucket, stable=True).astype(jnp.int32)
    counts = jnp.sum(bucket[:, None] == jnp.arange(N_BUCKETS, dtype=jnp.int32)[None, :], axis=0, dtype=jnp.int32)
    padded = ((counts + TE - 1) // TE) * TE
    pad_end = jnp.cumsum(padded)
    pad_start = pad_end - padded
    cnt_start = jnp.cumsum(counts) - counts
    tile_start = jnp.arange(n_tiles, dtype=jnp.int32) * TE
    tb = jnp.sum(tile_start[:, None] >= pad_end[None, :], axis=1, dtype=jnp.int32)
    used = tb < N_BUCKETS
    tbc = jnp.minimum(tb, N_BUCKETS - 1)
    nv = jnp.where(used, jnp.clip(counts[tbc] - (tile_start - pad_start[tbc]), 0, TE), 0).astype(jnp.int32)
    last_used = jnp.maximum(jnp.sum(used.astype(jnp.int32)) - 1, 0)
    tbe = jnp.where(used, tbc, tbc[last_used])
    pair_lo = jnp.array([p[0] for p in PAIRS], jnp.int32)
    pair_hi = jnp.array([p[1] for p in PAIRS], jnp.int32)
    e1 = (tbe // len(PAIRS)) * EPG + pair_lo[tbe % len(PAIRS)]
    e2 = (tbe // len(PAIRS)) * EPG + pair_hi[tbe % len(PAIRS)]
    slot = jnp.arange(n_slots, dtype=jnp.int32)
    stile = slot // TE
    srow = slot % TE
    sb = tbc[stile]
    within = slot - pad_start[sb]
    real = jnp.logical_and(used[stile], within < counts[sb])
    src = order[jnp.clip(cnt_start[sb] + within, 0, n_tok - 1)]
    dummy = n_tok + (stile % 2) * TE + srow
    tok = jnp.where(real, src, dummy).astype(jnp.int32)
    return tok, e1.astype(jnp.int32), e2.astype(jnp.int32), nv


def _final_kernel(x_ref, f_ref, mod_ref, nw_ref, o_ref):
    x = x_ref[...] + mod_ref[0, 5:6, :] * f_ref[...]
    o_ref[...] = _rms(x, nw_ref[...])


def _final_call(x, f, mod, nw, *, n_lat, seq):
    d = x.shape[1]
    tps = seq // TM
    return pl.pallas_call(
        _final_kernel,
        grid=(n_lat // TM,),
        in_specs=[pl.BlockSpec((TM, d), lambda i: (i, 0)), pl.BlockSpec((TM, d), lambda i: (i, 0)),
                  pl.BlockSpec((1, 8, d), lambda i: (i // tps, 0, 0)), pl.BlockSpec((1, d), lambda i: (0, 0))],
        out_specs=pl.BlockSpec((TM, d), lambda i: (i, 0)),
        out_shape=jax.ShapeDtypeStruct((n_lat, d), F32),
        compiler_params=_params(("arbitrary",)),
        name="final",
    )(x, f, mod, nw)


def _rope_table(pos_r, pos_c, dim, lane_of):
    d = dim // 2
    half = d // 2
    lane = jnp.arange(LANES)
    rl = lane_of(lane)
    is_rope = rl >= 0
    rl = jnp.maximum(rl, 0)
    use_col = rl >= d
    j = rl % half
    first = (rl % d) < half
    inv = ROPE_BASE ** (-(2.0 * j.astype(F32)) / d)
    pos = jnp.where(use_col[None, :], pos_c.astype(F32)[:, None], pos_r.astype(F32)[:, None])
    ang = pos * inv[None, :]
    cos = jnp.where(is_rope[None, :], jnp.cos(ang), 1.0)
    sin = jnp.where(is_rope[None, :], jnp.sin(ang), 0.0)
    return jnp.stack([cos, jnp.where(first[None, :], -sin, 0.0), jnp.where(first[None, :], 0.0, sin)]).astype(F32)


def _tables(seq):
    t = jnp.arange(seq)
    rows, cols = t // GRID_W, t % GRID_W
    tab_a = _rope_table(rows, cols, A_HEAD_DIM, lambda lane: lane % A_HEAD_DIM)
    tab_b = _rope_table(rows, cols, B_ROPE,
                        lambda lane: jnp.where((lane >= B_NOPE) & (lane < B_NOPE + B_ROPE), lane - B_NOPE, -1))
    ident = jnp.stack([jnp.ones((TM, LANES), F32), jnp.zeros((TM, LANES), F32), jnp.zeros((TM, LANES), F32)])
    return jnp.concatenate([tab_a, ident], axis=1), jnp.concatenate([tab_b, ident], axis=1)


def _layer_weights(w_in, w_uq, w_ukv, q_norm, kv_norm, w_o_a, w_o_b, w_out):
    d = w_in.shape[0]
    o = 0
    qa_w = w_in[:, o:o + 512]; o += 512
    ka_w = w_in[:, o:o + 128]; o += 128
    va_w = w_in[:, o:o + 128]; o += 128
    cq_w = w_in[:, o:o + B_Q_RANK]; o += B_Q_RANK
    ckv_w = w_in[:, o:o + B_KV_RANK]; o += B_KV_RANK
    kr_w = w_in[:, o:o + B_ROPE]; o += B_ROPE
    g_w = w_in[:, o:]

    def dup(wk):
        wk = wk.reshape(d, A_KV_HEADS, 1, A_HEAD_DIM)
        return jnp.broadcast_to(wk, (d, A_KV_HEADS, 2, A_HEAD_DIM)).reshape(d, A_KV_HEADS * 2 * A_HEAD_DIM)

    w1 = jnp.concatenate([qa_w, dup(ka_w), dup(va_w), cq_w, ckv_w, jnp.pad(kr_w, ((0, 0), (0, LANES - B_ROPE)))], axis=1)
    wuq = jnp.pad(w_uq.reshape(B_Q_RANK, B_HEADS, B_NOPE + B_ROPE),
                  ((0, 0), (0, 0), (0, LANES - B_NOPE - B_ROPE))).reshape(B_Q_RANK, B_HEADS * LANES)
    ukv = w_ukv.reshape(B_KV_RANK, B_HEADS, B_NOPE + B_V)
    wukn = jnp.pad(ukv[:, :, :B_NOPE], ((0, 0), (0, 0), (0, LANES - B_NOPE))).reshape(B_KV_RANK, B_HEADS * LANES)
    wuv = ukv[:, :, B_NOPE:].reshape(B_KV_RANK, B_HEADS * B_V)
    src = jnp.arange(LANES)[:, None]
    dst = jnp.arange(B_HEADS * LANES)[None, :]
    rp = ((src < B_ROPE) & (dst % LANES == src + B_NOPE)).astype(BF16)
    return dict(w1=w1.astype(BF16), wg=g_w.astype(BF16), qn=q_norm.reshape(1, -1), wuq=wuq.astype(BF16),
                kvn=kv_norm.reshape(1, -1), wukn=wukn.astype(BF16), wuv=wuv.astype(BF16), rp=rp,
                woa=w_o_a.astype(BF16), wob=w_o_b.astype(BF16), wout=w_out.astype(BF16))


def kernel(x, c, ctx, c_ctx, w_mod, b_mod, norm_mix, norm_ffn, w_in, attn_sink, mla_q_norm, w_uq, mla_kv_norm, w_ukv,
           w_o_a, w_o_b, w_out, w_router, router_bias, w_expert_gate, w_expert_up, w_expert_down, final_norm):
    batch, seq, d = x.shape
    ctx_len = ctx.shape[1]
    depth = w_mod.shape[0]
    n_lat = batch * seq
    n_all = n_lat + batch * ctx_len
    assert seq % TM == 0 and ctx_len % TM == 0 and seq % GRID_W == 0 and batch + 1 <= MOD_ROWS

    c_all = jnp.concatenate([c, c_ctx[None, :], jnp.zeros((MOD_ROWS - batch - 1, d), F32)], axis=0)
    mod = _mod_call(c_all, w_mod, b_mod)
    mod = jnp.pad(mod.transpose(0, 2, 1, 3), ((0, 0), (0, 0), (0, 2), (0, 0)))

    tab_a, tab_b = _tables(seq)
    wr = jnp.pad(w_router.astype(F32), ((0, 0), (0, LANES - N_EXPERTS)))
    rb = router_bias.astype(F32).reshape(N_EXPERTS, 1)
    wg_all = w_expert_gate.astype(BF16)
    wu_all = w_expert_up.astype(BF16)
    wd_all = w_expert_down.astype(BF16)

    xs = jnp.concatenate([x.reshape(n_lat, d), ctx.reshape(batch * ctx_len, d)], axis=0)
    f = None
    for l in range(depth):
        need_ctx = l < depth - 1
        w = _layer_weights(w_in[l], w_uq[l], w_ukv[l], mla_q_norm[l], mla_kv_norm[l], w_o_a[l], w_o_b[l], w_out[l])
        outs = _proj_call(xs, f, mod[l - 1] if l else None, mod[l], norm_mix[l].reshape(1, d), tab_a, tab_b, w,
                          n_lat=n_lat, seq=seq)
        if l:
            xs, outs = outs[0], outs[1:]
        qa, ka, va, qb, kb, vb, ga, gb = outs
        ya = _window_call(attn_sink[l].astype(F32), qa, ka, va, batch=batch, seq=seq, ctx_len=ctx_len, need_ctx=need_ctx)
        yb = _mla_call(qb, kb, vb, batch=batch, seq=seq, ctx_len=ctx_len, need_ctx=need_ctx)
        n_rows = n_all if need_ctx else n_lat
        xs, hext = _merge_call(xs, ya, yb, ga, gb, mod[l], norm_ffn[l].reshape(1, d), w, wr, rb,
                               n_rows=n_rows, n_lat=n_lat, seq=seq)
        bucket = hext[:, d + 2].astype(jnp.int32)
        tok, e1, e2, nv = _route_tables(bucket, n_rows)
        f = _moe_call(tok, e1, e2, nv, hext, wg_all, wu_all, wd_all, layer=l, n_tok=n_rows)
    out = _final_call(xs, f, mod[depth - 1], final_norm.reshape(1, d), n_lat=n_lat, seq=seq)
    return out.reshape(batch, seq, d)
```

```python
import functools

import jax
import jax.numpy as jnp
from jax import lax
from jax.experimental import pallas as pl
from jax.experimental.pallas import tpu as pltpu

F32 = jnp.float32
BF16 = jnp.bfloat16

EPS = 1e-6
ROPE_BASE = 10000.0
GRID_W = 64
BLOCK = 128
A_HEADS, A_KV_HEADS, A_HEAD_DIM = 8, 2, 64
A_GROUP = A_HEADS // A_KV_HEADS
WINDOW = 128
B_HEADS, B_NOPE, B_ROPE, B_V = 8, 64, 32, 64
B_Q_RANK, B_KV_RANK = 256, 128
N_EXPERTS, N_GROUPS = 16, 4
EPG = N_EXPERTS // N_GROUPS
D_EXPERT = 512
PAIRS = ((0, 1), (0, 2), (0, 3), (1, 2), (1, 3), (2, 3))
N_BUCKETS = N_GROUPS * len(PAIRS)

LANES = 128
SUB = 8
TM = 256
TQ = 256
TE = 256
MOD_ROWS = 24
NEG = -1e30
LOG2E = 1.4426950408889634
VMEM_LIMIT = 52 * 1024 * 1024

W1_QA, W1_KA, W1_VA, W1_CQ, W1_CKV, W1_KR = 0, 512, 768, 1024, 1280, 1408
W1_COLS = 1536


def _params(sem, vmem=VMEM_LIMIT):
    return pltpu.CompilerParams(dimension_semantics=sem, vmem_limit_bytes=vmem)


def _dot(a, b):
    return jnp.dot(a, b, preferred_element_type=F32)


def _dot_nt(a, b):
    return lax.dot_general(a, b, (((1,), (1,)), ((), ())), preferred_element_type=F32)


def _rms(x, g):
    return x * lax.rsqrt(jnp.mean(x * x, axis=-1, keepdims=True) + EPS) * g


def _lane_tile(t, width):
    return jnp.concatenate([t] * (width // LANES), axis=1)


def _rope(t, tab_ref, shift):
    w = t.shape[-1]
    up = pltpu.roll(t, w - shift, 1)
    dn = pltpu.roll(t, shift, 1)
    return (t * _lane_tile(tab_ref[0], w) + up * _lane_tile(tab_ref[1], w)
            + dn * _lane_tile(tab_ref[2], w))


def _load_token_rows(ref, rows):
    return jnp.concatenate([ref[pl.ds(s, rows, stride=SUB), :] for s in range(SUB)], axis=1)


def _store_token_rows(ref, val):
    rows = val.shape[0]
    for s in range(SUB):
        ref[pl.ds(s, rows, stride=SUB), :] = val[:, s * LANES:(s + 1) * LANES]


def _mod_kernel(c_ref, w_ref, b_ref, o_ref):
    c = c_ref[...]
    a = (c * jax.nn.sigmoid(c)).astype(BF16)
    o_ref[0, 0] = _dot(a, w_ref[0].astype(BF16)) + b_ref[0, 0]


def _mod_call(c_all, w_mod, b_mod):
    depth, d, _ = w_mod.shape
    return pl.pallas_call(
        _mod_kernel,
        grid=(depth, 6),
        in_specs=[pl.BlockSpec((MOD_ROWS, d), lambda l, k: (0, 0)),
                  pl.BlockSpec((1, d, d), lambda l, k: (l, 0, k)),
                  pl.BlockSpec((1, 1, 1, d), lambda l, k: (l, k, 0, 0))],
        out_specs=pl.BlockSpec((1, 1, MOD_ROWS, d), lambda l, k: (l, k, 0, 0)),
        out_shape=jax.ShapeDtypeStruct((depth, 6, MOD_ROWS, d), F32),
        compiler_params=_params(("arbitrary", "arbitrary")),
        name="mod",
    )(c_all, w_mod, b_mod.reshape(depth, 6, 1, d))


def _proj_kernel(*refs, has_f):
    if has_f:
        x_ref, f_ref, modp_ref, refs = refs[0], refs[1], refs[2], refs[3:]
    else:
        x_ref, refs = refs[0], refs[1:]
    (mod_ref, nw_ref, ta_ref, tb_ref, w1_ref, wg_ref, qn_ref, wuq_ref, kvn_ref, wukn_ref, wuv_ref,
     rp_ref), refs = refs[:12], refs[12:]
    if has_f:
        xo_ref, refs = refs[0], refs[1:]
    qa_ref, ka_ref, va_ref, qb_ref, kb_ref, vb_ref, ga_ref, gb_ref = refs

    x = x_ref[...]
    if has_f:
        x = x + modp_ref[0, 5:6, :] * _load_token_rows(f_ref, x.shape[0])
        xo_ref[...] = x
    h = (_rms(x, nw_ref[...]) * (1.0 + mod_ref[0, 1:2, :]) + mod_ref[0, 0:1, :]).astype(BF16)

    t = _dot(h, w1_ref[...])
    qa_ref[...] = (_rope(t[:, W1_QA:W1_KA], ta_ref, 16) * (A_HEAD_DIM ** -0.5 * LOG2E)).astype(BF16)
    ka_ref[...] = _rope(t[:, W1_KA:W1_VA], ta_ref, 16).astype(BF16)
    va_ref[...] = t[:, W1_VA:W1_CQ].astype(BF16)

    cq = _rms(t[:, W1_CQ:W1_CKV], qn_ref[...]).astype(BF16)
    qb = _rope(_dot(cq, wuq_ref[...]), tb_ref, 8)
    qb_ref[...] = (qb * ((B_NOPE + B_ROPE) ** -0.5 * LOG2E)).astype(BF16)

    ckv = _rms(t[:, W1_CKV:W1_KR], kvn_ref[...]).astype(BF16)
    vb_ref[...] = _dot(ckv, wuv_ref[...]).astype(BF16)
    kr = t[:, W1_KR:W1_COLS]
    kr_hi = kr.astype(BF16)
    kr_lo = (kr - kr_hi.astype(F32)).astype(BF16)
    kb = _dot(ckv, wukn_ref[...]) + _dot(kr_hi, rp_ref[...]) + _dot(kr_lo, rp_ref[...])
    kb_ref[...] = _rope(kb, tb_ref, 8).astype(BF16)

    g = jax.nn.sigmoid(_dot(h, wg_ref[...]))
    d = ga_ref.shape[-1]
    ga_ref[...] = g[:, :d].astype(BF16)
    gb_ref[...] = g[:, d:].astype(BF16)


def _proj_call(x, f, modp, mod, nw, tab_a, tab_b, w, *, n_lat, seq):
    n, d = x.shape
    nt = n // TM
    n_lat_t = n_lat // TM
    tps = seq // TM
    ctx_row = n_lat // seq
    has_f = f is not None

    def tok(i):
        return (i, 0)

    def modi(i):
        return (jnp.where(i < n_lat_t, i // tps, ctx_row), 0, 0)

    def tabi(i):
        return (0, jnp.where(i < n_lat_t, i % tps, tps), 0)

    def full(a):
        return pl.BlockSpec(a.shape, lambda i: (0,) * a.ndim)

    mod_spec = pl.BlockSpec((1, 8, d), modi)
    in_specs = [pl.BlockSpec((TM, d), tok)]
    args = [x]
    if has_f:
        in_specs += [pl.BlockSpec((TM * SUB, LANES), tok), mod_spec]
        args += [f, modp]
    in_specs += [mod_spec, full(nw), pl.BlockSpec((3, TM, LANES), tabi), pl.BlockSpec((3, TM, LANES), tabi)]
    args += [mod, nw, tab_a, tab_b]
    for k in ("w1", "wg", "qn", "wuq", "kvn", "wukn", "wuv", "rp"):
        in_specs.append(full(w[k]))
        args.append(w[k])

    widths = (512, 256, 256, 1024, 1024, 512, d, d)
    out_shape = [jax.ShapeDtypeStruct((n, wd), BF16) for wd in widths]
    out_specs = [pl.BlockSpec((TM, wd), tok) for wd in widths]
    if has_f:
        out_shape.insert(0, jax.ShapeDtypeStruct((n, d), F32))
        out_specs.insert(0, pl.BlockSpec((TM, d), tok))
    return pl.pallas_call(
        functools.partial(_proj_kernel, has_f=has_f),
        grid=(nt,), in_specs=in_specs, out_specs=out_specs, out_shape=out_shape,
        compiler_params=_params(("arbitrary",)),
        name="proj",
    )(*args)


def _window_kernel(sink_ref, q_ref, kp_ref, kc_ref, kn_ref, kx_ref, vp_ref, vc_ref, vn_ref, vx_ref, o_ref, *, nb):
    n = pl.program_id(1)
    rows = A_GROUP * BLOCK
    lo = lax.broadcasted_iota(jnp.int32, (BLOCK, LANES), 1) < A_HEAD_DIM
    r = lax.broadcasted_iota(jnp.int32, (rows, 3 * BLOCK), 0) % BLOCK
    c = lax.broadcasted_iota(jnp.int32, (rows, 3 * BLOCK), 1)
    dist = c - r
    valid = ((dist >= BLOCK - WINDOW) & (dist <= BLOCK + WINDOW) & (n < nb)
             & ((c >= BLOCK) | (n > 0)) & ((c < 2 * BLOCK) | (n < nb - 1)))
    row_head = lax.broadcasted_iota(jnp.int32, (rows, 1), 0) // BLOCK
    q = q_ref[...]
    zero = jnp.zeros((BLOCK, LANES), BF16)
    for k in range(A_KV_HEADS):
        ksl = slice(k * LANES, (k + 1) * LANES)
        parts = []
        for j in range(A_GROUP // 2):
            blk = q[:, k * 2 * LANES + j * LANES:k * 2 * LANES + (j + 1) * LANES]
            parts += [jnp.where(lo, blk, zero), jnp.where(lo, zero, blk)]
        qs = jnp.concatenate(parts, axis=0)
        k_lat = jnp.concatenate([kp_ref[:, ksl], kc_ref[:, ksl], kn_ref[:, ksl]], axis=0)
        v_lat = jnp.concatenate([vp_ref[:, ksl], vc_ref[:, ksl], vn_ref[:, ksl]], axis=0)
        s_l = jnp.where(valid, _dot_nt(qs, k_lat), NEG)
        s_x = _dot_nt(qs, kx_ref[:, ksl])
        sink = jnp.zeros((rows, 1), F32)
        for g in range(A_GROUP):
            sink = jnp.where(row_head == g, sink_ref[k * A_GROUP + g] * LOG2E, sink)
        m = jnp.maximum(jnp.maximum(jnp.max(s_l, axis=-1, keepdims=True), jnp.max(s_x, axis=-1, keepdims=True)), sink)
        e_l = jnp.exp2(s_l - m)
        e_x = jnp.exp2(s_x - m)
        den = jnp.sum(e_l, axis=-1, keepdims=True) + jnp.sum(e_x, axis=-1, keepdims=True) + jnp.exp2(sink - m)
        o = (_dot(e_l.astype(BF16), v_lat) + _dot(e_x.astype(BF16), vx_ref[:, ksl])) / den
        for j in range(A_GROUP // 2):
            even = o[(2 * j) * BLOCK:(2 * j + 1) * BLOCK]
            odd = o[(2 * j + 1) * BLOCK:(2 * j + 2) * BLOCK]
            o_ref[:, k * 2 * LANES + j * LANES:k * 2 * LANES + (j + 1) * LANES] = jnp.where(lo, even, odd).astype(BF16)


def _window_call(sink, qa, ka, va, *, batch, seq, ctx_len, need_ctx):
    n = qa.shape[0] if need_ctx else batch * seq
    nb = seq // BLOCK
    ncb = ctx_len // BLOCK
    lat_blocks = batch * nb
    nq = nb + (ncb if need_ctx else 0)

    def qi(b, i):
        return (jnp.where(i < nb, b * nb + i, lat_blocks + b * ncb + (i - nb)), 0)

    def ki(off):
        return lambda b, i: (b * nb + jnp.clip(i + off, 0, nb - 1), 0)

    def xi(b, i):
        return (batch * seq // ctx_len + b, 0)

    kv_specs = [pl.BlockSpec((BLOCK, 2 * LANES), ki(-1)), pl.BlockSpec((BLOCK, 2 * LANES), ki(0)),
                pl.BlockSpec((BLOCK, 2 * LANES), ki(1)), pl.BlockSpec((ctx_len, 2 * LANES), xi)]
    return pl.pallas_call(
        functools.partial(_window_kernel, nb=nb),
        grid=(batch, nq),
        in_specs=[pl.BlockSpec(memory_space=pltpu.SMEM), pl.BlockSpec((BLOCK, 4 * LANES), qi)] + kv_specs + kv_specs,
        out_specs=pl.BlockSpec((BLOCK, 4 * LANES), qi),
        out_shape=jax.ShapeDtypeStruct((n, 4 * LANES), BF16),
        compiler_params=_params(("arbitrary", "arbitrary")),
        name="window",
    )(sink, qa, ka, ka, ka, ka, va, va, va, va)


def _mla_kernel(q_ref, kl_ref, kx_ref, vl_ref, vx_ref, o_ref, vaug_ref, *, nql, seq):
    j = pl.program_id(2)
    lo = lax.broadcasted_iota(jnp.int32, (TQ, LANES), 1) < B_V

    @pl.when(j == 0)
    def _():
        one = jnp.ones((1, LANES), BF16)
        for v_ref, start in ((vl_ref, 0), (vx_ref, seq)):
            v = v_ref[...]
            keep = lax.broadcasted_iota(jnp.int32, v.shape, 1) < B_V
            vaug_ref[0, start:start + v.shape[0], :] = jnp.where(keep, v, one)
            vaug_ref[1, start:start + v.shape[0], :] = jnp.where(keep, one, v)

    def head(hh, with_lat):
        sl = slice(hh * LANES, (hh + 1) * LANES)
        q = q_ref[:, sl]
        s_x = _dot_nt(q, kx_ref[:, sl])
        m = jnp.max(s_x, axis=-1, keepdims=True)
        if with_lat:
            s_l = _dot_nt(q, kl_ref[:, sl])
            m = jnp.maximum(m, jnp.max(s_l, axis=-1, keepdims=True))
        o = _dot(jnp.exp2(s_x - m).astype(BF16), vaug_ref[hh, seq:, :])
        if with_lat:
            o = o + _dot(jnp.exp2(s_l - m).astype(BF16), vaug_ref[hh, :seq, :])
        return o / pltpu.roll(o, B_V, 1)

    def run(with_lat):
        o_ref[...] = jnp.where(lo, head(0, with_lat), head(1, with_lat)).astype(BF16)

    @pl.when(j < nql)
    def _():
        run(True)

    @pl.when(j >= nql)
    def _():
        run(False)


def _mla_call(qb, kb, vb, *, batch, seq, ctx_len, need_ctx):
    n = qb.shape[0] if need_ctx else batch * seq
    assert ctx_len == TQ
    nql = seq // TQ
    nq = nql + (1 if need_ctx else 0)
    ctx0 = batch * seq // ctx_len
    pairs = B_HEADS // 2

    def qi(b, p, j):
        return (jnp.where(j < nql, b * nql + j, ctx0 + b), p)

    return pl.pallas_call(
        functools.partial(_mla_kernel, nql=nql, seq=seq),
        grid=(batch, pairs, nq),
        in_specs=[pl.BlockSpec((TQ, 2 * LANES), qi),
                  pl.BlockSpec((seq, 2 * LANES), lambda b, p, j: (b, p)),
                  pl.BlockSpec((ctx_len, 2 * LANES), lambda b, p, j: (ctx0 + b, p)),
                  pl.BlockSpec((seq, LANES), lambda b, p, j: (b, p)),
                  pl.BlockSpec((ctx_len, LANES), lambda b, p, j: (ctx0 + b, p))],
        out_specs=pl.BlockSpec((TQ, LANES), qi),
        out_shape=jax.ShapeDtypeStruct((n, pairs * LANES), BF16),
        scratch_shapes=[pltpu.VMEM((2, seq + ctx_len, LANES), BF16)],
        compiler_params=_params(("arbitrary", "arbitrary", "arbitrary")),
        name="mla",
    )(qb, kb, kb, vb, vb)


def _top2(v):
    i1 = jnp.zeros_like(v[0])
    m1 = v[0]
    for i in range(1, EPG):
        u = v[i] > m1
        i1 = jnp.where(u, float(i), i1)
        m1 = jnp.where(u, v[i], m1)
    i2 = jnp.zeros_like(v[0])
    m2 = jnp.full_like(v[0], -jnp.inf)
    for i in range(EPG):
        cand = jnp.where(i1 == float(i), -jnp.inf, v[i])
        u = cand > m2
        i2 = jnp.where(u, float(i), i2)
        m2 = jnp.where(u, cand, m2)
    return i1, i2, m1, m2


def _pick(idx, vals):
    out = vals[0]
    for i in range(1, len(vals)):
        out = jnp.where(idx == float(i), vals[i], out)
    return out


def _merge_kernel(x_ref, ya_ref, yb_ref, ga_ref, gb_ref, mod_ref, nw_ref, woa_ref, wob_ref, wout_ref, wr2_ref, rb_ref,
                  xo_ref, h_ref, meta_ref):
    a = _dot(ya_ref[...], woa_ref[...])
    b = _dot(yb_ref[...], wob_ref[...])
    mix = (ga_ref[...].astype(F32) * a + gb_ref[...].astype(F32) * b).astype(BF16)
    x = x_ref[...] + mod_ref[0, 2:3, :] * _dot(mix, wout_ref[...])
    xo_ref[...] = x
    h = _rms(x, nw_ref[...]) * (1.0 + mod_ref[0, 4:5, :]) + mod_ref[0, 3:4, :]
    _store_token_rows(h_ref, h)

    h_hi = h.astype(BF16)
    h_lo = (h - h_hi.astype(F32)).astype(BF16)
    l_hi = _dot(h_hi, wr2_ref[...])
    logits = l_hi[:, :LANES] + l_hi[:, LANES:] + _dot(h_lo, wr2_ref[:, :LANES])
    sc = jax.nn.sigmoid(logits.T[:N_EXPERTS, :])
    sel = sc + rb_ref[...]
    sel_rows = [sel[e:e + 1, :] for e in range(N_EXPERTS)]
    sc_rows = [sc[e:e + 1, :] for e in range(N_EXPERTS)]
    best = jnp.zeros_like(sel_rows[0])
    best_v = None
    for g in range(N_GROUPS):
        _, _, m1, m2 = _top2(sel_rows[g * EPG:(g + 1) * EPG])
        gv = m1 + m2
        if best_v is None:
            best_v = gv
        else:
            u = gv > best_v
            best = jnp.where(u, float(g), best)
            best_v = jnp.where(u, gv, best_v)
    sel_g = [_pick(best, [sel_rows[g * EPG + i] for g in range(N_GROUPS)]) for i in range(EPG)]
    sc_g = [_pick(best, [sc_rows[g * EPG + i] for g in range(N_GROUPS)]) for i in range(EPG)]
    i1, i2, _, _ = _top2(sel_g)
    s1 = _pick(i1, sc_g)
    s2 = _pick(i2, sc_g)
    tot = s1 + s2
    first_low = i1 < i2
    e_lo = jnp.where(first_low, i1, i2)
    e_hi = jnp.where(first_low, i2, i1)
    w_lo = jnp.where(first_low, s1, s2) / tot
    w_hi = jnp.where(first_low, s2, s1) / tot
    pid = jnp.where(e_lo == 0.0, e_hi - 1.0, jnp.where(e_lo == 1.0, e_hi + 1.0, 5.0))
    bucket = best * float(len(PAIRS)) + pid
    t = sel.shape[1]
    meta_ref[0] = jnp.concatenate([w_lo, w_hi, bucket, jnp.zeros((SUB - 3, t), F32)], axis=0)


def _merge_call(x, ya, yb, ga, gb, mod, nw, w, wr2, rb, *, n_rows, n_lat, seq):
    d = x.shape[1]
    nt = n_rows // TM
    n_lat_t = n_lat // TM
    tps = seq // TM
    ctx_row = n_lat // seq

    def tok(i):
        return (i, 0)

    def full(a):
        return pl.BlockSpec(a.shape, lambda i: (0,) * a.ndim)

    return pl.pallas_call(
        _merge_kernel,
        grid=(nt,),
        in_specs=[pl.BlockSpec((TM, d), tok), pl.BlockSpec((TM, 512), tok), pl.BlockSpec((TM, 512), tok),
                  pl.BlockSpec((TM, d), tok), pl.BlockSpec((TM, d), tok),
                  pl.BlockSpec((1, 8, d), lambda i: (jnp.where(i < n_lat_t, i // tps, ctx_row), 0, 0)),
                  full(nw), full(w["woa"]), full(w["wob"]), full(w["wout"]), full(wr2), full(rb)],
        out_specs=[pl.BlockSpec((TM, d), tok), pl.BlockSpec((TM * SUB, LANES), tok),
                   pl.BlockSpec((1, SUB, TM), lambda i: (i, 0, 0))],
        out_shape=[jax.ShapeDtypeStruct((n_rows, d), F32), jax.ShapeDtypeStruct((n_rows * SUB, LANES), F32),
                   jax.ShapeDtypeStruct((nt, SUB, TM), F32)],
        compiler_params=_params(("arbitrary",)),
        name="merge",
    )(x, ya, yb, ga, gb, mod, nw, w["woa"], w["wob"], w["wout"], wr2, rb)


def _moe_kernel(tok_ref, e1_ref, e2_ref, nv_ref, wts_ref, h_hbm, wg1_ref, wu1_ref, wd1_ref, wg2_ref, wu2_ref, wd2_ref,
                f_hbm, hbuf, obuf, gsem, ssem, *, n_tok, n_tiles):
    t = pl.program_id(0)
    slot = t % 2
    other = 1 - slot
    live = nv_ref[t] > 0
    prev_live = jnp.logical_and(t >= 1, nv_ref[jnp.maximum(t - 1, 0)] > 0)

    def gather_row(tile, s, r):
        tk = jnp.minimum(tok_ref[tile * TE + r], n_tok - 1)
        return pltpu.make_async_copy(h_hbm.at[pl.ds(tk * SUB, SUB)], hbuf.at[s, pl.ds(r * SUB, SUB)], gsem.at[s])

    def scatter_row(tile, s, r):
        tk = tok_ref[tile * TE + r]
        return pltpu.make_async_copy(obuf.at[s, pl.ds(r * SUB, SUB)], f_hbm.at[pl.ds(tk * SUB, SUB)], ssem.at[s])

    def wait_gather(s):
        pltpu.make_async_copy(h_hbm.at[pl.ds(0, TE * SUB)], hbuf.at[s], gsem.at[s]).wait()

    def wait_scatter(s):
        pltpu.make_async_copy(obuf.at[s], f_hbm.at[pl.ds(0, TE * SUB)], ssem.at[s]).wait()

    @pl.when(t == 0)
    def _():
        obuf[...] = jnp.zeros(obuf.shape, F32)
        for half in range(2):
            fill = pltpu.make_async_copy(obuf.at[0], f_hbm.at[pl.ds((n_tok + half * TE) * SUB, TE * SUB)], ssem.at[0])
            fill.start()
            fill.wait()

        def body(r, carry):
            gather_row(0, 0, r).start()
            return carry
        lax.fori_loop(0, TE, body, 0, unroll=8)

    @pl.when(live)
    def _():
        wait_gather(slot)

        @pl.when(t >= 1)
        def _():
            wait_scatter(slot)

    def compute(cur):
        nxt = t + 1
        prv = jnp.where(t == 0, n_tiles - 1, t - 1)
        issue = ([functools.partial(gather_row, nxt, 1 - cur, r) for r in range(TE)]
                 + [functools.partial(scatter_row, prv, 1 - cur, r) for r in range(TE)])
        n_stage = 7
        per_stage = -(-len(issue) // n_stage)

        def issue_stage(k):
            for mk in issue[k * per_stage:(k + 1) * per_stage]:
                mk().start()

        h = _load_token_rows(hbuf.at[cur], TE).astype(BF16)
        col = jnp.broadcast_to(wts_ref[0], (2, TE))
        w_lo = jnp.broadcast_to(col[0:1, :], (LANES, TE)).T[:, :1]
        w_hi = jnp.broadcast_to(col[1:2, :], (LANES, TE)).T[:, :1]
        issue_stage(0)
        outs = []
        stage = 1
        for wg_ref, wu_ref, wd_ref, wt in ((wg1_ref, wu1_ref, wd1_ref, w_lo), (wg2_ref, wu2_ref, wd2_ref, w_hi)):
            hg = _dot(h, wg_ref[0, 0])
            issue_stage(stage)
            hu = _dot(h, wu_ref[0, 0])
            issue_stage(stage + 1)
            act = (hg * jax.nn.sigmoid(hg) * hu).astype(BF16)
            outs.append(wt * _dot(act, wd_ref[0, 0]))
            issue_stage(stage + 2)
            stage += 3
        _store_token_rows(obuf.at[cur], outs[0] + outs[1])

    pl.when(live)(functools.partial(compute, slot))

    @pl.when(jnp.logical_and(jnp.logical_not(live), prev_live))
    def _():
        wait_gather(slot)
        wait_scatter(slot)

        def body(r, carry):
            scatter_row(t - 1, other, r).start()
            return carry
        lax.fori_loop(0, TE, body, 0, unroll=8)
        wait_scatter(other)


def _moe_call(tok, e1, e2, nv, wts, h3, wg, wu, wd, *, layer, n_tok):
    d = wg.shape[2]
    de = wg.shape[3]
    n_tiles = e1.shape[0]

    def wspec(shape, which):
        return pl.BlockSpec((1, 1) + shape, lambda t, tok, e1, e2, nv: (layer, (e1, e2)[which][t], 0, 0))

    grid_spec = pltpu.PrefetchScalarGridSpec(
        num_scalar_prefetch=4,
        grid=(n_tiles,),
        in_specs=[pl.BlockSpec((1, 2, TE), lambda t, tok, e1, e2, nv: (t, 0, 0)),
                  pl.BlockSpec(memory_space=pl.ANY),
                  wspec((d, de), 0), wspec((d, de), 0), wspec((de, d), 0),
                  wspec((d, de), 1), wspec((d, de), 1), wspec((de, d), 1)],
        out_specs=pl.BlockSpec(memory_space=pl.ANY),
        scratch_shapes=[pltpu.VMEM((2, TE * SUB, LANES), F32), pltpu.VMEM((2, TE * SUB, LANES), F32),
                        pltpu.SemaphoreType.DMA((2,)), pltpu.SemaphoreType.DMA((2,))],
    )
    return pl.pallas_call(
        functools.partial(_moe_kernel, n_tok=n_tok, n_tiles=n_tiles),
        grid_spec=grid_spec,
        out_shape=jax.ShapeDtypeStruct(((n_tok + 2 * TE) * SUB, LANES), F32),
        compiler_params=_params(("arbitrary",)),
        name="moe",
    )(tok, e1, e2, nv, wts, h3, wg, wu, wd, wg, wu, wd)


def _route_tables(meta, n_tok):
    w_lo = meta[:, 0, :].reshape(-1)
    w_hi = meta[:, 1, :].reshape(-1)
    bucket = meta[:, 2, :].reshape(-1).astype(jnp.int32)
    n_slots = n_tok + N_BUCKETS * TE
    n_tiles = n_slots // TE
    buckets = jnp.arange(N_BUCKETS, dtype=jnp.int32)
    counts = jnp.sum(bucket[:, None] == buckets[None, :], axis=0, dtype=jnp.int32)
    padded = ((counts + TE - 1) // TE) * TE
    pad_end = jnp.cumsum(padded)
    pad_start = pad_end - padded
    fill_i = jnp.arange(TE, dtype=jnp.int32)[None, :]
    fill_key = jnp.where(fill_i < (padded - counts)[:, None], 2 * buckets[:, None] + 1, 2 * N_BUCKETS).reshape(-1)
    keys = jnp.concatenate([2 * bucket, fill_key])
    ids = jnp.concatenate([jnp.arange(n_tok, dtype=jnp.int32), jnp.full((N_BUCKETS * TE,), -1, jnp.int32)])
    zeros = jnp.zeros((N_BUCKETS * TE,), F32)
    _, ids, w_lo, w_hi = lax.sort((keys, ids, jnp.concatenate([w_lo, zeros]), jnp.concatenate([w_hi, zeros])),
                                  num_keys=1, is_stable=True)
    slot = jnp.arange(n_slots, dtype=jnp.int32)
    tok = jnp.where(ids < 0, n_tok + ((slot // TE) % 2) * TE + slot % TE, ids)
    wts = jnp.stack([w_lo.reshape(n_tiles, TE), w_hi.reshape(n_tiles, TE)], axis=1)

    tile_start = jnp.arange(n_tiles, dtype=jnp.int32) * TE
    tb = jnp.sum(tile_start[:, None] >= pad_end[None, :], axis=1, dtype=jnp.int32)
    used = tb < N_BUCKETS
    onehot = (tb[:, None] == buckets[None, :]).astype(jnp.int32)
    nv = jnp.clip(jnp.sum(onehot * (counts + pad_start)[None, :], axis=1) - tile_start, 0, TE) * used
    tbe = jnp.where(used, tb, jnp.max(jnp.where(used, tb, 0)))
    pid = tbe % len(PAIRS)
    pair_lo = (pid >= 3).astype(jnp.int32) + (pid >= 5).astype(jnp.int32)
    pair_hi = pid + 1 - 2 * (pid >= 3).astype(jnp.int32) - (pid >= 5).astype(jnp.int32)
    e1 = (tbe // len(PAIRS)) * EPG + pair_lo
    e2 = (tbe // len(PAIRS)) * EPG + pair_hi
    return tok.astype(jnp.int32), e1.astype(jnp.int32), e2.astype(jnp.int32), nv.astype(jnp.int32), wts


def _final_kernel(x_ref, f_ref, mod_ref, nw_ref, o_ref):
    x = x_ref[...] + mod_ref[0, 5:6, :] * _load_token_rows(f_ref, x_ref.shape[0])
    o_ref[...] = _rms(x, nw_ref[...])


def _final_call(x, f, mod, nw, *, n_lat, seq):
    d = x.shape[1]
    tps = seq // TM
    return pl.pallas_call(
        _final_kernel,
        grid=(n_lat // TM,),
        in_specs=[pl.BlockSpec((TM, d), lambda i: (i, 0)), pl.BlockSpec((TM * SUB, LANES), lambda i: (i, 0)),
                  pl.BlockSpec((1, 8, d), lambda i: (i // tps, 0, 0)), pl.BlockSpec((1, d), lambda i: (0, 0))],
        out_specs=pl.BlockSpec((TM, d), lambda i: (i, 0)),
        out_shape=jax.ShapeDtypeStruct((n_lat, d), F32),
        compiler_params=_params(("arbitrary",)),
        name="final",
    )(x, f, mod, nw)


def _rope_table(pos_r, pos_c, dim, lane_of):
    d = dim // 2
    half = d // 2
    lane = jnp.arange(LANES)
    rl = lane_of(lane)
    is_rope = rl >= 0
    rl = jnp.maximum(rl, 0)
    use_col = rl >= d
    j = rl % half
    first = (rl % d) < half
    inv = ROPE_BASE ** (-(2.0 * j.astype(F32)) / d)
    pos = jnp.where(use_col[None, :], pos_c.astype(F32)[:, None], pos_r.astype(F32)[:, None])
    ang = pos * inv[None, :]
    cos = jnp.where(is_rope[None, :], jnp.cos(ang), 1.0)
    sin = jnp.where(is_rope[None, :], jnp.sin(ang), 0.0)
    return jnp.stack([cos, jnp.where(first[None, :], -sin, 0.0), jnp.where(first[None, :], 0.0, sin)]).astype(F32)


def _tables(seq):
    t = jnp.arange(seq)
    rows, cols = t // GRID_W, t % GRID_W
    tab_a = _rope_table(rows, cols, A_HEAD_DIM, lambda lane: lane % A_HEAD_DIM)
    tab_b = _rope_table(rows, cols, B_ROPE,
                        lambda lane: jnp.where((lane >= B_NOPE) & (lane < B_NOPE + B_ROPE), lane - B_NOPE, -1))
    ident = jnp.stack([jnp.ones((TM, LANES), F32), jnp.zeros((TM, LANES), F32), jnp.zeros((TM, LANES), F32)])
    return jnp.concatenate([tab_a, ident], axis=1), jnp.concatenate([tab_b, ident], axis=1)


def _layer_weights(w_in, w_uq, w_ukv, q_norm, kv_norm, w_o_a, w_o_b, w_out):
    d = w_in.shape[0]
    o = 0
    qa_w = w_in[:, o:o + 512]; o += 512
    ka_w = w_in[:, o:o + 128]; o += 128
    va_w = w_in[:, o:o + 128]; o += 128
    cq_w = w_in[:, o:o + B_Q_RANK]; o += B_Q_RANK
    ckv_w = w_in[:, o:o + B_KV_RANK]; o += B_KV_RANK
    kr_w = w_in[:, o:o + B_ROPE]; o += B_ROPE
    g_w = w_in[:, o:]

    def dup(wk):
        wk = wk.reshape(d, A_KV_HEADS, 1, A_HEAD_DIM)
        return jnp.broadcast_to(wk, (d, A_KV_HEADS, 2, A_HEAD_DIM)).reshape(d, A_KV_HEADS * 2 * A_HEAD_DIM)

    w1 = jnp.concatenate([qa_w, dup(ka_w), dup(va_w), cq_w, ckv_w, jnp.pad(kr_w, ((0, 0), (0, LANES - B_ROPE)))], axis=1)
    wuq = jnp.pad(w_uq.reshape(B_Q_RANK, B_HEADS, B_NOPE + B_ROPE),
                  ((0, 0), (0, 0), (0, LANES - B_NOPE - B_ROPE))).reshape(B_Q_RANK, B_HEADS * LANES)
    ukv = w_ukv.reshape(B_KV_RANK, B_HEADS, B_NOPE + B_V)
    wukn = jnp.pad(ukv[:, :, :B_NOPE], ((0, 0), (0, 0), (0, LANES - B_NOPE))).reshape(B_KV_RANK, B_HEADS * LANES)
    wuv = ukv[:, :, B_NOPE:].reshape(B_KV_RANK, B_HEADS * B_V)
    src = jnp.arange(LANES)[:, None]
    dst = jnp.arange(B_HEADS * LANES)[None, :]
    rp = ((src < B_ROPE) & (dst % LANES == src + B_NOPE)).astype(BF16)
    return dict(w1=w1.astype(BF16), wg=g_w.astype(BF16), qn=q_norm.reshape(1, -1), wuq=wuq.astype(BF16),
                kvn=kv_norm.reshape(1, -1), wukn=wukn.astype(BF16), wuv=wuv.astype(BF16), rp=rp,
                woa=w_o_a.astype(BF16), wob=w_o_b.astype(BF16), wout=w_out.astype(BF16))


def kernel(x, c, ctx, c_ctx, w_mod, b_mod, norm_mix, norm_ffn, w_in, attn_sink, mla_q_norm, w_uq, mla_kv_norm, w_ukv,
           w_o_a, w_o_b, w_out, w_router, router_bias, w_expert_gate, w_expert_up, w_expert_down, final_norm):
    batch, seq, d = x.shape
    ctx_len = ctx.shape[1]
    depth = w_mod.shape[0]
    n_lat = batch * seq
    n_all = n_lat + batch * ctx_len
    assert d == SUB * LANES and seq % TM == 0 and ctx_len % TM == 0 and seq % GRID_W == 0 and batch + 1 <= MOD_ROWS

    c_all = jnp.concatenate([c, c_ctx[None, :], jnp.zeros((MOD_ROWS - batch - 1, d), F32)], axis=0)
    mod = _mod_call(c_all, w_mod, b_mod)
    mod = jnp.pad(mod.transpose(0, 2, 1, 3), ((0, 0), (0, 0), (0, 2), (0, 0)))

    tab_a, tab_b = _tables(seq)
    wr = jnp.pad(w_router.astype(F32), ((0, 0), (0, LANES - N_EXPERTS)))
    wr_hi = wr.astype(BF16)
    wr2 = jnp.concatenate([wr_hi, (wr - wr_hi.astype(F32)).astype(BF16)], axis=1)
    rb = router_bias.astype(F32).reshape(N_EXPERTS, 1)
    wg_all = w_expert_gate.astype(BF16)
    wu_all = w_expert_up.astype(BF16)
    wd_all = w_expert_down.astype(BF16)

    xs = jnp.concatenate([x.reshape(n_lat, d), ctx.reshape(batch * ctx_len, d)], axis=0)
    f = None
    for l in range(depth):
        need_ctx = l < depth - 1
        w = _layer_weights(w_in[l], w_uq[l], w_ukv[l], mla_q_norm[l], mla_kv_norm[l], w_o_a[l], w_o_b[l], w_out[l])
        outs = _proj_call(xs, f, mod[l - 1] if l else None, mod[l], norm_mix[l].reshape(1, d), tab_a, tab_b, w,
                          n_lat=n_lat, seq=seq)
        if l:
            xs, outs = outs[0], outs[1:]
        qa, ka, va, qb, kb, vb, ga, gb = outs
        ya = _window_call(attn_sink[l].astype(F32), qa, ka, va, batch=batch, seq=seq, ctx_len=ctx_len, need_ctx=need_ctx)
        yb = _mla_call(qb, kb, vb, batch=batch, seq=seq, ctx_len=ctx_len, need_ctx=need_ctx)
        n_rows = n_all if need_ctx else n_lat
        xs, h3, meta = _merge_call(xs, ya, yb, ga, gb, mod[l], norm_ffn[l].reshape(1, d), w, wr2, rb,
                                   n_rows=n_rows, n_lat=n_lat, seq=seq)
        tok, e1, e2, nv, wts = _route_tables(meta, n_rows)
        f = _moe_call(tok, e1, e2, nv, wts, h3, wg_all, wu_all, wd_all, layer=l, n_tok=n_rows)
    out = _final_call(xs, f, mod[depth - 1], final_norm.reshape(1, d), n_lat=n_lat, seq=seq)
    return out.reshape(batch, seq, d)
```

```python
import functools

import jax
import jax.numpy as jnp
from jax import lax
from jax.experimental import pallas as pl
from jax.experimental.pallas import tpu as pltpu

F32 = jnp.float32
BF16 = jnp.bfloat16

EPS = 1e-6
ROPE_BASE = 10000.0
GRID_W = 64
BLOCK = 128
A_HEADS, A_KV_HEADS, A_HEAD_DIM = 8, 2, 64
A_GROUP = A_HEADS // A_KV_HEADS
WINDOW = 128
B_HEADS, B_NOPE, B_ROPE, B_V = 8, 64, 32, 64
B_Q_RANK, B_KV_RANK = 256, 128
N_EXPERTS, N_GROUPS = 16, 4
EPG = N_EXPERTS // N_GROUPS
D_EXPERT = 512
PAIRS = ((0, 1), (0, 2), (0, 3), (1, 2), (1, 3), (2, 3))
N_BUCKETS = N_GROUPS * len(PAIRS)

LANES = 128
SUB = 8
TM = 256
TM_MERGE = 512
TQ = 256
TE = 256
MOE_BUFS = 3
MOD_ROWS = 24
NEG = -1e30
LOG2E = 1.4426950408889634
VMEM_LIMIT = 52 * 1024 * 1024

W1_QA, W1_KA, W1_VA, W1_CQ, W1_CKV, W1_KR = 0, 512, 768, 1024, 1280, 1408
W1_COLS = 1536


def _params(sem, vmem=VMEM_LIMIT):
    return pltpu.CompilerParams(dimension_semantics=sem, vmem_limit_bytes=vmem)


def _dot(a, b):
    return jnp.dot(a, b, preferred_element_type=F32)


def _dot_nt(a, b):
    return lax.dot_general(a, b, (((1,), (1,)), ((), ())), preferred_element_type=F32)


def _rms(x, g):
    return x * lax.rsqrt(jnp.mean(x * x, axis=-1, keepdims=True) + EPS) * g


def _lane_tile(t, width):
    return jnp.concatenate([t] * (width // LANES), axis=1)


def _rope(t, tab_ref, shift):
    w = t.shape[-1]
    up = pltpu.roll(t, w - shift, 1)
    dn = pltpu.roll(t, shift, 1)
    return (t * _lane_tile(tab_ref[0], w) + up * _lane_tile(tab_ref[1], w)
            + dn * _lane_tile(tab_ref[2], w))


def _load_token_rows(ref, rows):
    return jnp.concatenate([ref[pl.ds(s, rows, stride=SUB), :] for s in range(SUB)], axis=1)


def _store_token_rows(ref, val):
    rows = val.shape[0]
    for s in range(SUB):
        ref[pl.ds(s, rows, stride=SUB), :] = val[:, s * LANES:(s + 1) * LANES]


def _mod_kernel(c_ref, w_ref, b_ref, o_ref):
    c = c_ref[...]
    a = (c * jax.nn.sigmoid(c)).astype(BF16)
    o_ref[0, 0] = _dot(a, w_ref[0].astype(BF16)) + b_ref[0, 0]


def _mod_call(c_all, w_mod, b_mod):
    depth, d, _ = w_mod.shape
    return pl.pallas_call(
        _mod_kernel,
        grid=(depth, 6),
        in_specs=[pl.BlockSpec((MOD_ROWS, d), lambda l, k: (0, 0)),
                  pl.BlockSpec((1, d, d), lambda l, k: (l, 0, k)),
                  pl.BlockSpec((1, 1, 1, d), lambda l, k: (l, k, 0, 0))],
        out_specs=pl.BlockSpec((1, 1, MOD_ROWS, d), lambda l, k: (l, k, 0, 0)),
        out_shape=jax.ShapeDtypeStruct((depth, 6, MOD_ROWS, d), F32),
        compiler_params=_params(("arbitrary", "arbitrary")),
        name="mod",
    )(c_all, w_mod, b_mod.reshape(depth, 6, 1, d))


def _proj_kernel(*refs, has_f):
    if has_f:
        x_ref, f_ref, modp_ref, refs = refs[0], refs[1], refs[2], refs[3:]
    else:
        x_ref, refs = refs[0], refs[1:]
    (mod_ref, nw_ref, ta_ref, tb_ref, w1_ref, wg_ref, qn_ref, wuq_ref, kvn_ref, wukn_ref, wuv_ref,
     rp_ref), refs = refs[:12], refs[12:]
    if has_f:
        xo_ref, refs = refs[0], refs[1:]
    qa_ref, ka_ref, va_ref, qb_ref, kb_ref, vb_ref, ga_ref, gb_ref = refs

    x = x_ref[...]
    if has_f:
        x = x + modp_ref[0, 5:6, :] * _load_token_rows(f_ref, x.shape[0])
        xo_ref[...] = x
    h = (_rms(x, nw_ref[...]) * (1.0 + mod_ref[0, 1:2, :]) + mod_ref[0, 0:1, :]).astype(BF16)

    t = _dot(h, w1_ref[...])
    qa_ref[...] = (_rope(t[:, W1_QA:W1_KA], ta_ref, 16) * (A_HEAD_DIM ** -0.5 * LOG2E)).astype(BF16)
    ka_ref[...] = _rope(t[:, W1_KA:W1_VA], ta_ref, 16).astype(BF16)
    va_ref[...] = t[:, W1_VA:W1_CQ].astype(BF16)

    cq = _rms(t[:, W1_CQ:W1_CKV], qn_ref[...]).astype(BF16)
    qb = _rope(_dot(cq, wuq_ref[...]), tb_ref, 8)
    qb_ref[...] = (qb * ((B_NOPE + B_ROPE) ** -0.5 * LOG2E)).astype(BF16)

    ckv = _rms(t[:, W1_CKV:W1_KR], kvn_ref[...]).astype(BF16)
    vb_ref[...] = _dot(ckv, wuv_ref[...]).astype(BF16)
    kr = t[:, W1_KR:W1_COLS]
    kr_hi = kr.astype(BF16)
    kr_lo = (kr - kr_hi.astype(F32)).astype(BF16)
    kb = _dot(ckv, wukn_ref[...]) + _dot(kr_hi, rp_ref[...]) + _dot(kr_lo, rp_ref[...])
    kb_ref[...] = _rope(kb, tb_ref, 8).astype(BF16)

    g = jax.nn.sigmoid(_dot(h, wg_ref[...]))
    d = ga_ref.shape[-1]
    ga_ref[...] = g[:, :d].astype(BF16)
    gb_ref[...] = g[:, d:].astype(BF16)


def _proj_call(x, f, modp, mod, nw, tab_a, tab_b, w, *, n_lat, seq):
    n, d = x.shape
    nt = n // TM
    n_lat_t = n_lat // TM
    tps = seq // TM
    ctx_row = n_lat // seq
    has_f = f is not None

    def tok(i):
        return (i, 0)

    def modi(i):
        return (jnp.where(i < n_lat_t, i // tps, ctx_row), 0, 0)

    def tabi(i):
        return (0, jnp.where(i < n_lat_t, i % tps, tps), 0)

    def full(a):
        return pl.BlockSpec(a.shape, lambda i: (0,) * a.ndim)

    mod_spec = pl.BlockSpec((1, 8, d), modi)
    in_specs = [pl.BlockSpec((TM, d), tok)]
    args = [x]
    if has_f:
        in_specs += [pl.BlockSpec((TM * SUB, LANES), tok), mod_spec]
        args += [f, modp]
    in_specs += [mod_spec, full(nw), pl.BlockSpec((3, TM, LANES), tabi), pl.BlockSpec((3, TM, LANES), tabi)]
    args += [mod, nw, tab_a, tab_b]
    for k in ("w1", "wg", "qn", "wuq", "kvn", "wukn", "wuv", "rp"):
        in_specs.append(full(w[k]))
        args.append(w[k])

    widths = (512, 256, 256, 1024, 1024, 512, d, d)
    out_shape = [jax.ShapeDtypeStruct((n, wd), BF16) for wd in widths]
    out_specs = [pl.BlockSpec((TM, wd), tok) for wd in widths]
    if has_f:
        out_shape.insert(0, jax.ShapeDtypeStruct((n, d), F32))
        out_specs.insert(0, pl.BlockSpec((TM, d), tok))
    return pl.pallas_call(
        functools.partial(_proj_kernel, has_f=has_f),
        grid=(nt,), in_specs=in_specs, out_specs=out_specs, out_shape=out_shape,
        compiler_params=_params(("arbitrary",)),
        name="proj",
    )(*args)


def _window_kernel(sink_ref, q_ref, kp_ref, kc_ref, kn_ref, kx_ref, vp_ref, vc_ref, vn_ref, vx_ref, o_ref, *, nb):
    n = pl.program_id(1)
    rows = A_GROUP * BLOCK
    lo = lax.broadcasted_iota(jnp.int32, (BLOCK, LANES), 1) < A_HEAD_DIM
    r = lax.broadcasted_iota(jnp.int32, (rows, 3 * BLOCK), 0) % BLOCK
    c = lax.broadcasted_iota(jnp.int32, (rows, 3 * BLOCK), 1)
    dist = c - r
    valid = ((dist >= BLOCK - WINDOW) & (dist <= BLOCK + WINDOW) & (n < nb)
             & ((c >= BLOCK) | (n > 0)) & ((c < 2 * BLOCK) | (n < nb - 1)))
    row_head = lax.broadcasted_iota(jnp.int32, (rows, 1), 0) // BLOCK
    q = q_ref[...]
    zero = jnp.zeros((BLOCK, LANES), BF16)
    for k in range(A_KV_HEADS):
        ksl = slice(k * LANES, (k + 1) * LANES)
        parts = []
        for j in range(A_GROUP // 2):
            blk = q[:, k * 2 * LANES + j * LANES:k * 2 * LANES + (j + 1) * LANES]
            parts += [jnp.where(lo, blk, zero), jnp.where(lo, zero, blk)]
        qs = jnp.concatenate(parts, axis=0)
        k_lat = jnp.concatenate([kp_ref[:, ksl], kc_ref[:, ksl], kn_ref[:, ksl]], axis=0)
        v_lat = jnp.concatenate([vp_ref[:, ksl], vc_ref[:, ksl], vn_ref[:, ksl]], axis=0)
        s_l = jnp.where(valid, _dot_nt(qs, k_lat), NEG)
        s_x = _dot_nt(qs, kx_ref[:, ksl])
        sink = jnp.zeros((rows, 1), F32)
        for g in range(A_GROUP):
            sink = jnp.where(row_head == g, sink_ref[k * A_GROUP + g] * LOG2E, sink)
        m = jnp.maximum(jnp.maximum(jnp.max(s_l, axis=-1, keepdims=True), jnp.max(s_x, axis=-1, keepdims=True)), sink)
        e_l = jnp.exp2(s_l - m)
        e_x = jnp.exp2(s_x - m)
        den = jnp.sum(e_l, axis=-1, keepdims=True) + jnp.sum(e_x, axis=-1, keepdims=True) + jnp.exp2(sink - m)
        o = (_dot(e_l.astype(BF16), v_lat) + _dot(e_x.astype(BF16), vx_ref[:, ksl])) / den
        for j in range(A_GROUP // 2):
            even = o[(2 * j) * BLOCK:(2 * j + 1) * BLOCK]
            odd = o[(2 * j + 1) * BLOCK:(2 * j + 2) * BLOCK]
            o_ref[:, k * 2 * LANES + j * LANES:k * 2 * LANES + (j + 1) * LANES] = jnp.where(lo, even, odd).astype(BF16)


def _window_call(sink, qa, ka, va, *, batch, seq, ctx_len, need_ctx):
    n = qa.shape[0] if need_ctx else batch * seq
    nb = seq // BLOCK
    ncb = ctx_len // BLOCK
    lat_blocks = batch * nb
    nq = nb + (ncb if need_ctx else 0)

    def qi(b, i):
        return (jnp.where(i < nb, b * nb + i, lat_blocks + b * ncb + (i - nb)), 0)

    def ki(off):
        return lambda b, i: (b * nb + jnp.clip(i + off, 0, nb - 1), 0)

    def xi(b, i):
        return (batch * seq // ctx_len + b, 0)

    kv_specs = [pl.BlockSpec((BLOCK, 2 * LANES), ki(-1)), pl.BlockSpec((BLOCK, 2 * LANES), ki(0)),
                pl.BlockSpec((BLOCK, 2 * LANES), ki(1)), pl.BlockSpec((ctx_len, 2 * LANES), xi)]
    return pl.pallas_call(
        functools.partial(_window_kernel, nb=nb),
        grid=(batch, nq),
        in_specs=[pl.BlockSpec(memory_space=pltpu.SMEM), pl.BlockSpec((BLOCK, 4 * LANES), qi)] + kv_specs + kv_specs,
        out_specs=pl.BlockSpec((BLOCK, 4 * LANES), qi),
        out_shape=jax.ShapeDtypeStruct((n, 4 * LANES), BF16),
        compiler_params=_params(("arbitrary", "arbitrary")),
        name="window",
    )(sink, qa, ka, ka, ka, ka, va, va, va, va)


def _mla_kernel(q_ref, kl_ref, kx_ref, vl_ref, vx_ref, o_ref, vaug_ref, *, nql, seq):
    j = pl.program_id(2)
    lo = lax.broadcasted_iota(jnp.int32, (TQ, LANES), 1) < B_V

    @pl.when(j == 0)
    def _():
        one = jnp.ones((1, LANES), BF16)
        for v_ref, start in ((vl_ref, 0), (vx_ref, seq)):
            v = v_ref[...]
            keep = lax.broadcasted_iota(jnp.int32, v.shape, 1) < B_V
            vaug_ref[0, start:start + v.shape[0], :] = jnp.where(keep, v, one)
            vaug_ref[1, start:start + v.shape[0], :] = jnp.where(keep, one, v)

    def head(hh, with_lat):
        sl = slice(hh * LANES, (hh + 1) * LANES)
        q = q_ref[:, sl]
        s_x = _dot_nt(q, kx_ref[:, sl])
        m = jnp.max(s_x, axis=-1, keepdims=True)
        if with_lat:
            s_l = _dot_nt(q, kl_ref[:, sl])
            m = jnp.maximum(m, jnp.max(s_l, axis=-1, keepdims=True))
        o = _dot(jnp.exp2(s_x - m).astype(BF16), vaug_ref[hh, seq:, :])
        if with_lat:
            o = o + _dot(jnp.exp2(s_l - m).astype(BF16), vaug_ref[hh, :seq, :])
        return o / pltpu.roll(o, B_V, 1)

    def run(with_lat):
        o_ref[...] = jnp.where(lo, head(0, with_lat), head(1, with_lat)).astype(BF16)

    @pl.when(j < nql)
    def _():
        run(True)

    @pl.when(j >= nql)
    def _():
        run(False)


def _mla_call(qb, kb, vb, *, batch, seq, ctx_len, need_ctx):
    n = qb.shape[0] if need_ctx else batch * seq
    assert ctx_len == TQ
    nql = seq // TQ
    nq = nql + (1 if need_ctx else 0)
    ctx0 = batch * seq // ctx_len
    pairs = B_HEADS // 2

    def qi(b, p, j):
        return (jnp.where(j < nql, b * nql + j, ctx0 + b), p)

    return pl.pallas_call(
        functools.partial(_mla_kernel, nql=nql, seq=seq),
        grid=(batch, pairs, nq),
        in_specs=[pl.BlockSpec((TQ, 2 * LANES), qi),
                  pl.BlockSpec((seq, 2 * LANES), lambda b, p, j: (b, p)),
                  pl.BlockSpec((ctx_len, 2 * LANES), lambda b, p, j: (ctx0 + b, p)),
                  pl.BlockSpec((seq, LANES), lambda b, p, j: (b, p)),
                  pl.BlockSpec((ctx_len, LANES), lambda b, p, j: (ctx0 + b, p))],
        out_specs=pl.BlockSpec((TQ, LANES), qi),
        out_shape=jax.ShapeDtypeStruct((n, pairs * LANES), BF16),
        scratch_shapes=[pltpu.VMEM((2, seq + ctx_len, LANES), BF16)],
        compiler_params=_params(("arbitrary", "arbitrary", "arbitrary")),
        name="mla",
    )(qb, kb, kb, vb, vb)


def _top2(v):
    i1 = jnp.zeros_like(v[0])
    m1 = v[0]
    for i in range(1, EPG):
        u = v[i] > m1
        i1 = jnp.where(u, float(i), i1)
        m1 = jnp.where(u, v[i], m1)
    i2 = jnp.zeros_like(v[0])
    m2 = jnp.full_like(v[0], -jnp.inf)
    for i in range(EPG):
        cand = jnp.where(i1 == float(i), -jnp.inf, v[i])
        u = cand > m2
        i2 = jnp.where(u, float(i), i2)
        m2 = jnp.where(u, cand, m2)
    return i1, i2, m1, m2


def _pick(idx, vals):
    out = vals[0]
    for i in range(1, len(vals)):
        out = jnp.where(idx == float(i), vals[i], out)
    return out


def _merge_kernel(x_ref, ya_ref, yb_ref, ga_ref, gb_ref, mod_ref, nw_ref, woa_ref, wob_ref, wout_ref, wr2_ref, rb_ref,
                  xo_ref, h_ref, meta_ref):
    a = _dot(ya_ref[...], woa_ref[...])
    b = _dot(yb_ref[...], wob_ref[...])
    mix = (ga_ref[...].astype(F32) * a + gb_ref[...].astype(F32) * b).astype(BF16)
    x = x_ref[...] + mod_ref[0, 2:3, :] * _dot(mix, wout_ref[...])
    xo_ref[...] = x
    h = _rms(x, nw_ref[...]) * (1.0 + mod_ref[0, 4:5, :]) + mod_ref[0, 3:4, :]
    _store_token_rows(h_ref, h)

    h_hi = h.astype(BF16)
    h_lo = (h - h_hi.astype(F32)).astype(BF16)
    l_hi = _dot(h_hi, wr2_ref[...])
    logits = l_hi[:, :LANES] + l_hi[:, LANES:] + _dot(h_lo, wr2_ref[:, :LANES])
    sc = jax.nn.sigmoid(logits.T[:N_EXPERTS, :])
    sel = sc + rb_ref[...]
    sel_rows = [sel[e:e + 1, :] for e in range(N_EXPERTS)]
    sc_rows = [sc[e:e + 1, :] for e in range(N_EXPERTS)]
    best = jnp.zeros_like(sel_rows[0])
    best_v = None
    for g in range(N_GROUPS):
        _, _, m1, m2 = _top2(sel_rows[g * EPG:(g + 1) * EPG])
        gv = m1 + m2
        if best_v is None:
            best_v = gv
        else:
            u = gv > best_v
            best = jnp.where(u, float(g), best)
            best_v = jnp.where(u, gv, best_v)
    sel_g = [_pick(best, [sel_rows[g * EPG + i] for g in range(N_GROUPS)]) for i in range(EPG)]
    sc_g = [_pick(best, [sc_rows[g * EPG + i] for g in range(N_GROUPS)]) for i in range(EPG)]
    i1, i2, _, _ = _top2(sel_g)
    s1 = _pick(i1, sc_g)
    s2 = _pick(i2, sc_g)
    tot = s1 + s2
    first_low = i1 < i2
    e_lo = jnp.where(first_low, i1, i2)
    e_hi = jnp.where(first_low, i2, i1)
    w_lo = jnp.where(first_low, s1, s2) / tot
    w_hi = jnp.where(first_low, s2, s1) / tot
    pid = jnp.where(e_lo == 0.0, e_hi - 1.0, jnp.where(e_lo == 1.0, e_hi + 1.0, 5.0))
    bucket = best * float(len(PAIRS)) + pid
    t = sel.shape[1]
    meta_ref[0] = jnp.concatenate([w_lo, w_hi, bucket, jnp.zeros((SUB - 3, t), F32)], axis=0)


def _merge_call(x, ya, yb, ga, gb, mod, nw, w, wr2, rb, *, n_rows, n_lat, seq):
    d = x.shape[1]
    tm = TM_MERGE
    nt = n_rows // tm
    n_lat_t = n_lat // tm
    tps = seq // tm
    ctx_row = n_lat // seq

    def tok(i):
        return (i, 0)

    def full(a):
        return pl.BlockSpec(a.shape, lambda i: (0,) * a.ndim)

    return pl.pallas_call(
        _merge_kernel,
        grid=(nt,),
        in_specs=[pl.BlockSpec((tm, d), tok), pl.BlockSpec((tm, 512), tok), pl.BlockSpec((tm, 512), tok),
                  pl.BlockSpec((tm, d), tok), pl.BlockSpec((tm, d), tok),
                  pl.BlockSpec((1, 8, d), lambda i: (jnp.where(i < n_lat_t, i // tps, ctx_row), 0, 0)),
                  full(nw), full(w["woa"]), full(w["wob"]), full(w["wout"]), full(wr2), full(rb)],
        out_specs=[pl.BlockSpec((tm, d), tok), pl.BlockSpec((tm * SUB, LANES), tok),
                   pl.BlockSpec((1, SUB, tm), lambda i: (i, 0, 0))],
        out_shape=[jax.ShapeDtypeStruct((n_rows, d), F32), jax.ShapeDtypeStruct((n_rows * SUB, LANES), F32),
                   jax.ShapeDtypeStruct((nt, SUB, tm), F32)],
        compiler_params=_params(("arbitrary",)),
        name="merge",
    )(x, ya, yb, ga, gb, mod, nw, w["woa"], w["wob"], w["wout"], wr2, rb)


def _moe_kernel(tok_ref, e1_ref, e2_ref, nv_ref, wts_ref, h_hbm, wg1_ref, wu1_ref, wd1_ref, wg2_ref, wu2_ref, wd2_ref,
                f_hbm, hbuf, obuf, gsem, ssem, *, n_tok, n_tiles):
    t = pl.program_id(0)
    slot = t % MOE_BUFS
    slot1 = (t + 1) % MOE_BUFS
    slot2 = (t + 2) % MOE_BUFS
    live = nv_ref[t] > 0
    prev_live = jnp.logical_and(t >= 1, nv_ref[jnp.maximum(t - 1, 0)] > 0)

    def gather_row(tile, s, r):
        tk = jnp.minimum(tok_ref[tile * TE + r], n_tok - 1)
        return pltpu.make_async_copy(h_hbm.at[pl.ds(tk * SUB, SUB)], hbuf.at[s, pl.ds(r * SUB, SUB)], gsem.at[s])

    def scatter_row(tile, s, r):
        tk = tok_ref[tile * TE + r]
        return pltpu.make_async_copy(obuf.at[s, pl.ds(r * SUB, SUB)], f_hbm.at[pl.ds(tk * SUB, SUB)], ssem.at[s])

    def wait_gather(s):
        pltpu.make_async_copy(h_hbm.at[pl.ds(0, TE * SUB)], hbuf.at[s], gsem.at[s]).wait()

    def wait_scatter(s):
        pltpu.make_async_copy(obuf.at[s], f_hbm.at[pl.ds(0, TE * SUB)], ssem.at[s]).wait()

    def start_rows(make_row, tile, s):
        def body(r, carry):
            make_row(tile, s, r).start()
            return carry
        lax.fori_loop(0, TE, body, 0, unroll=8)

    @pl.when(t == 0)
    def _():
        obuf[...] = jnp.zeros(obuf.shape, F32)
        for half in range(2):
            fill = pltpu.make_async_copy(obuf.at[0], f_hbm.at[pl.ds((n_tok + half * TE) * SUB, TE * SUB)], ssem.at[0])
            fill.start()
            fill.wait()
        start_rows(gather_row, 0, 0)
        start_rows(gather_row, 1, 1)

    @pl.when(live)
    def _():
        start_rows(gather_row, jnp.minimum(t + 2, n_tiles - 1), slot2)
        start_rows(scatter_row, jnp.where(t == 0, n_tiles - 1, t - 1), slot2)
        wait_gather(slot)

        @pl.when(t >= 2)
        def _():
            wait_scatter(slot)

    @pl.when(live)
    def _():
        h = _load_token_rows(hbuf.at[slot], TE).astype(BF16)
        w_lo = jnp.broadcast_to(wts_ref[0, 0:1, :], (LANES, TE)).T[:, :1]
        w_hi = jnp.broadcast_to(wts_ref[0, 1:2, :], (LANES, TE)).T[:, :1]
        outs = []
        for wg_ref, wu_ref, wd_ref, wt in ((wg1_ref, wu1_ref, wd1_ref, w_lo), (wg2_ref, wu2_ref, wd2_ref, w_hi)):
            hg = _dot(h, wg_ref[0, 0])
            hu = _dot(h, wu_ref[0, 0])
            act = (hg * jax.nn.sigmoid(hg) * hu).astype(BF16)
            outs.append(wt * _dot(act, wd_ref[0, 0]))
        _store_token_rows(obuf.at[slot], outs[0] + outs[1])

    @pl.when(jnp.logical_and(jnp.logical_not(live), prev_live))
    def _():
        wait_gather(slot)
        wait_gather(slot1)
        wait_scatter(slot1)

        @pl.when(t >= 2)
        def _():
            wait_scatter(slot)
        start_rows(scatter_row, t - 1, slot2)
        wait_scatter(slot2)


def _moe_call(tok, e1, e2, nv, wts, h3, wg, wu, wd, *, layer, n_tok):
    d = wg.shape[2]
    de = wg.shape[3]
    n_tiles = e1.shape[0]

    def wspec(shape, which):
        return pl.BlockSpec((1, 1) + shape, lambda t, tok, e1, e2, nv: (layer, (e1, e2)[which][t], 0, 0))

    grid_spec = pltpu.PrefetchScalarGridSpec(
        num_scalar_prefetch=4,
        grid=(n_tiles,),
        in_specs=[pl.BlockSpec((1, 2, TE), lambda t, tok, e1, e2, nv: (t, 0, 0)),
                  pl.BlockSpec(memory_space=pl.ANY),
                  wspec((d, de), 0), wspec((d, de), 0), wspec((de, d), 0),
                  wspec((d, de), 1), wspec((d, de), 1), wspec((de, d), 1)],
        out_specs=pl.BlockSpec(memory_space=pl.ANY),
        scratch_shapes=[pltpu.VMEM((MOE_BUFS, TE * SUB, LANES), F32), pltpu.VMEM((MOE_BUFS, TE * SUB, LANES), F32),
                        pltpu.SemaphoreType.DMA((MOE_BUFS,)), pltpu.SemaphoreType.DMA((MOE_BUFS,))],
    )
    return pl.pallas_call(
        functools.partial(_moe_kernel, n_tok=n_tok, n_tiles=n_tiles),
        grid_spec=grid_spec,
        out_shape=jax.ShapeDtypeStruct(((n_tok + 2 * TE) * SUB, LANES), F32),
        compiler_params=_params(("arbitrary",)),
        name="moe",
    )(tok, e1, e2, nv, wts, h3, wg, wu, wd, wg, wu, wd)


def _route_tables(meta, n_tok):
    w_lo = meta[:, 0, :].reshape(-1)
    w_hi = meta[:, 1, :].reshape(-1)
    bucket = meta[:, 2, :].reshape(-1).astype(jnp.int32)
    n_slots = n_tok + N_BUCKETS * TE
    n_tiles = n_slots // TE
    buckets = jnp.arange(N_BUCKETS, dtype=jnp.int32)
    counts = jnp.sum(bucket[:, None] == buckets[None, :], axis=0, dtype=jnp.int32)
    padded = ((counts + TE - 1) // TE) * TE
    pad_end = jnp.cumsum(padded)
    pad_start = pad_end - padded
    fill_i = jnp.arange(TE, dtype=jnp.int32)[None, :]
    fill_key = jnp.where(fill_i < (padded - counts)[:, None], 2 * buckets[:, None] + 1, 2 * N_BUCKETS).reshape(-1)
    keys = jnp.concatenate([2 * bucket, fill_key])
    ids = jnp.concatenate([jnp.arange(n_tok, dtype=jnp.int32), jnp.full((N_BUCKETS * TE,), -1, jnp.int32)])
    zeros = jnp.zeros((N_BUCKETS * TE,), F32)
    _, ids, w_lo, w_hi = lax.sort((keys, ids, jnp.concatenate([w_lo, zeros]), jnp.concatenate([w_hi, zeros])),
                                  num_keys=1, is_stable=True)
    slot = jnp.arange(n_slots, dtype=jnp.int32)
    tok = jnp.where(ids < 0, n_tok + ((slot // TE) % 2) * TE + slot % TE, ids)
    wts = jnp.stack([w_lo.reshape(n_tiles, TE), w_hi.reshape(n_tiles, TE)], axis=1)

    tile_start = jnp.arange(n_tiles, dtype=jnp.int32) * TE
    tb = jnp.sum(tile_start[:, None] >= pad_end[None, :], axis=1, dtype=jnp.int32)
    used = tb < N_BUCKETS
    onehot = (tb[:, None] == buckets[None, :]).astype(jnp.int32)
    nv = jnp.clip(jnp.sum(onehot * (counts + pad_start)[None, :], axis=1) - tile_start, 0, TE) * used
    tbe = jnp.where(used, tb, jnp.max(jnp.where(used, tb, 0)))
    pid = tbe % len(PAIRS)
    pair_lo = (pid >= 3).astype(jnp.int32) + (pid >= 5).astype(jnp.int32)
    pair_hi = pid + 1 - 2 * (pid >= 3).astype(jnp.int32) - (pid >= 5).astype(jnp.int32)
    e1 = (tbe // len(PAIRS)) * EPG + pair_lo
    e2 = (tbe // len(PAIRS)) * EPG + pair_hi
    return tok.astype(jnp.int32), e1.astype(jnp.int32), e2.astype(jnp.int32), nv.astype(jnp.int32), wts


def _final_kernel(x_ref, f_ref, mod_ref, nw_ref, o_ref):
    x = x_ref[...] + mod_ref[0, 5:6, :] * _load_token_rows(f_ref, x_ref.shape[0])
    o_ref[...] = _rms(x, nw_ref[...])


def _final_call(x, f, mod, nw, *, n_lat, seq):
    d = x.shape[1]
    tm = TM_MERGE
    tps = seq // tm
    return pl.pallas_call(
        _final_kernel,
        grid=(n_lat // tm,),
        in_specs=[pl.BlockSpec((tm, d), lambda i: (i, 0)), pl.BlockSpec((tm * SUB, LANES), lambda i: (i, 0)),
                  pl.BlockSpec((1, 8, d), lambda i: (i // tps, 0, 0)), pl.BlockSpec((1, d), lambda i: (0, 0))],
        out_specs=pl.BlockSpec((tm, d), lambda i: (i, 0)),
        out_shape=jax.ShapeDtypeStruct((n_lat, d), F32),
        compiler_params=_params(("arbitrary",)),
        name="final",
    )(x, f, mod, nw)


def _rope_table(pos_r, pos_c, dim, lane_of):
    d = dim // 2
    half = d // 2
    lane = jnp.arange(LANES)
    rl = lane_of(lane)
    is_rope = rl >= 0
    rl = jnp.maximum(rl, 0)
    use_col = rl >= d
    j = rl % half
    first = (rl % d) < half
    inv = ROPE_BASE ** (-(2.0 * j.astype(F32)) / d)
    pos = jnp.where(use_col[None, :], pos_c.astype(F32)[:, None], pos_r.astype(F32)[:, None])
    ang = pos * inv[None, :]
    cos = jnp.where(is_rope[None, :], jnp.cos(ang), 1.0)
    sin = jnp.where(is_rope[None, :], jnp.sin(ang), 0.0)
    return jnp.stack([cos, jnp.where(first[None, :], -sin, 0.0), jnp.where(first[None, :], 0.0, sin)]).astype(F32)


def _tables(seq):
    t = jnp.arange(seq)
    rows, cols = t // GRID_W, t % GRID_W
    tab_a = _rope_table(rows, cols, A_HEAD_DIM, lambda lane: lane % A_HEAD_DIM)
    tab_b = _rope_table(rows, cols, B_ROPE,
                        lambda lane: jnp.where((lane >= B_NOPE) & (lane < B_NOPE + B_ROPE), lane - B_NOPE, -1))
    ident = jnp.stack([jnp.ones((TM, LANES), F32), jnp.zeros((TM, LANES), F32), jnp.zeros((TM, LANES), F32)])
    return jnp.concatenate([tab_a, ident], axis=1), jnp.concatenate([tab_b, ident], axis=1)


def _layer_weights(w_in, w_uq, w_ukv, q_norm, kv_norm, w_o_a, w_o_b, w_out):
    d = w_in.shape[0]
    o = 0
    qa_w = w_in[:, o:o + 512]; o += 512
    ka_w = w_in[:, o:o + 128]; o += 128
    va_w = w_in[:, o:o + 128]; o += 128
    cq_w = w_in[:, o:o + B_Q_RANK]; o += B_Q_RANK
    ckv_w = w_in[:, o:o + B_KV_RANK]; o += B_KV_RANK
    kr_w = w_in[:, o:o + B_ROPE]; o += B_ROPE
    g_w = w_in[:, o:]

    def dup(wk):
        wk = wk.reshape(d, A_KV_HEADS, 1, A_HEAD_DIM)
        return jnp.broadcast_to(wk, (d, A_KV_HEADS, 2, A_HEAD_DIM)).reshape(d, A_KV_HEADS * 2 * A_HEAD_DIM)

    w1 = jnp.concatenate([qa_w, dup(ka_w), dup(va_w), cq_w, ckv_w, jnp.pad(kr_w, ((0, 0), (0, LANES - B_ROPE)))], axis=1)
    wuq = jnp.pad(w_uq.reshape(B_Q_RANK, B_HEADS, B_NOPE + B_ROPE),
                  ((0, 0), (0, 0), (0, LANES - B_NOPE - B_ROPE))).reshape(B_Q_RANK, B_HEADS * LANES)
    ukv = w_ukv.reshape(B_KV_RANK, B_HEADS, B_NOPE + B_V)
    wukn = jnp.pad(ukv[:, :, :B_NOPE], ((0, 0), (0, 0), (0, LANES - B_NOPE))).reshape(B_KV_RANK, B_HEADS * LANES)
    wuv = ukv[:, :, B_NOPE:].reshape(B_KV_RANK, B_HEADS * B_V)
    src = jnp.arange(LANES)[:, None]
    dst = jnp.arange(B_HEADS * LANES)[None, :]
    rp = ((src < B_ROPE) & (dst % LANES == src + B_NOPE)).astype(BF16)
    return dict(w1=w1.astype(BF16), wg=g_w.astype(BF16), qn=q_norm.reshape(1, -1), wuq=wuq.astype(BF16),
                kvn=kv_norm.reshape(1, -1), wukn=wukn.astype(BF16), wuv=wuv.astype(BF16), rp=rp,
                woa=w_o_a.astype(BF16), wob=w_o_b.astype(BF16), wout=w_out.astype(BF16))


def kernel(x, c, ctx, c_ctx, w_mod, b_mod, norm_mix, norm_ffn, w_in, attn_sink, mla_q_norm, w_uq, mla_kv_norm, w_ukv,
           w_o_a, w_o_b, w_out, w_router, router_bias, w_expert_gate, w_expert_up, w_expert_down, final_norm):
    batch, seq, d = x.shape
    ctx_len = ctx.shape[1]
    depth = w_mod.shape[0]
    n_lat = batch * seq
    n_all = n_lat + batch * ctx_len
    assert d == SUB * LANES and seq % TM_MERGE == 0 and (batch * ctx_len) % TM_MERGE == 0 and ctx_len % TM == 0
    assert seq % GRID_W == 0 and batch + 1 <= MOD_ROWS

    c_all = jnp.concatenate([c, c_ctx[None, :], jnp.zeros((MOD_ROWS - batch - 1, d), F32)], axis=0)
    mod = _mod_call(c_all, w_mod, b_mod)
    mod = jnp.pad(mod.transpose(0, 2, 1, 3), ((0, 0), (0, 0), (0, 2), (0, 0)))

    tab_a, tab_b = _tables(seq)
    wr = jnp.pad(w_router.astype(F32), ((0, 0), (0, LANES - N_EXPERTS)))
    wr_hi = wr.astype(BF16)
    wr2 = jnp.concatenate([wr_hi, (wr - wr_hi.astype(F32)).astype(BF16)], axis=1)
    rb = router_bias.astype(F32).reshape(N_EXPERTS, 1)
    wg_all = w_expert_gate.astype(BF16)
    wu_all = w_expert_up.astype(BF16)
    wd_all = w_expert_down.astype(BF16)

    xs = jnp.concatenate([x.reshape(n_lat, d), ctx.reshape(batch * ctx_len, d)], axis=0)
    f = None
    for l in range(depth):
        need_ctx = l < depth - 1
        w = _layer_weights(w_in[l], w_uq[l], w_ukv[l], mla_q_norm[l], mla_kv_norm[l], w_o_a[l], w_o_b[l], w_out[l])
        outs = _proj_call(xs, f, mod[l - 1] if l else None, mod[l], norm_mix[l].reshape(1, d), tab_a, tab_b, w,
                          n_lat=n_lat, seq=seq)
        if l:
            xs, outs = outs[0], outs[1:]
        qa, ka, va, qb, kb, vb, ga, gb = outs
        ya = _window_call(attn_sink[l].astype(F32), qa, ka, va, batch=batch, seq=seq, ctx_len=ctx_len, need_ctx=need_ctx)
        yb = _mla_call(qb, kb, vb, batch=batch, seq=seq, ctx_len=ctx_len, need_ctx=need_ctx)
        n_rows = n_all if need_ctx else n_lat
        xs, h3, meta = _merge_call(xs, ya, yb, ga, gb, mod[l], norm_ffn[l].reshape(1, d), w, wr2, rb,
                                   n_rows=n_rows, n_lat=n_lat, seq=seq)
        tok, e1, e2, nv, wts = _route_tables(meta, n_rows)
        f = _moe_call(tok, e1, e2, nv, wts, h3, wg_all, wu_all, wd_all, layer=l, n_tok=n_rows)
    out = _final_call(xs, f, mod[depth - 1], final_norm.reshape(1, d), n_lat=n_lat, seq=seq)
    return out.reshape(batch, seq, d)
```

```python
import functools

import numpy as np
import jax
import jax.numpy as jnp
from jax import lax
from jax.experimental import pallas as pl
from jax.experimental.pallas import tpu as pltpu

F32 = jnp.float32
BF16 = jnp.bfloat16

EPS = 1e-6
ROPE_BASE = 10000.0
GRID_W = 64
BLOCK = 128
A_HEADS, A_KV_HEADS, A_HEAD_DIM = 8, 2, 64
A_GROUP = A_HEADS // A_KV_HEADS
WINDOW = 128
B_HEADS, B_NOPE, B_ROPE, B_V = 8, 64, 32, 64
B_Q_RANK, B_KV_RANK = 256, 128
N_EXPERTS, N_GROUPS = 16, 4
EPG = N_EXPERTS // N_GROUPS
D_EXPERT = 512
PAIRS = ((0, 1), (0, 2), (0, 3), (1, 2), (1, 3), (2, 3))
N_BUCKETS = N_GROUPS * len(PAIRS)

LANES = 128
SUB = 8
TM = 512
TM_MERGE = 512
TQ = 512
TE = 256
MOE_BUFS = 3
MOD_ROWS = 24
NEG = -1e30
LOG2E = 1.4426950408889634
VMEM_LIMIT = 56 * 1024 * 1024

W1_QA, W1_KA, W1_VA, W1_CQ, W1_CKV, W1_KR = 0, 512, 768, 1024, 1280, 1408
W1_COLS = 1536


def _params(sem, vmem=VMEM_LIMIT):
    return pltpu.CompilerParams(dimension_semantics=sem, vmem_limit_bytes=vmem)


def _dot(a, b):
    return jnp.dot(a, b, preferred_element_type=F32)


def _dot_nt(a, b):
    return lax.dot_general(a, b, (((1,), (1,)), ((), ())), preferred_element_type=F32)


def _rms(x, g):
    return x * lax.rsqrt(jnp.mean(x * x, axis=-1, keepdims=True) + EPS) * g


def _lane_tile(t, width):
    return jnp.concatenate([t] * (width // LANES), axis=1)


def _rope(t, tab_ref, shift):
    w = t.shape[-1]
    up = pltpu.roll(t, w - shift, 1)
    dn = pltpu.roll(t, shift, 1)
    return (t * _lane_tile(tab_ref[0], w) + up * _lane_tile(tab_ref[1], w)
            + dn * _lane_tile(tab_ref[2], w))


def _load_token_rows(ref, rows):
    return jnp.concatenate([ref[pl.ds(s, rows, stride=SUB), :] for s in range(SUB)], axis=1)


def _store_token_rows(ref, val):
    rows = val.shape[0]
    for s in range(SUB):
        ref[pl.ds(s, rows, stride=SUB), :] = val[:, s * LANES:(s + 1) * LANES]


def _mod_kernel(c_ref, w_ref, b_ref, o_ref):
    c = c_ref[...]
    a = (c * jax.nn.sigmoid(c)).astype(BF16)
    o_ref[0, 0] = _dot(a, w_ref[0].astype(BF16)) + b_ref[0, 0]


def _mod_call(c_all, w_mod, b_mod):
    depth, d, _ = w_mod.shape
    return pl.pallas_call(
        _mod_kernel,
        grid=(depth, 6),
        in_specs=[pl.BlockSpec((MOD_ROWS, d), lambda l, k: (0, 0)),
                  pl.BlockSpec((1, d, d), lambda l, k: (l, 0, k)),
                  pl.BlockSpec((1, 1, 1, d), lambda l, k: (l, k, 0, 0))],
        out_specs=pl.BlockSpec((1, 1, MOD_ROWS, d), lambda l, k: (l, k, 0, 0)),
        out_shape=jax.ShapeDtypeStruct((depth, 6, MOD_ROWS, d), F32),
        compiler_params=_params(("arbitrary", "arbitrary")),
        name="mod",
    )(c_all, w_mod, b_mod.reshape(depth, 6, 1, d))


def _proj_kernel(*refs, has_f):
    if has_f:
        x_ref, f_ref, modp_ref, refs = refs[0], refs[1], refs[2], refs[3:]
    else:
        x_ref, refs = refs[0], refs[1:]
    (mod_ref, nw_ref, ta_ref, tb_ref, w1_ref, wg_ref, qn_ref, wuq_ref, kvn_ref, wukn_ref, wuv_ref,
     rp_ref), refs = refs[:12], refs[12:]
    if has_f:
        xo_ref, refs = refs[0], refs[1:]
    qa_ref, ka_ref, va_ref, qb_ref, kb_ref, vb_ref, ga_ref, gb_ref = refs

    x = x_ref[...]
    if has_f:
        x = x + modp_ref[0, 5:6, :] * _load_token_rows(f_ref, x.shape[0])
        xo_ref[...] = x
    h = (_rms(x, nw_ref[...]) * (1.0 + mod_ref[0, 1:2, :]) + mod_ref[0, 0:1, :]).astype(BF16)

    t = _dot(h, w1_ref[...])
    qa_ref[...] = (_rope(t[:, W1_QA:W1_KA], ta_ref, 16) * (A_HEAD_DIM ** -0.5 * LOG2E)).astype(BF16)
    ka_ref[...] = _rope(t[:, W1_KA:W1_VA], ta_ref, 16).astype(BF16)
    va_ref[...] = t[:, W1_VA:W1_CQ].astype(BF16)

    cq = _rms(t[:, W1_CQ:W1_CKV], qn_ref[...]).astype(BF16)
    qb = _rope(_dot(cq, wuq_ref[...]), tb_ref, 8)
    qb_ref[...] = (qb * ((B_NOPE + B_ROPE) ** -0.5 * LOG2E)).astype(BF16)

    ckv = _rms(t[:, W1_CKV:W1_KR], kvn_ref[...]).astype(BF16)
    vb_ref[...] = _dot(ckv, wuv_ref[...]).astype(BF16)
    kr = t[:, W1_KR:W1_COLS]
    kr_hi = kr.astype(BF16)
    kr_lo = (kr - kr_hi.astype(F32)).astype(BF16)
    kb = _dot(ckv, wukn_ref[...]) + _dot(kr_hi, rp_ref[...]) + _dot(kr_lo, rp_ref[...])
    kb_ref[...] = _rope(kb, tb_ref, 8).astype(BF16)

    d = ga_ref.shape[-1]
    ga_ref[...] = jax.nn.sigmoid(_dot(h, wg_ref[:, :d])).astype(BF16)
    gb_ref[...] = jax.nn.sigmoid(_dot(h, wg_ref[:, d:])).astype(BF16)


def _proj_call(x, f, modp, mod, nw, tab_a, tab_b, w, *, n_lat, seq):
    n, d = x.shape
    nt = n // TM
    n_lat_t = n_lat // TM
    tps = seq // TM
    ctx_row = n_lat // seq
    has_f = f is not None

    def tok(i):
        return (i, 0)

    def modi(i):
        return (jnp.where(i < n_lat_t, i // tps, ctx_row), 0, 0)

    def tabi(i):
        return (0, jnp.where(i < n_lat_t, i % tps, tps), 0)

    def full(a):
        return pl.BlockSpec(a.shape, lambda i: (0,) * a.ndim, pipeline_mode=pl.Buffered(1))

    mod_spec = pl.BlockSpec((1, 8, d), modi)
    in_specs = [pl.BlockSpec((TM, d), tok)]
    args = [x]
    if has_f:
        in_specs += [pl.BlockSpec((TM * SUB, LANES), tok), mod_spec]
        args += [f, modp]
    in_specs += [mod_spec, full(nw), pl.BlockSpec((3, TM, LANES), tabi), pl.BlockSpec((3, TM, LANES), tabi)]
    args += [mod, nw, tab_a, tab_b]
    for k in ("w1", "wg", "qn", "wuq", "kvn", "wukn", "wuv", "rp"):
        in_specs.append(full(w[k]))
        args.append(w[k])

    widths = (512, 256, 256, 1024, 1024, 512, d, d)
    out_shape = [jax.ShapeDtypeStruct((n, wd), BF16) for wd in widths]
    out_specs = [pl.BlockSpec((TM, wd), tok) for wd in widths]
    if has_f:
        out_shape.insert(0, jax.ShapeDtypeStruct((n, d), F32))
        out_specs.insert(0, pl.BlockSpec((TM, d), tok))
    return pl.pallas_call(
        functools.partial(_proj_kernel, has_f=has_f),
        grid=(nt,), in_specs=in_specs, out_specs=out_specs, out_shape=out_shape,
        compiler_params=_params(("arbitrary",)),
        name="proj",
    )(*args)


def _window_kernel(sink_ref, band_ref, q_ref, kp_ref, kc_ref, kn_ref, kx_ref, vp_ref, vc_ref, vn_ref, vx_ref, o_ref,
                   *, nb):
    n = pl.program_id(1)
    rows = A_GROUP * BLOCK
    lo = lax.broadcasted_iota(jnp.int32, (BLOCK, LANES), 1) < A_HEAD_DIM
    col_blk = lax.broadcasted_iota(jnp.int32, (1, 3 * BLOCK), 1) // BLOCK
    off_prev = jnp.where(jnp.logical_or(n == 0, n >= nb), NEG, 0.0)
    off_cur = jnp.where(n >= nb, NEG, 0.0)
    off_next = jnp.where(n >= nb - 1, NEG, 0.0)
    bias = band_ref[...] + jnp.where(col_blk == 0, off_prev, jnp.where(col_blk == 1, off_cur, off_next))
    bias = jnp.concatenate([bias] * A_GROUP, axis=0)
    row_head = lax.broadcasted_iota(jnp.int32, (rows, 1), 0) // BLOCK
    q = q_ref[...]
    zero = jnp.zeros((BLOCK, LANES), BF16)
    for k in range(A_KV_HEADS):
        ksl = slice(k * LANES, (k + 1) * LANES)
        parts = []
        for j in range(A_GROUP // 2):
            blk = q[:, k * 2 * LANES + j * LANES:k * 2 * LANES + (j + 1) * LANES]
            parts += [jnp.where(lo, blk, zero), jnp.where(lo, zero, blk)]
        qs = jnp.concatenate(parts, axis=0)
        k_lat = jnp.concatenate([kp_ref[:, ksl], kc_ref[:, ksl], kn_ref[:, ksl]], axis=0)
        v_lat = jnp.concatenate([vp_ref[:, ksl], vc_ref[:, ksl], vn_ref[:, ksl]], axis=0)
        s_l = _dot_nt(qs, k_lat) + bias
        s_x = _dot_nt(qs, kx_ref[:, ksl])
        sink = jnp.zeros((rows, 1), F32)
        for g in range(A_GROUP):
            sink = jnp.where(row_head == g, sink_ref[k * A_GROUP + g] * LOG2E, sink)
        m = jnp.maximum(jnp.maximum(jnp.max(s_l, axis=-1, keepdims=True), jnp.max(s_x, axis=-1, keepdims=True)), sink)
        e_l = jnp.exp2(s_l - m)
        e_x = jnp.exp2(s_x - m)
        den = jnp.sum(e_l, axis=-1, keepdims=True) + jnp.sum(e_x, axis=-1, keepdims=True) + jnp.exp2(sink - m)
        o = (_dot(e_l.astype(BF16), v_lat) + _dot(e_x.astype(BF16), vx_ref[:, ksl])) / den
        for j in range(A_GROUP // 2):
            even = o[(2 * j) * BLOCK:(2 * j + 1) * BLOCK]
            odd = o[(2 * j + 1) * BLOCK:(2 * j + 2) * BLOCK]
            o_ref[:, k * 2 * LANES + j * LANES:k * 2 * LANES + (j + 1) * LANES] = jnp.where(lo, even, odd).astype(BF16)


def _window_call(sink, qa, ka, va, *, batch, seq, ctx_len, need_ctx):
    n = qa.shape[0] if need_ctx else batch * seq
    nb = seq // BLOCK
    ncb = ctx_len // BLOCK
    lat_blocks = batch * nb
    nq = nb + (ncb if need_ctx else 0)

    def qi(b, i):
        return (jnp.where(i < nb, b * nb + i, lat_blocks + b * ncb + (i - nb)), 0)

    def ki(off):
        return lambda b, i: (b * nb + jnp.clip(i + off, 0, nb - 1), 0)

    def xi(b, i):
        return (batch * seq // ctx_len + b, 0)

    kv_specs = [pl.BlockSpec((BLOCK, 2 * LANES), ki(-1)), pl.BlockSpec((BLOCK, 2 * LANES), ki(0)),
                pl.BlockSpec((BLOCK, 2 * LANES), ki(1)), pl.BlockSpec((ctx_len, 2 * LANES), xi)]
    return pl.pallas_call(
        functools.partial(_window_kernel, nb=nb),
        grid=(batch, nq),
        in_specs=[pl.BlockSpec(memory_space=pltpu.SMEM), pl.BlockSpec((BLOCK, 3 * BLOCK), lambda b, i: (0, 0)),
                  pl.BlockSpec((BLOCK, 4 * LANES), qi)] + kv_specs + kv_specs,
        out_specs=pl.BlockSpec((BLOCK, 4 * LANES), qi),
        out_shape=jax.ShapeDtypeStruct((n, 4 * LANES), BF16),
        compiler_params=_params(("arbitrary", "arbitrary")),
        name="window",
    )(sink, _window_band(), qa, ka, ka, ka, ka, va, va, va, va)


def _mla_kernel(q_ref, *refs, with_lat):
    if with_lat:
        kl_ref, kx_ref, vl_ref, vx_ref, o_ref, vaug_ref = refs
    else:
        kx_ref, vx_ref, o_ref, vaug_ref = refs
    n_ctx = vx_ref.shape[0]
    lo = lax.broadcasted_iota(jnp.int32, o_ref.shape, 1) < B_V

    @pl.when(pl.program_id(2) == 0)
    def _():
        one = jnp.ones((1, LANES), BF16)
        for v_ref, start in ((vx_ref, 0),) + (((vl_ref, n_ctx),) if with_lat else ()):
            v = v_ref[...]
            keep = lax.broadcasted_iota(jnp.int32, v.shape, 1) < B_V
            vaug_ref[0, start:start + v.shape[0], :] = jnp.where(keep, v, one)
            vaug_ref[1, start:start + v.shape[0], :] = jnp.where(keep, one, v)

    def head(hh):
        sl = slice(hh * LANES, (hh + 1) * LANES)
        q = q_ref[:, sl]
        s_x = _dot_nt(q, kx_ref[:, sl])
        m = jnp.max(s_x, axis=-1, keepdims=True)
        if with_lat:
            s_l = _dot_nt(q, kl_ref[:, sl])
            m = jnp.maximum(m, jnp.max(s_l, axis=-1, keepdims=True))
        o = _dot(jnp.exp2(s_x - m).astype(BF16), vaug_ref[hh, :n_ctx, :])
        if with_lat:
            o = o + _dot(jnp.exp2(s_l - m).astype(BF16), vaug_ref[hh, n_ctx:, :])
        return o / pltpu.roll(o, B_V, 1)

    o_ref[...] = jnp.where(lo, head(0), head(1)).astype(BF16)


def _mla_call(qb, kb, vb, *, batch, seq, ctx_len, latent):
    ctx0 = batch * seq // ctx_len
    pairs = B_HEADS // 2
    tq = TQ if latent else ctx_len
    nq = seq // TQ if latent else 1
    kx_spec = pl.BlockSpec((ctx_len, 2 * LANES), lambda b, p, j: (ctx0 + b, p))
    vx_spec = pl.BlockSpec((ctx_len, LANES), lambda b, p, j: (ctx0 + b, p))
    if latent:
        in_specs = [pl.BlockSpec((tq, 2 * LANES), lambda b, p, j: (b * nq + j, p)),
                    pl.BlockSpec((seq, 2 * LANES), lambda b, p, j: (b, p)), kx_spec,
                    pl.BlockSpec((seq, LANES), lambda b, p, j: (b, p)), vx_spec]
        args = (qb, kb, kb, vb, vb)
    else:
        in_specs = [pl.BlockSpec((tq, 2 * LANES), lambda b, p, j: (ctx0 + b, p)), kx_spec, vx_spec]
        args = (qb, kb, vb)
    return pl.pallas_call(
        functools.partial(_mla_kernel, with_lat=latent),
        grid=(batch, pairs, nq),
        in_specs=in_specs,
        out_specs=pl.BlockSpec((tq, LANES), lambda b, p, j: (b * nq + j, p)),
        out_shape=jax.ShapeDtypeStruct((batch * nq * tq, pairs * LANES), BF16),
        scratch_shapes=[pltpu.VMEM((2, ctx_len + (seq if latent else 0), LANES), BF16)],
        compiler_params=_params(("arbitrary", "arbitrary", "arbitrary")),
        name="mla" if latent else "mla_ctx",
    )(*args)


def _top2(v):
    i1 = jnp.zeros_like(v[0])
    m1 = v[0]
    for i in range(1, EPG):
        u = v[i] > m1
        i1 = jnp.where(u, float(i), i1)
        m1 = jnp.where(u, v[i], m1)
    i2 = jnp.zeros_like(v[0])
    m2 = jnp.full_like(v[0], -jnp.inf)
    for i in range(EPG):
        cand = jnp.where(i1 == float(i), -jnp.inf, v[i])
        u = cand > m2
        i2 = jnp.where(u, float(i), i2)
        m2 = jnp.where(u, cand, m2)
    return i1, i2, m1, m2


def _pick(idx, vals):
    out = vals[0]
    for i in range(1, len(vals)):
        out = jnp.where(idx == float(i), vals[i], out)
    return out


def _merge_kernel(x_ref, ya_ref, yb_ref, ga_ref, gb_ref, mod_ref, nw_ref, woa_ref, wob_ref, wout_ref, wr2_ref, rb_ref,
                  xo_ref, h_ref, meta_ref):
    a = _dot(ya_ref[...], woa_ref[...])
    b = _dot(yb_ref[...], wob_ref[...])
    mix = (ga_ref[...].astype(F32) * a + gb_ref[...].astype(F32) * b).astype(BF16)
    x = x_ref[...] + mod_ref[0, 2:3, :] * _dot(mix, wout_ref[...])
    xo_ref[...] = x
    h = _rms(x, nw_ref[...]) * (1.0 + mod_ref[0, 4:5, :]) + mod_ref[0, 3:4, :]
    _store_token_rows(h_ref, h)

    h_hi = h.astype(BF16)
    h_lo = (h - h_hi.astype(F32)).astype(BF16)
    l_hi = _dot(h_hi, wr2_ref[...])
    logits = l_hi[:, :LANES] + l_hi[:, LANES:] + _dot(h_lo, wr2_ref[:, :LANES])
    sc = jax.nn.sigmoid(logits.T[:N_EXPERTS, :])
    sel = sc + rb_ref[...]
    sel_rows = [sel[e:e + 1, :] for e in range(N_EXPERTS)]
    sc_rows = [sc[e:e + 1, :] for e in range(N_EXPERTS)]
    best = jnp.zeros_like(sel_rows[0])
    best_v = None
    for g in range(N_GROUPS):
        _, _, m1, m2 = _top2(sel_rows[g * EPG:(g + 1) * EPG])
        gv = m1 + m2
        if best_v is None:
            best_v = gv
        else:
            u = gv > best_v
            best = jnp.where(u, float(g), best)
            best_v = jnp.where(u, gv, best_v)
    sel_g = [_pick(best, [sel_rows[g * EPG + i] for g in range(N_GROUPS)]) for i in range(EPG)]
    sc_g = [_pick(best, [sc_rows[g * EPG + i] for g in range(N_GROUPS)]) for i in range(EPG)]
    i1, i2, _, _ = _top2(sel_g)
    s1 = _pick(i1, sc_g)
    s2 = _pick(i2, sc_g)
    tot = s1 + s2
    first_low = i1 < i2
    e_lo = jnp.where(first_low, i1, i2)
    e_hi = jnp.where(first_low, i2, i1)
    w_lo = jnp.where(first_low, s1, s2) / tot
    w_hi = jnp.where(first_low, s2, s1) / tot
    pid = jnp.where(e_lo == 0.0, e_hi - 1.0, jnp.where(e_lo == 1.0, e_hi + 1.0, 5.0))
    bucket = best * float(len(PAIRS)) + pid
    t = sel.shape[1]
    meta_ref[0] = jnp.concatenate([w_lo, w_hi, bucket, jnp.zeros((SUB - 3, t), F32)], axis=0)


def _merge_call(x, ya, yb, ga, gb, mod, nw, w, wr2, rb, *, n_rows, n_lat, seq):
    d = x.shape[1]
    tm = TM_MERGE
    nt = n_rows // tm
    n_lat_t = n_lat // tm
    tps = seq // tm
    ctx_row = n_lat // seq

    def tok(i):
        return (i, 0)

    def full(a):
        return pl.BlockSpec(a.shape, lambda i: (0,) * a.ndim)

    return pl.pallas_call(
        _merge_kernel,
        grid=(nt,),
        in_specs=[pl.BlockSpec((tm, d), tok), pl.BlockSpec((tm, 512), tok), pl.BlockSpec((tm, 512), tok),
                  pl.BlockSpec((tm, d), tok), pl.BlockSpec((tm, d), tok),
                  pl.BlockSpec((1, 8, d), lambda i: (jnp.where(i < n_lat_t, i // tps, ctx_row), 0, 0)),
                  full(nw), full(w["woa"]), full(w["wob"]), full(w["wout"]), full(wr2), full(rb)],
        out_specs=[pl.BlockSpec((tm, d), tok), pl.BlockSpec((tm * SUB, LANES), tok),
                   pl.BlockSpec((1, SUB, tm), lambda i: (i, 0, 0))],
        out_shape=[jax.ShapeDtypeStruct((n_rows, d), F32), jax.ShapeDtypeStruct((n_rows * SUB, LANES), F32),
                   jax.ShapeDtypeStruct((nt, SUB, tm), F32)],
        compiler_params=_params(("arbitrary",)),
        name="merge",
    )(x, ya, yb, ga, gb, mod, nw, w["woa"], w["wob"], w["wout"], wr2, rb)


def _moe_kernel(tok_ref, e1_ref, e2_ref, nv_ref, wts_ref, h_hbm, wg1_ref, wu1_ref, wd1_ref, wg2_ref, wu2_ref, wd2_ref,
                f_hbm, hbuf, obuf, gsem, ssem, *, n_tok, n_tiles):
    t = pl.program_id(0)
    slot = t % MOE_BUFS
    slot1 = (t + 1) % MOE_BUFS
    slot2 = (t + 2) % MOE_BUFS
    live = nv_ref[t] > 0
    prev_live = jnp.logical_and(t >= 1, nv_ref[jnp.maximum(t - 1, 0)] > 0)

    def gather_row(tile, s, r):
        tk = jnp.minimum(tok_ref[tile * TE + r], n_tok - 1)
        return pltpu.make_async_copy(h_hbm.at[pl.ds(tk * SUB, SUB)], hbuf.at[s, pl.ds(r * SUB, SUB)], gsem.at[s])

    def scatter_row(tile, s, r):
        tk = tok_ref[tile * TE + r]
        return pltpu.make_async_copy(obuf.at[s, pl.ds(r * SUB, SUB)], f_hbm.at[pl.ds(tk * SUB, SUB)], ssem.at[s])

    def wait_gather(s):
        pltpu.make_async_copy(h_hbm.at[pl.ds(0, TE * SUB)], hbuf.at[s], gsem.at[s]).wait()

    def wait_scatter(s):
        pltpu.make_async_copy(obuf.at[s], f_hbm.at[pl.ds(0, TE * SUB)], ssem.at[s]).wait()

    def start_rows(make_row, tile, s):
        def body(i, carry):
            make_row(tile, s, 2 * i).start(priority=0)
            make_row(tile, s, 2 * i + 1).start(priority=1)
            return carry
        lax.fori_loop(0, TE // 2, body, 0, unroll=4)

    @pl.when(t == 0)
    def _():
        obuf[...] = jnp.zeros(obuf.shape, F32)
        for half in range(2):
            fill = pltpu.make_async_copy(obuf.at[0], f_hbm.at[pl.ds((n_tok + half * TE) * SUB, TE * SUB)], ssem.at[0])
            fill.start()
            fill.wait()
        start_rows(gather_row, 0, 0)
        start_rows(gather_row, 1, 1)

    @pl.when(live)
    def _():
        start_rows(gather_row, jnp.minimum(t + 2, n_tiles - 1), slot2)
        start_rows(scatter_row, jnp.where(t == 0, n_tiles - 1, t - 1), slot2)
        wait_gather(slot)

        @pl.when(t >= 2)
        def _():
            wait_scatter(slot)

    @pl.when(live)
    def _():
        h = _load_token_rows(hbuf.at[slot], TE).astype(BF16)
        w_lo = jnp.broadcast_to(wts_ref[0, 0:1, :], (LANES, TE)).T[:, :1]
        w_hi = jnp.broadcast_to(wts_ref[0, 1:2, :], (LANES, TE)).T[:, :1]
        outs = []
        for wg_ref, wu_ref, wd_ref, wt in ((wg1_ref, wu1_ref, wd1_ref, w_lo), (wg2_ref, wu2_ref, wd2_ref, w_hi)):
            hg = _dot(h, wg_ref[0, 0])
            hu = _dot(h, wu_ref[0, 0])
            act = (hg * jax.nn.sigmoid(hg) * hu).astype(BF16)
            outs.append(wt * _dot(act, wd_ref[0, 0]))
        _store_token_rows(obuf.at[slot], outs[0] + outs[1])

    @pl.when(jnp.logical_and(jnp.logical_not(live), prev_live))
    def _():
        wait_gather(slot)
        wait_gather(slot1)
        wait_scatter(slot1)

        @pl.when(t >= 2)
        def _():
            wait_scatter(slot)
        start_rows(scatter_row, t - 1, slot2)
        wait_scatter(slot2)


def _moe_call(tok, e1, e2, nv, wts, h3, wg, wu, wd, *, layer, n_tok):
    d = wg.shape[2]
    de = wg.shape[3]
    n_tiles = e1.shape[0]

    def wspec(shape, which):
        return pl.BlockSpec((1, 1) + shape, lambda t, tok, e1, e2, nv: (layer, (e1, e2)[which][t], 0, 0))

    grid_spec = pltpu.PrefetchScalarGridSpec(
        num_scalar_prefetch=4,
        grid=(n_tiles,),
        in_specs=[pl.BlockSpec((1, 2, TE), lambda t, tok, e1, e2, nv: (t, 0, 0)),
                  pl.BlockSpec(memory_space=pl.ANY),
                  wspec((d, de), 0), wspec((d, de), 0), wspec((de, d), 0),
                  wspec((d, de), 1), wspec((d, de), 1), wspec((de, d), 1)],
        out_specs=pl.BlockSpec(memory_space=pl.ANY),
        scratch_shapes=[pltpu.VMEM((MOE_BUFS, TE * SUB, LANES), F32), pltpu.VMEM((MOE_BUFS, TE * SUB, LANES), F32),
                        pltpu.SemaphoreType.DMA((MOE_BUFS,)), pltpu.SemaphoreType.DMA((MOE_BUFS,))],
    )
    return pl.pallas_call(
        functools.partial(_moe_kernel, n_tok=n_tok, n_tiles=n_tiles),
        grid_spec=grid_spec,
        out_shape=jax.ShapeDtypeStruct(((n_tok + 2 * TE) * SUB, LANES), F32),
        compiler_params=_params(("arbitrary",)),
        name="moe",
    )(tok, e1, e2, nv, wts, h3, wg, wu, wd, wg, wu, wd)


def _route_tables(meta, n_tok):
    w_lo = meta[:, 0, :].reshape(-1)
    w_hi = meta[:, 1, :].reshape(-1)
    bucket = meta[:, 2, :].reshape(-1).astype(jnp.int32)
    n_slots = n_tok + N_BUCKETS * TE
    n_tiles = n_slots // TE
    buckets = jnp.arange(N_BUCKETS, dtype=jnp.int32)
    counts = jnp.sum(bucket[:, None] == buckets[None, :], axis=0, dtype=jnp.int32)
    padded = ((counts + TE - 1) // TE) * TE
    pad_end = jnp.cumsum(padded)
    pad_start = pad_end - padded
    fill_i = jnp.arange(TE, dtype=jnp.int32)[None, :]
    fill_key = jnp.where(fill_i < (padded - counts)[:, None], 2 * buckets[:, None] + 1, 2 * N_BUCKETS).reshape(-1)
    keys = jnp.concatenate([2 * bucket, fill_key])
    ids = jnp.concatenate([jnp.arange(n_tok, dtype=jnp.int32), jnp.full((N_BUCKETS * TE,), -1, jnp.int32)])
    zeros = jnp.zeros((N_BUCKETS * TE,), F32)
    _, ids, w_lo, w_hi = lax.sort((keys, ids, jnp.concatenate([w_lo, zeros]), jnp.concatenate([w_hi, zeros])),
                                  num_keys=1, is_stable=True)
    slot = jnp.arange(n_slots, dtype=jnp.int32)
    tok = jnp.where(ids < 0, n_tok + ((slot // TE) % 2) * TE + slot % TE, ids)
    wts = jnp.stack([w_lo.reshape(n_tiles, TE), w_hi.reshape(n_tiles, TE)], axis=1)

    tile_start = jnp.arange(n_tiles, dtype=jnp.int32) * TE
    tb = jnp.sum(tile_start[:, None] >= pad_end[None, :], axis=1, dtype=jnp.int32)
    used = tb < N_BUCKETS
    onehot = (tb[:, None] == buckets[None, :]).astype(jnp.int32)
    nv = jnp.clip(jnp.sum(onehot * (counts + pad_start)[None, :], axis=1) - tile_start, 0, TE) * used
    tbe = jnp.where(used, tb, jnp.max(jnp.where(used, tb, 0)))
    pid = tbe % len(PAIRS)
    pair_lo = (pid >= 3).astype(jnp.int32) + (pid >= 5).astype(jnp.int32)
    pair_hi = pid + 1 - 2 * (pid >= 3).astype(jnp.int32) - (pid >= 5).astype(jnp.int32)
    e1 = (tbe // len(PAIRS)) * EPG + pair_lo
    e2 = (tbe // len(PAIRS)) * EPG + pair_hi
    return tok.astype(jnp.int32), e1.astype(jnp.int32), e2.astype(jnp.int32), nv.astype(jnp.int32), wts


def _final_kernel(x_ref, f_ref, mod_ref, nw_ref, o_ref):
    x = x_ref[...] + mod_ref[0, 5:6, :] * _load_token_rows(f_ref, x_ref.shape[0])
    o_ref[...] = _rms(x, nw_ref[...])


def _final_call(x, f, mod, nw, *, n_lat, seq):
    d = x.shape[1]
    tm = TM_MERGE
    tps = seq // tm
    return pl.pallas_call(
        _final_kernel,
        grid=(n_lat // tm,),
        in_specs=[pl.BlockSpec((tm, d), lambda i: (i, 0)), pl.BlockSpec((tm * SUB, LANES), lambda i: (i, 0)),
                  pl.BlockSpec((1, 8, d), lambda i: (i // tps, 0, 0)), pl.BlockSpec((1, d), lambda i: (0, 0))],
        out_specs=pl.BlockSpec((tm, d), lambda i: (i, 0)),
        out_shape=jax.ShapeDtypeStruct((n_lat, d), F32),
        compiler_params=_params(("arbitrary",)),
        name="final",
    )(x, f, mod, nw)


def _rope_table(pos_r, pos_c, dim, lane_of):
    d = dim // 2
    half = d // 2
    lane = np.arange(LANES)
    rl = lane_of(lane)
    is_rope = rl >= 0
    rl = np.maximum(rl, 0)
    use_col = rl >= d
    j = rl % half
    first = (rl % d) < half
    inv = (np.float32(ROPE_BASE) ** (-(2.0 * j).astype(np.float32) / np.float32(d))).astype(np.float32)
    pos = np.where(use_col[None, :], pos_c[:, None], pos_r[:, None]).astype(np.float32)
    ang = pos * inv[None, :]
    cos = np.where(is_rope[None, :], np.cos(ang), 1.0)
    sin = np.where(is_rope[None, :], np.sin(ang), 0.0)
    return np.stack([cos, np.where(first[None, :], -sin, 0.0), np.where(first[None, :], 0.0, sin)]).astype(np.float32)


def _tables(seq):
    t = np.arange(seq)
    rows, cols = t // GRID_W, t % GRID_W
    tab_a = _rope_table(rows, cols, A_HEAD_DIM, lambda lane: lane % A_HEAD_DIM)
    tab_b = _rope_table(rows, cols, B_ROPE,
                        lambda lane: np.where((lane >= B_NOPE) & (lane < B_NOPE + B_ROPE), lane - B_NOPE, -1))
    ident = np.stack([np.ones((TM, LANES), np.float32), np.zeros((TM, LANES), np.float32),
                      np.zeros((TM, LANES), np.float32)])
    return jnp.asarray(np.concatenate([tab_a, ident], axis=1)), jnp.asarray(np.concatenate([tab_b, ident], axis=1))


def _window_band():
    r = np.arange(BLOCK)[:, None]
    c = np.arange(3 * BLOCK)[None, :]
    dist = c - r
    return jnp.asarray(np.where((dist >= BLOCK - WINDOW) & (dist <= BLOCK + WINDOW), 0.0, NEG).astype(np.float32))


def _layer_weights(w_in, w_uq, w_ukv, q_norm, kv_norm, w_o_a, w_o_b, w_out):
    d = w_in.shape[0]
    o = 0
    qa_w = w_in[:, o:o + 512]; o += 512
    ka_w = w_in[:, o:o + 128]; o += 128
    va_w = w_in[:, o:o + 128]; o += 128
    cq_w = w_in[:, o:o + B_Q_RANK]; o += B_Q_RANK
    ckv_w = w_in[:, o:o + B_KV_RANK]; o += B_KV_RANK
    kr_w = w_in[:, o:o + B_ROPE]; o += B_ROPE
    g_w = w_in[:, o:]

    def dup(wk):
        wk = wk.reshape(d, A_KV_HEADS, 1, A_HEAD_DIM)
        return jnp.broadcast_to(wk, (d, A_KV_HEADS, 2, A_HEAD_DIM)).reshape(d, A_KV_HEADS * 2 * A_HEAD_DIM)

    w1 = jnp.concatenate([qa_w, dup(ka_w), dup(va_w), cq_w, ckv_w, jnp.pad(kr_w, ((0, 0), (0, LANES - B_ROPE)))], axis=1)
    wuq = jnp.pad(w_uq.reshape(B_Q_RANK, B_HEADS, B_NOPE + B_ROPE),
                  ((0, 0), (0, 0), (0, LANES - B_NOPE - B_ROPE))).reshape(B_Q_RANK, B_HEADS * LANES)
    ukv = w_ukv.reshape(B_KV_RANK, B_HEADS, B_NOPE + B_V)
    wukn = jnp.pad(ukv[:, :, :B_NOPE], ((0, 0), (0, 0), (0, LANES - B_NOPE))).reshape(B_KV_RANK, B_HEADS * LANES)
    wuv = ukv[:, :, B_NOPE:].reshape(B_KV_RANK, B_HEADS * B_V)
    src = jnp.arange(LANES)[:, None]
    dst = jnp.arange(B_HEADS * LANES)[None, :]
    rp = ((src < B_ROPE) & (dst % LANES == src + B_NOPE)).astype(BF16)
    return dict(w1=w1.astype(BF16), wg=g_w.astype(BF16), qn=q_norm.reshape(1, -1), wuq=wuq.astype(BF16),
                kvn=kv_norm.reshape(1, -1), wukn=wukn.astype(BF16), wuv=wuv.astype(BF16), rp=rp,
                woa=w_o_a.astype(BF16), wob=w_o_b.astype(BF16), wout=w_out.astype(BF16))


def kernel(x, c, ctx, c_ctx, w_mod, b_mod, norm_mix, norm_ffn, w_in, attn_sink, mla_q_norm, w_uq, mla_kv_norm, w_ukv,
           w_o_a, w_o_b, w_out, w_router, router_bias, w_expert_gate, w_expert_up, w_expert_down, final_norm):
    batch, seq, d = x.shape
    ctx_len = ctx.shape[1]
    depth = w_mod.shape[0]
    n_lat = batch * seq
    n_all = n_lat + batch * ctx_len
    for tile in (TM, TM_MERGE):
        assert seq % tile == 0 and (batch * ctx_len) % tile == 0
    assert d == SUB * LANES and seq % TQ == 0 and seq % GRID_W == 0 and batch + 1 <= MOD_ROWS

    c_all = jnp.concatenate([c, c_ctx[None, :], jnp.zeros((MOD_ROWS - batch - 1, d), F32)], axis=0)
    mod = _mod_call(c_all, w_mod, b_mod)
    mod = jnp.pad(mod.transpose(0, 2, 1, 3), ((0, 0), (0, 0), (0, 2), (0, 0)))

    tab_a, tab_b = _tables(seq)
    wr = jnp.pad(w_router.astype(F32), ((0, 0), (0, LANES - N_EXPERTS)))
    wr_hi = wr.astype(BF16)
    wr2 = jnp.concatenate([wr_hi, (wr - wr_hi.astype(F32)).astype(BF16)], axis=1)
    rb = router_bias.astype(F32).reshape(N_EXPERTS, 1)
    wg_all = w_expert_gate.astype(BF16)
    wu_all = w_expert_up.astype(BF16)
    wd_all = w_expert_down.astype(BF16)

    xs = jnp.concatenate([x.reshape(n_lat, d), ctx.reshape(batch * ctx_len, d)], axis=0)
    f = None
    for l in range(depth):
        need_ctx = l < depth - 1
        w = _layer_weights(w_in[l], w_uq[l], w_ukv[l], mla_q_norm[l], mla_kv_norm[l], w_o_a[l], w_o_b[l], w_out[l])
        outs = _proj_call(xs, f, mod[l - 1] if l else None, mod[l], norm_mix[l].reshape(1, d), tab_a, tab_b, w,
                          n_lat=n_lat, seq=seq)
        if l:
            xs, outs = outs[0], outs[1:]
        qa, ka, va, qb, kb, vb, ga, gb = outs
        ya = _window_call(attn_sink[l].astype(F32), qa, ka, va, batch=batch, seq=seq, ctx_len=ctx_len, need_ctx=need_ctx)
        yb = _mla_call(qb, kb, vb, batch=batch, seq=seq, ctx_len=ctx_len, latent=True)
        if need_ctx:
            yb = jnp.concatenate([yb, _mla_call(qb, kb, vb, batch=batch, seq=seq, ctx_len=ctx_len, latent=False)], axis=0)
        n_rows = n_all if need_ctx else n_lat
        xs, h3, meta = _merge_call(xs, ya, yb, ga, gb, mod[l], norm_ffn[l].reshape(1, d), w, wr2, rb,
                                   n_rows=n_rows, n_lat=n_lat, seq=seq)
        tok, e1, e2, nv, wts = _route_tables(meta, n_rows)
        f = _moe_call(tok, e1, e2, nv, wts, h3, wg_all, wu_all, wd_all, layer=l, n_tok=n_rows)
    out = _final_call(xs, f, mod[depth - 1], final_norm.reshape(1, d), n_lat=n_lat, seq=seq)
    return out.reshape(batch, seq, d)
```

```python
import functools

import numpy as np
import jax
import jax.numpy as jnp
from jax import lax
from jax.experimental import pallas as pl
from jax.experimental.pallas import tpu as pltpu

F32 = jnp.float32
BF16 = jnp.bfloat16

EPS = 1e-6
ROPE_BASE = 10000.0
GRID_W = 64
BLOCK = 128
A_HEADS, A_KV_HEADS, A_HEAD_DIM = 8, 2, 64
A_GROUP = A_HEADS // A_KV_HEADS
WINDOW = 128
B_HEADS, B_NOPE, B_ROPE, B_V = 8, 64, 32, 64
B_Q_RANK, B_KV_RANK = 256, 128
N_EXPERTS, N_GROUPS = 16, 4
EPG = N_EXPERTS // N_GROUPS
D_EXPERT = 512
PAIRS = ((0, 1), (0, 2), (0, 3), (1, 2), (1, 3), (2, 3))
N_BUCKETS = N_GROUPS * len(PAIRS)

LANES = 128
SUB = 8
TM = 512
TM_MERGE = 512
TQ = 512
TE = 256
MOE_BUFS = 3
MOD_ROWS = 24
NEG = -1e30
LOG2E = 1.4426950408889634
VMEM_LIMIT = 56 * 1024 * 1024

W1_QA, W1_KA, W1_VA, W1_CQ, W1_CKV, W1_KR = 0, 512, 768, 1024, 1280, 1408
W1_COLS = 1536


def _params(sem, vmem=VMEM_LIMIT):
    return pltpu.CompilerParams(dimension_semantics=sem, vmem_limit_bytes=vmem)


def _dot(a, b):
    return jnp.dot(a, b, preferred_element_type=F32)


def _dot_nt(a, b):
    return lax.dot_general(a, b, (((1,), (1,)), ((), ())), preferred_element_type=F32)


def _rms(x, g):
    return x * lax.rsqrt(jnp.mean(x * x, axis=-1, keepdims=True) + EPS) * g


def _lane_tile(t, width):
    return jnp.concatenate([t] * (width // LANES), axis=1)


def _rope(t, tab_ref, shift):
    w = t.shape[-1]
    up = pltpu.roll(t, w - shift, 1)
    dn = pltpu.roll(t, shift, 1)
    return (t * _lane_tile(tab_ref[0], w) + up * _lane_tile(tab_ref[1], w)
            + dn * _lane_tile(tab_ref[2], w))


def _load_token_rows(ref, rows):
    return jnp.concatenate([ref[pl.ds(s, rows, stride=SUB), :] for s in range(SUB)], axis=1)


def _store_token_rows(ref, val):
    rows = val.shape[0]
    for s in range(SUB):
        ref[pl.ds(s, rows, stride=SUB), :] = val[:, s * LANES:(s + 1) * LANES]


def _mod_kernel(c_ref, w_ref, b_ref, o_ref):
    c = c_ref[...]
    a = (c * jax.nn.sigmoid(c)).astype(BF16)
    o_ref[0, 0] = _dot(a, w_ref[0].astype(BF16)) + b_ref[0, 0]


def _mod_call(c_all, w_mod, b_mod):
    depth, d, _ = w_mod.shape
    return pl.pallas_call(
        _mod_kernel,
        grid=(depth, 6),
        in_specs=[pl.BlockSpec((MOD_ROWS, d), lambda l, k: (0, 0)),
                  pl.BlockSpec((1, d, d), lambda l, k: (l, 0, k)),
                  pl.BlockSpec((1, 1, 1, d), lambda l, k: (l, k, 0, 0))],
        out_specs=pl.BlockSpec((1, 1, MOD_ROWS, d), lambda l, k: (l, k, 0, 0)),
        out_shape=jax.ShapeDtypeStruct((depth, 6, MOD_ROWS, d), F32),
        compiler_params=_params(("arbitrary", "arbitrary")),
        name="mod",
    )(c_all, w_mod, b_mod.reshape(depth, 6, 1, d))


def _proj_kernel(*refs, has_f):
    if has_f:
        x_ref, f_ref, modp_ref, refs = refs[0], refs[1], refs[2], refs[3:]
    else:
        x_ref, refs = refs[0], refs[1:]
    (mod_ref, nw_ref, ta_ref, tb_ref, w1_ref, wg_ref, qn_ref, wuq_ref, kvn_ref, wukn_ref, wuv_ref,
     rp_ref), refs = refs[:12], refs[12:]
    if has_f:
        xo_ref, refs = refs[0], refs[1:]
    qa_ref, ka_ref, va_ref, qb_ref, kb_ref, vb_ref, ga_ref, gb_ref = refs

    x = x_ref[...]
    if has_f:
        x = x + modp_ref[0, 5:6, :] * _load_token_rows(f_ref, x.shape[0])
        xo_ref[...] = x
    h = (_rms(x, nw_ref[...]) * (1.0 + mod_ref[0, 1:2, :]) + mod_ref[0, 0:1, :]).astype(BF16)

    t = _dot(h, w1_ref[...])
    qa_ref[...] = (_rope(t[:, W1_QA:W1_KA], ta_ref, 16) * (A_HEAD_DIM ** -0.5 * LOG2E)).astype(BF16)
    ka_ref[...] = _rope(t[:, W1_KA:W1_VA], ta_ref, 16).astype(BF16)
    va_ref[...] = t[:, W1_VA:W1_CQ].astype(BF16)

    cq = _rms(t[:, W1_CQ:W1_CKV], qn_ref[...]).astype(BF16)
    qb = _rope(_dot(cq, wuq_ref[...]), tb_ref, 8)
    qb_ref[...] = (qb * ((B_NOPE + B_ROPE) ** -0.5 * LOG2E)).astype(BF16)

    ckv = _rms(t[:, W1_CKV:W1_KR], kvn_ref[...]).astype(BF16)
    vb_ref[...] = _dot(ckv, wuv_ref[...]).astype(BF16)
    kr = t[:, W1_KR:W1_COLS]
    kr_hi = kr.astype(BF16)
    kr_lo = (kr - kr_hi.astype(F32)).astype(BF16)
    kb = _dot(ckv, wukn_ref[...]) + _dot(kr_hi, rp_ref[...]) + _dot(kr_lo, rp_ref[...])
    kb_ref[...] = _rope(kb, tb_ref, 8).astype(BF16)

    d = ga_ref.shape[-1]
    ga_ref[...] = jax.nn.sigmoid(_dot(h, wg_ref[:, :d])).astype(BF16)
    gb_ref[...] = jax.nn.sigmoid(_dot(h, wg_ref[:, d:])).astype(BF16)


def _proj_call(x, f, modp, mod, nw, tab_a, tab_b, w, *, n_lat, seq):
    n, d = x.shape
    nt = n // TM
    n_lat_t = n_lat // TM
    tps = seq // TM
    ctx_row = n_lat // seq
    has_f = f is not None

    def tok(i):
        return (i, 0)

    def modi(i):
        return (jnp.where(i < n_lat_t, i // tps, ctx_row), 0, 0)

    def tabi(i):
        return (0, jnp.where(i < n_lat_t, i % tps, tps), 0)

    def full(a):
        return pl.BlockSpec(a.shape, lambda i: (0,) * a.ndim, pipeline_mode=pl.Buffered(1))

    mod_spec = pl.BlockSpec((1, 8, d), modi)
    in_specs = [pl.BlockSpec((TM, d), tok)]
    args = [x]
    if has_f:
        in_specs += [pl.BlockSpec((TM * SUB, LANES), tok), mod_spec]
        args += [f, modp]
    in_specs += [mod_spec, full(nw), pl.BlockSpec((3, TM, LANES), tabi), pl.BlockSpec((3, TM, LANES), tabi)]
    args += [mod, nw, tab_a, tab_b]
    for k in ("w1", "wg", "qn", "wuq", "kvn", "wukn", "wuv", "rp"):
        in_specs.append(full(w[k]))
        args.append(w[k])

    widths = (512, 256, 256, 1024, 1024, 512, d, d)
    out_shape = [jax.ShapeDtypeStruct((n, wd), BF16) for wd in widths]
    out_specs = [pl.BlockSpec((TM, wd), tok) for wd in widths]
    if has_f:
        out_shape.insert(0, jax.ShapeDtypeStruct((n, d), F32))
        out_specs.insert(0, pl.BlockSpec((TM, d), tok))
    return pl.pallas_call(
        functools.partial(_proj_kernel, has_f=has_f),
        grid=(nt,), in_specs=in_specs, out_specs=out_specs, out_shape=out_shape,
        compiler_params=_params(("arbitrary",)),
        name="proj",
    )(*args)


def _window_kernel(sink_ref, band_ref, q_ref, kp_ref, kc_ref, kn_ref, kx_ref, vp_ref, vc_ref, vn_ref, vx_ref, o_ref,
                   *, nb):
    n = pl.program_id(1)
    rows = A_GROUP * BLOCK
    lo = lax.broadcasted_iota(jnp.int32, (BLOCK, LANES), 1) < A_HEAD_DIM
    col_blk = lax.broadcasted_iota(jnp.int32, (1, 3 * BLOCK), 1) // BLOCK
    off_prev = jnp.where(jnp.logical_or(n == 0, n >= nb), NEG, 0.0)
    off_cur = jnp.where(n >= nb, NEG, 0.0)
    off_next = jnp.where(n >= nb - 1, NEG, 0.0)
    bias = band_ref[...] + jnp.where(col_blk == 0, off_prev, jnp.where(col_blk == 1, off_cur, off_next))
    bias = jnp.concatenate([bias] * A_GROUP, axis=0)
    row_head = lax.broadcasted_iota(jnp.int32, (rows, 1), 0) // BLOCK
    q = q_ref[...]
    zero = jnp.zeros((BLOCK, LANES), BF16)
    for k in range(A_KV_HEADS):
        ksl = slice(k * LANES, (k + 1) * LANES)
        parts = []
        for j in range(A_GROUP // 2):
            blk = q[:, k * 2 * LANES + j * LANES:k * 2 * LANES + (j + 1) * LANES]
            parts += [jnp.where(lo, blk, zero), jnp.where(lo, zero, blk)]
        qs = jnp.concatenate(parts, axis=0)
        k_lat = jnp.concatenate([kp_ref[:, ksl], kc_ref[:, ksl], kn_ref[:, ksl]], axis=0)
        v_lat = jnp.concatenate([vp_ref[:, ksl], vc_ref[:, ksl], vn_ref[:, ksl]], axis=0)
        s_l = _dot_nt(qs, k_lat) + bias
        s_x = _dot_nt(qs, kx_ref[:, ksl])
        sink = jnp.zeros((rows, 1), F32)
        for g in range(A_GROUP):
            sink = jnp.where(row_head == g, sink_ref[k * A_GROUP + g] * LOG2E, sink)
        m = jnp.maximum(jnp.maximum(jnp.max(s_l, axis=-1, keepdims=True), jnp.max(s_x, axis=-1, keepdims=True)), sink)
        e_l = jnp.exp2(s_l - m)
        e_x = jnp.exp2(s_x - m)
        den = jnp.sum(e_l, axis=-1, keepdims=True) + jnp.sum(e_x, axis=-1, keepdims=True) + jnp.exp2(sink - m)
        o = (_dot(e_l.astype(BF16), v_lat) + _dot(e_x.astype(BF16), vx_ref[:, ksl])) / den
        for j in range(A_GROUP // 2):
            even = o[(2 * j) * BLOCK:(2 * j + 1) * BLOCK]
            odd = o[(2 * j + 1) * BLOCK:(2 * j + 2) * BLOCK]
            o_ref[:, k * 2 * LANES + j * LANES:k * 2 * LANES + (j + 1) * LANES] = jnp.where(lo, even, odd).astype(BF16)


def _window_call(sink, qa, ka, va, *, batch, seq, ctx_len, need_ctx):
    n = qa.shape[0] if need_ctx else batch * seq
    nb = seq // BLOCK
    ncb = ctx_len // BLOCK
    lat_blocks = batch * nb
    nq = nb + (ncb if need_ctx else 0)

    def qi(b, i):
        return (jnp.where(i < nb, b * nb + i, lat_blocks + b * ncb + (i - nb)), 0)

    def ki(off):
        return lambda b, i: (b * nb + jnp.clip(i + off, 0, nb - 1), 0)

    def xi(b, i):
        return (batch * seq // ctx_len + b, 0)

    kv_specs = [pl.BlockSpec((BLOCK, 2 * LANES), ki(-1)), pl.BlockSpec((BLOCK, 2 * LANES), ki(0)),
                pl.BlockSpec((BLOCK, 2 * LANES), ki(1)), pl.BlockSpec((ctx_len, 2 * LANES), xi)]
    return pl.pallas_call(
        functools.partial(_window_kernel, nb=nb),
        grid=(batch, nq),
        in_specs=[pl.BlockSpec(memory_space=pltpu.SMEM), pl.BlockSpec((BLOCK, 3 * BLOCK), lambda b, i: (0, 0)),
                  pl.BlockSpec((BLOCK, 4 * LANES), qi)] + kv_specs + kv_specs,
        out_specs=pl.BlockSpec((BLOCK, 4 * LANES), qi),
        out_shape=jax.ShapeDtypeStruct((n, 4 * LANES), BF16),
        compiler_params=_params(("arbitrary", "arbitrary")),
        name="window",
    )(sink, _window_band(), qa, ka, ka, ka, ka, va, va, va, va)


def _mla_kernel(q_ref, *refs, with_lat):
    if with_lat:
        kl_ref, kx_ref, vl_ref, vx_ref, o_ref, kt_ref, vaug_ref = refs
    else:
        kx_ref, vx_ref, o_ref, kt_ref, vaug_ref = refs
    n_ctx = vx_ref.shape[0]
    lo = lax.broadcasted_iota(jnp.int32, o_ref.shape, 1) < B_V

    @pl.when(pl.program_id(2) == 0)
    def _():
        one = jnp.ones((1, LANES), BF16)
        for k_ref, v_ref, start in ((kx_ref, vx_ref, 0),) + (((kl_ref, vl_ref, n_ctx),) if with_lat else ()):
            n = v_ref.shape[0]
            v = v_ref[...]
            keep = lax.broadcasted_iota(jnp.int32, v.shape, 1) < B_V
            vaug_ref[0, start:start + n, :] = jnp.where(keep, v, one)
            vaug_ref[1, start:start + n, :] = jnp.where(keep, one, v)
            for hh in range(2):
                kt_ref[hh, :, start:start + n] = k_ref[:, hh * LANES:(hh + 1) * LANES].astype(F32).T.astype(BF16)

    def head(hh):
        s = _dot(q_ref[:, hh * LANES:(hh + 1) * LANES], kt_ref[hh])
        e = jnp.exp2(s - jnp.max(s, axis=-1, keepdims=True)).astype(BF16)
        o = _dot(e, vaug_ref[hh])
        return o / pltpu.roll(o, B_V, 1)

    o_ref[...] = jnp.where(lo, head(0), head(1)).astype(BF16)


def _mla_call(qb, kb, vb, *, batch, seq, ctx_len, latent):
    ctx0 = batch * seq // ctx_len
    pairs = B_HEADS // 2
    tq = TQ if latent else ctx_len
    nq = seq // TQ if latent else 1
    kx_spec = pl.BlockSpec((ctx_len, 2 * LANES), lambda b, p, j: (ctx0 + b, p))
    vx_spec = pl.BlockSpec((ctx_len, LANES), lambda b, p, j: (ctx0 + b, p))
    if latent:
        in_specs = [pl.BlockSpec((tq, 2 * LANES), lambda b, p, j: (b * nq + j, p)),
                    pl.BlockSpec((seq, 2 * LANES), lambda b, p, j: (b, p)), kx_spec,
                    pl.BlockSpec((seq, LANES), lambda b, p, j: (b, p)), vx_spec]
        args = (qb, kb, kb, vb, vb)
    else:
        in_specs = [pl.BlockSpec((tq, 2 * LANES), lambda b, p, j: (ctx0 + b, p)), kx_spec, vx_spec]
        args = (qb, kb, vb)
    return pl.pallas_call(
        functools.partial(_mla_kernel, with_lat=latent),
        grid=(batch, pairs, nq),
        in_specs=in_specs,
        out_specs=pl.BlockSpec((tq, LANES), lambda b, p, j: (b * nq + j, p)),
        out_shape=jax.ShapeDtypeStruct((batch * nq * tq, pairs * LANES), BF16),
        scratch_shapes=[pltpu.VMEM((2, LANES, ctx_len + (seq if latent else 0)), BF16),
                        pltpu.VMEM((2, ctx_len + (seq if latent else 0), LANES), BF16)],
        compiler_params=_params(("arbitrary", "arbitrary", "arbitrary")),
        name="mla" if latent else "mla_ctx",
    )(*args)


def _top2(v):
    i1 = jnp.zeros_like(v[0])
    m1 = v[0]
    for i in range(1, EPG):
        u = v[i] > m1
        i1 = jnp.where(u, float(i), i1)
        m1 = jnp.where(u, v[i], m1)
    i2 = jnp.zeros_like(v[0])
    m2 = jnp.full_like(v[0], -jnp.inf)
    for i in range(EPG):
        cand = jnp.where(i1 == float(i), -jnp.inf, v[i])
        u = cand > m2
        i2 = jnp.where(u, float(i), i2)
        m2 = jnp.where(u, cand, m2)
    return i1, i2, m1, m2


def _pick(idx, vals):
    out = vals[0]
    for i in range(1, len(vals)):
        out = jnp.where(idx == float(i), vals[i], out)
    return out


def _merge_kernel(x_ref, ya_ref, yb_ref, ga_ref, gb_ref, mod_ref, nw_ref, woa_ref, wob_ref, wout_ref, wr2_ref, rb_ref,
                  xo_ref, h_ref, meta_ref):
    a = _dot(ya_ref[...], woa_ref[...])
    b = _dot(yb_ref[...], wob_ref[...])
    mix = (ga_ref[...].astype(F32) * a + gb_ref[...].astype(F32) * b).astype(BF16)
    x = x_ref[...] + mod_ref[0, 2:3, :] * _dot(mix, wout_ref[...])
    xo_ref[...] = x
    h = _rms(x, nw_ref[...]) * (1.0 + mod_ref[0, 4:5, :]) + mod_ref[0, 3:4, :]
    _store_token_rows(h_ref, h)

    h_hi = h.astype(BF16)
    h_lo = (h - h_hi.astype(F32)).astype(BF16)
    l_hi = _dot(h_hi, wr2_ref[...])
    logits = l_hi[:, :LANES] + l_hi[:, LANES:] + _dot(h_lo, wr2_ref[:, :LANES])
    sc = jax.nn.sigmoid(logits.T[:N_EXPERTS, :])
    sel = sc + rb_ref[...]
    sel_rows = [sel[e:e + 1, :] for e in range(N_EXPERTS)]
    sc_rows = [sc[e:e + 1, :] for e in range(N_EXPERTS)]
    best = jnp.zeros_like(sel_rows[0])
    best_v = None
    for g in range(N_GROUPS):
        _, _, m1, m2 = _top2(sel_rows[g * EPG:(g + 1) * EPG])
        gv = m1 + m2
        if best_v is None:
            best_v = gv
        else:
            u = gv > best_v
            best = jnp.where(u, float(g), best)
            best_v = jnp.where(u, gv, best_v)
    sel_g = [_pick(best, [sel_rows[g * EPG + i] for g in range(N_GROUPS)]) for i in range(EPG)]
    sc_g = [_pick(best, [sc_rows[g * EPG + i] for g in range(N_GROUPS)]) for i in range(EPG)]
    i1, i2, _, _ = _top2(sel_g)
    s1 = _pick(i1, sc_g)
    s2 = _pick(i2, sc_g)
    tot = s1 + s2
    first_low = i1 < i2
    e_lo = jnp.where(first_low, i1, i2)
    e_hi = jnp.where(first_low, i2, i1)
    w_lo = jnp.where(first_low, s1, s2) / tot
    w_hi = jnp.where(first_low, s2, s1) / tot
    pid = jnp.where(e_lo == 0.0, e_hi - 1.0, jnp.where(e_lo == 1.0, e_hi + 1.0, 5.0))
    bucket = best * float(len(PAIRS)) + pid
    t = sel.shape[1]
    meta_ref[0] = jnp.concatenate([w_lo, w_hi, bucket, jnp.zeros((SUB - 3, t), F32)], axis=0)


def _merge_call(x, ya, yb, ga, gb, mod, nw, w, wr2, rb, *, n_rows, n_lat, seq):
    d = x.shape[1]
    tm = TM_MERGE
    nt = n_rows // tm
    n_lat_t = n_lat // tm
    tps = seq // tm
    ctx_row = n_lat // seq

    def tok(i):
        return (i, 0)

    def full(a):
        return pl.BlockSpec(a.shape, lambda i: (0,) * a.ndim)

    return pl.pallas_call(
        _merge_kernel,
        grid=(nt,),
        in_specs=[pl.BlockSpec((tm, d), tok), pl.BlockSpec((tm, 512), tok), pl.BlockSpec((tm, 512), tok),
                  pl.BlockSpec((tm, d), tok), pl.BlockSpec((tm, d), tok),
                  pl.BlockSpec((1, 8, d), lambda i: (jnp.where(i < n_lat_t, i // tps, ctx_row), 0, 0)),
                  full(nw), full(w["woa"]), full(w["wob"]), full(w["wout"]), full(wr2), full(rb)],
        out_specs=[pl.BlockSpec((tm, d), tok), pl.BlockSpec((tm * SUB, LANES), tok),
                   pl.BlockSpec((1, SUB, tm), lambda i: (i, 0, 0))],
        out_shape=[jax.ShapeDtypeStruct((n_rows, d), F32), jax.ShapeDtypeStruct((n_rows * SUB, LANES), F32),
                   jax.ShapeDtypeStruct((nt, SUB, tm), F32)],
        compiler_params=_params(("arbitrary",)),
        name="merge",
    )(x, ya, yb, ga, gb, mod, nw, w["woa"], w["wob"], w["wout"], wr2, rb)


def _moe_kernel(tok_ref, e1_ref, e2_ref, nv_ref, wts_ref, h_hbm, wg1_ref, wu1_ref, wd1_ref, wg2_ref, wu2_ref, wd2_ref,
                f_hbm, hbuf, obuf, gsem, ssem, *, n_tok, n_tiles):
    t = pl.program_id(0)
    slot = t % MOE_BUFS
    slot1 = (t + 1) % MOE_BUFS
    slot2 = (t + 2) % MOE_BUFS
    live = nv_ref[t] > 0
    prev_live = jnp.logical_and(t >= 1, nv_ref[jnp.maximum(t - 1, 0)] > 0)

    def gather_row(tile, s, r):
        tk = jnp.minimum(tok_ref[tile * TE + r], n_tok - 1)
        return pltpu.make_async_copy(h_hbm.at[pl.ds(tk * SUB, SUB)], hbuf.at[s, pl.ds(r * SUB, SUB)], gsem.at[s])

    def scatter_row(tile, s, r):
        tk = tok_ref[tile * TE + r]
        return pltpu.make_async_copy(obuf.at[s, pl.ds(r * SUB, SUB)], f_hbm.at[pl.ds(tk * SUB, SUB)], ssem.at[s])

    def wait_gather(s):
        pltpu.make_async_copy(h_hbm.at[pl.ds(0, TE * SUB)], hbuf.at[s], gsem.at[s]).wait()

    def wait_scatter(s):
        pltpu.make_async_copy(obuf.at[s], f_hbm.at[pl.ds(0, TE * SUB)], ssem.at[s]).wait()

    def start_rows(make_row, tile, s):
        def body(i, carry):
            make_row(tile, s, 2 * i).start(priority=0)
            make_row(tile, s, 2 * i + 1).start(priority=1)
            return carry
        lax.fori_loop(0, TE // 2, body, 0, unroll=4)

    @pl.when(t == 0)
    def _():
        obuf[...] = jnp.zeros(obuf.shape, F32)
        for half in range(2):
            fill = pltpu.make_async_copy(obuf.at[0], f_hbm.at[pl.ds((n_tok + half * TE) * SUB, TE * SUB)], ssem.at[0])
            fill.start()
            fill.wait()
        start_rows(gather_row, 0, 0)
        start_rows(gather_row, 1, 1)

    @pl.when(live)
    def _():
        wait_gather(slot)

        @pl.when(t >= 2)
        def _():
            wait_scatter(slot)

    @pl.when(live)
    def _():
        g_tile = jnp.minimum(t + 2, n_tiles - 1)
        s_tile = jnp.where(t == 0, n_tiles - 1, t - 1)
        issue = ([functools.partial(gather_row, g_tile, slot2, r) for r in range(TE)]
                 + [functools.partial(scatter_row, s_tile, slot2, r) for r in range(TE)])
        n_stage = 6
        per_stage = -(-len(issue) // n_stage)

        def issue_stage(k):
            for i, make in enumerate(issue[k * per_stage:(k + 1) * per_stage]):
                make().start(priority=i % 2)

        h = _load_token_rows(hbuf.at[slot], TE).astype(BF16)
        w_lo = jnp.broadcast_to(wts_ref[0, 0:1, :], (LANES, TE)).T[:, :1]
        w_hi = jnp.broadcast_to(wts_ref[0, 1:2, :], (LANES, TE)).T[:, :1]
        outs = []
        stage = 0
        for wg_ref, wu_ref, wd_ref, wt in ((wg1_ref, wu1_ref, wd1_ref, w_lo), (wg2_ref, wu2_ref, wd2_ref, w_hi)):
            hg = _dot(h, wg_ref[0, 0])
            issue_stage(stage)
            hu = _dot(h, wu_ref[0, 0])
            issue_stage(stage + 1)
            act = (hg * jax.nn.sigmoid(hg) * hu).astype(BF16)
            outs.append(wt * _dot(act, wd_ref[0, 0]))
            issue_stage(stage + 2)
            stage += 3
        _store_token_rows(obuf.at[slot], outs[0] + outs[1])

    @pl.when(jnp.logical_and(jnp.logical_not(live), prev_live))
    def _():
        wait_gather(slot)
        wait_gather(slot1)
        wait_scatter(slot1)

        @pl.when(t >= 2)
        def _():
            wait_scatter(slot)
        start_rows(scatter_row, t - 1, slot2)
        wait_scatter(slot2)


def _moe_call(tok, e1, e2, nv, wts, h3, wg, wu, wd, *, layer, n_tok):
    d = wg.shape[2]
    de = wg.shape[3]
    n_tiles = e1.shape[0]

    def wspec(shape, which):
        return pl.BlockSpec((1, 1) + shape, lambda t, tok, e1, e2, nv: (layer, (e1, e2)[which][t], 0, 0))

    grid_spec = pltpu.PrefetchScalarGridSpec(
        num_scalar_prefetch=4,
        grid=(n_tiles,),
        in_specs=[pl.BlockSpec((1, 2, TE), lambda t, tok, e1, e2, nv: (t, 0, 0)),
                  pl.BlockSpec(memory_space=pl.ANY),
                  wspec((d, de), 0), wspec((d, de), 0), wspec((de, d), 0),
                  wspec((d, de), 1), wspec((d, de), 1), wspec((de, d), 1)],
        out_specs=pl.BlockSpec(memory_space=pl.ANY),
        scratch_shapes=[pltpu.VMEM((MOE_BUFS, TE * SUB, LANES), F32), pltpu.VMEM((MOE_BUFS, TE * SUB, LANES), F32),
                        pltpu.SemaphoreType.DMA((MOE_BUFS,)), pltpu.SemaphoreType.DMA((MOE_BUFS,))],
    )
    return pl.pallas_call(
        functools.partial(_moe_kernel, n_tok=n_tok, n_tiles=n_tiles),
        grid_spec=grid_spec,
        out_shape=jax.ShapeDtypeStruct(((n_tok + 2 * TE) * SUB, LANES), F32),
        compiler_params=_params(("arbitrary",)),
        name="moe",
    )(tok, e1, e2, nv, wts, h3, wg, wu, wd, wg, wu, wd)


def _route_tables(meta, n_tok):
    w_lo = meta[:, 0, :].reshape(-1)
    w_hi = meta[:, 1, :].reshape(-1)
    bucket = meta[:, 2, :].reshape(-1).astype(jnp.int32)
    n_slots = n_tok + N_BUCKETS * TE
    n_tiles = n_slots // TE
    buckets = jnp.arange(N_BUCKETS, dtype=jnp.int32)
    counts = jnp.sum(bucket[:, None] == buckets[None, :], axis=0, dtype=jnp.int32)
    padded = ((counts + TE - 1) // TE) * TE
    pad_end = jnp.cumsum(padded)
    pad_start = pad_end - padded
    fill_i = jnp.arange(TE, dtype=jnp.int32)[None, :]
    fill_key = jnp.where(fill_i < (padded - counts)[:, None], 2 * buckets[:, None] + 1, 2 * N_BUCKETS).reshape(-1)
    keys = jnp.concatenate([2 * bucket, fill_key])
    ids = jnp.concatenate([jnp.arange(n_tok, dtype=jnp.int32), jnp.full((N_BUCKETS * TE,), -1, jnp.int32)])
    zeros = jnp.zeros((N_BUCKETS * TE,), F32)
    _, ids, w_lo, w_hi = lax.sort((keys, ids, jnp.concatenate([w_lo, zeros]), jnp.concatenate([w_hi, zeros])),
                                  num_keys=1, is_stable=True)
    slot = jnp.arange(n_slots, dtype=jnp.int32)
    tok = jnp.where(ids < 0, n_tok + ((slot // TE) % 2) * TE + slot % TE, ids)
    wts = jnp.stack([w_lo.reshape(n_tiles, TE), w_hi.reshape(n_tiles, TE)], axis=1)

    tile_start = jnp.arange(n_tiles, dtype=jnp.int32) * TE
    tb = jnp.sum(tile_start[:, None] >= pad_end[None, :], axis=1, dtype=jnp.int32)
    used = tb < N_BUCKETS
    onehot = (tb[:, None] == buckets[None, :]).astype(jnp.int32)
    nv = jnp.clip(jnp.sum(onehot * (counts + pad_start)[None, :], axis=1) - tile_start, 0, TE) * used
    tbe = jnp.where(used, tb, jnp.max(jnp.where(used, tb, 0)))
    pid = tbe % len(PAIRS)
    pair_lo = (pid >= 3).astype(jnp.int32) + (pid >= 5).astype(jnp.int32)
    pair_hi = pid + 1 - 2 * (pid >= 3).astype(jnp.int32) - (pid >= 5).astype(jnp.int32)
    e1 = (tbe // len(PAIRS)) * EPG + pair_lo
    e2 = (tbe // len(PAIRS)) * EPG + pair_hi
    return tok.astype(jnp.int32), e1.astype(jnp.int32), e2.astype(jnp.int32), nv.astype(jnp.int32), wts


def _final_kernel(x_ref, f_ref, mod_ref, nw_ref, o_ref):
    x = x_ref[...] + mod_ref[0, 5:6, :] * _load_token_rows(f_ref, x_ref.shape[0])
    o_ref[...] = _rms(x, nw_ref[...])


def _final_call(x, f, mod, nw, *, n_lat, seq):
    d = x.shape[1]
    tm = TM_MERGE
    tps = seq // tm
    return pl.pallas_call(
        _final_kernel,
        grid=(n_lat // tm,),
        in_specs=[pl.BlockSpec((tm, d), lambda i: (i, 0)), pl.BlockSpec((tm * SUB, LANES), lambda i: (i, 0)),
                  pl.BlockSpec((1, 8, d), lambda i: (i // tps, 0, 0)), pl.BlockSpec((1, d), lambda i: (0, 0))],
        out_specs=pl.BlockSpec((tm, d), lambda i: (i, 0)),
        out_shape=jax.ShapeDtypeStruct((n_lat, d), F32),
        compiler_params=_params(("arbitrary",)),
        name="final",
    )(x, f, mod, nw)


def _rope_table(pos_r, pos_c, dim, lane_of):
    d = dim // 2
    half = d // 2
    lane = np.arange(LANES)
    rl = lane_of(lane)
    is_rope = rl >= 0
    rl = np.maximum(rl, 0)
    use_col = rl >= d
    j = rl % half
    first = (rl % d) < half
    inv = (np.float32(ROPE_BASE) ** (-(2.0 * j).astype(np.float32) / np.float32(d))).astype(np.float32)
    pos = np.where(use_col[None, :], pos_c[:, None], pos_r[:, None]).astype(np.float32)
    ang = pos * inv[None, :]
    cos = np.where(is_rope[None, :], np.cos(ang), 1.0)
    sin = np.where(is_rope[None, :], np.sin(ang), 0.0)
    return np.stack([cos, np.where(first[None, :], -sin, 0.0), np.where(first[None, :], 0.0, sin)]).astype(np.float32)


def _tables(seq):
    t = np.arange(seq)
    rows, cols = t // GRID_W, t % GRID_W
    tab_a = _rope_table(rows, cols, A_HEAD_DIM, lambda lane: lane % A_HEAD_DIM)
    tab_b = _rope_table(rows, cols, B_ROPE,
                        lambda lane: np.where((lane >= B_NOPE) & (lane < B_NOPE + B_ROPE), lane - B_NOPE, -1))
    ident = np.stack([np.ones((TM, LANES), np.float32), np.zeros((TM, LANES), np.float32),
                      np.zeros((TM, LANES), np.float32)])
    return jnp.asarray(np.concatenate([tab_a, ident], axis=1)), jnp.asarray(np.concatenate([tab_b, ident], axis=1))


def _window_band():
    r = np.arange(BLOCK)[:, None]
    c = np.arange(3 * BLOCK)[None, :]
    dist = c - r
    return jnp.asarray(np.where((dist >= BLOCK - WINDOW) & (dist <= BLOCK + WINDOW), 0.0, NEG).astype(np.float32))


def _layer_weights(w_in, w_uq, w_ukv, q_norm, kv_norm, w_o_a, w_o_b, w_out):
    d = w_in.shape[0]
    o = 0
    qa_w = w_in[:, o:o + 512]; o += 512
    ka_w = w_in[:, o:o + 128]; o += 128
    va_w = w_in[:, o:o + 128]; o += 128
    cq_w = w_in[:, o:o + B_Q_RANK]; o += B_Q_RANK
    ckv_w = w_in[:, o:o + B_KV_RANK]; o += B_KV_RANK
    kr_w = w_in[:, o:o + B_ROPE]; o += B_ROPE
    g_w = w_in[:, o:]

    def dup(wk):
        wk = wk.reshape(d, A_KV_HEADS, 1, A_HEAD_DIM)
        return jnp.broadcast_to(wk, (d, A_KV_HEADS, 2, A_HEAD_DIM)).reshape(d, A_KV_HEADS * 2 * A_HEAD_DIM)

    w1 = jnp.concatenate([qa_w, dup(ka_w), dup(va_w), cq_w, ckv_w, jnp.pad(kr_w, ((0, 0), (0, LANES - B_ROPE)))], axis=1)
    wuq = jnp.pad(w_uq.reshape(B_Q_RANK, B_HEADS, B_NOPE + B_ROPE),
                  ((0, 0), (0, 0), (0, LANES - B_NOPE - B_ROPE))).reshape(B_Q_RANK, B_HEADS * LANES)
    ukv = w_ukv.reshape(B_KV_RANK, B_HEADS, B_NOPE + B_V)
    wukn = jnp.pad(ukv[:, :, :B_NOPE], ((0, 0), (0, 0), (0, LANES - B_NOPE))).reshape(B_KV_RANK, B_HEADS * LANES)
    wuv = ukv[:, :, B_NOPE:].reshape(B_KV_RANK, B_HEADS * B_V)
    src = jnp.arange(LANES)[:, None]
    dst = jnp.arange(B_HEADS * LANES)[None, :]
    rp = ((src < B_ROPE) & (dst % LANES == src + B_NOPE)).astype(BF16)
    return dict(w1=w1.astype(BF16), wg=g_w.astype(BF16), qn=q_norm.reshape(1, -1), wuq=wuq.astype(BF16),
                kvn=kv_norm.reshape(1, -1), wukn=wukn.astype(BF16), wuv=wuv.astype(BF16), rp=rp,
                woa=w_o_a.astype(BF16), wob=w_o_b.astype(BF16), wout=w_out.astype(BF16))


def kernel(x, c, ctx, c_ctx, w_mod, b_mod, norm_mix, norm_ffn, w_in, attn_sink, mla_q_norm, w_uq, mla_kv_norm, w_ukv,
           w_o_a, w_o_b, w_out, w_router, router_bias, w_expert_gate, w_expert_up, w_expert_down, final_norm):
    batch, seq, d = x.shape
    ctx_len = ctx.shape[1]
    depth = w_mod.shape[0]
    n_lat = batch * seq
    n_all = n_lat + batch * ctx_len
    for tile in (TM, TM_MERGE):
        assert seq % tile == 0 and (batch * ctx_len) % tile == 0
    assert d == SUB * LANES and seq % TQ == 0 and seq % GRID_W == 0 and batch + 1 <= MOD_ROWS

    c_all = jnp.concatenate([c, c_ctx[None, :], jnp.zeros((MOD_ROWS - batch - 1, d), F32)], axis=0)
    mod = _mod_call(c_all, w_mod, b_mod)
    mod = jnp.pad(mod.transpose(0, 2, 1, 3), ((0, 0), (0, 0), (0, 2), (0, 0)))

    tab_a, tab_b = _tables(seq)
    wr = jnp.pad(w_router.astype(F32), ((0, 0), (0, LANES - N_EXPERTS)))
    wr_hi = wr.astype(BF16)
    wr2 = jnp.concatenate([wr_hi, (wr - wr_hi.astype(F32)).astype(BF16)], axis=1)
    rb = router_bias.astype(F32).reshape(N_EXPERTS, 1)
    wg_all = w_expert_gate.astype(BF16)
    wu_all = w_expert_up.astype(BF16)
    wd_all = w_expert_down.astype(BF16)

    xs = jnp.concatenate([x.reshape(n_lat, d), ctx.reshape(batch * ctx_len, d)], axis=0)
    f = None
    for l in range(depth):
        need_ctx = l < depth - 1
        w = _layer_weights(w_in[l], w_uq[l], w_ukv[l], mla_q_norm[l], mla_kv_norm[l], w_o_a[l], w_o_b[l], w_out[l])
        outs = _proj_call(xs, f, mod[l - 1] if l else None, mod[l], norm_mix[l].reshape(1, d), tab_a, tab_b, w,
                          n_lat=n_lat, seq=seq)
        if l:
            xs, outs = outs[0], outs[1:]
        qa, ka, va, qb, kb, vb, ga, gb = outs
        ya = _window_call(attn_sink[l].astype(F32), qa, ka, va, batch=batch, seq=seq, ctx_len=ctx_len, need_ctx=need_ctx)
        yb = _mla_call(qb, kb, vb, batch=batch, seq=seq, ctx_len=ctx_len, latent=True)
        if need_ctx:
            yb = jnp.concatenate([yb, _mla_call(qb, kb, vb, batch=batch, seq=seq, ctx_len=ctx_len, latent=False)], axis=0)
        n_rows = n_all if need_ctx else n_lat
        xs, h3, meta = _merge_call(xs, ya, yb, ga, gb, mod[l], norm_ffn[l].reshape(1, d), w, wr2, rb,
                                   n_rows=n_rows, n_lat=n_lat, seq=seq)
        tok, e1, e2, nv, wts = _route_tables(meta, n_rows)
        f = _moe_call(tok, e1, e2, nv, wts, h3, wg_all, wu_all, wd_all, layer=l, n_tok=n_rows)
    out = _final_call(xs, f, mod[depth - 1], final_norm.reshape(1, d), n_lat=n_lat, seq=seq)
    return out.reshape(batch, seq, d)
```

```python
import functools

import numpy as np
import jax
import jax.numpy as jnp
from jax import lax
from jax.experimental import pallas as pl
from jax.experimental.pallas import tpu as pltpu

F32 = jnp.float32
BF16 = jnp.bfloat16

EPS = 1e-6
ROPE_BASE = 10000.0
GRID_W = 64
BLOCK = 128
A_HEADS, A_KV_HEADS, A_HEAD_DIM = 8, 2, 64
A_GROUP = A_HEADS // A_KV_HEADS
WINDOW = 128
B_HEADS, B_NOPE, B_ROPE, B_V = 8, 64, 32, 64
B_Q_RANK, B_KV_RANK = 256, 128
N_EXPERTS, N_GROUPS = 16, 4
EPG = N_EXPERTS // N_GROUPS
D_EXPERT = 512
PAIRS = ((0, 1), (0, 2), (0, 3), (1, 2), (1, 3), (2, 3))
N_BUCKETS = N_GROUPS * len(PAIRS)

LANES = 128
SUB = 8
TM = 512
TM_MERGE = 512
TQ = 512
MLA_PAIRS = 2
TE = 256
MOE_BUFS = 3
MOD_ROWS = 24
NEG = -1e30
LOG2E = 1.4426950408889634
VMEM_LIMIT = 56 * 1024 * 1024

W1_QA, W1_KA, W1_VA, W1_CQ, W1_CKV, W1_KR = 0, 512, 768, 1024, 1280, 1408
W1_COLS = 1536


def _params(sem, vmem=VMEM_LIMIT):
    return pltpu.CompilerParams(dimension_semantics=sem, vmem_limit_bytes=vmem)


def _dot(a, b):
    return jnp.dot(a, b, preferred_element_type=F32)


def _dot_nt(a, b):
    return lax.dot_general(a, b, (((1,), (1,)), ((), ())), preferred_element_type=F32)


def _rms(x, g):
    return x * lax.rsqrt(jnp.mean(x * x, axis=-1, keepdims=True) + EPS) * g


def _lane_tile(t, width):
    return jnp.concatenate([t] * (width // LANES), axis=1)


def _rope(t, tab_ref, shift):
    w = t.shape[-1]
    up = pltpu.roll(t, w - shift, 1)
    dn = pltpu.roll(t, shift, 1)
    return (t * _lane_tile(tab_ref[0], w) + up * _lane_tile(tab_ref[1], w)
            + dn * _lane_tile(tab_ref[2], w))


def _load_token_rows(ref, rows):
    return jnp.concatenate([ref[pl.ds(s, rows, stride=SUB), :] for s in range(SUB)], axis=1)


def _store_token_rows(ref, val):
    rows = val.shape[0]
    for s in range(SUB):
        ref[pl.ds(s, rows, stride=SUB), :] = val[:, s * LANES:(s + 1) * LANES]


def _mod_kernel(c_ref, w_ref, b_ref, o_ref):
    c = c_ref[...]
    a = (c * jax.nn.sigmoid(c)).astype(BF16)
    o_ref[0, 0] = _dot(a, w_ref[0].astype(BF16)) + b_ref[0, 0]


def _mod_call(c_all, w_mod, b_mod):
    depth, d, _ = w_mod.shape
    return pl.pallas_call(
        _mod_kernel,
        grid=(depth, 6),
        in_specs=[pl.BlockSpec((MOD_ROWS, d), lambda l, k: (0, 0)),
                  pl.BlockSpec((1, d, d), lambda l, k: (l, 0, k)),
                  pl.BlockSpec((1, 1, 1, d), lambda l, k: (l, k, 0, 0))],
        out_specs=pl.BlockSpec((1, 1, MOD_ROWS, d), lambda l, k: (l, k, 0, 0)),
        out_shape=jax.ShapeDtypeStruct((depth, 6, MOD_ROWS, d), F32),
        compiler_params=_params(("arbitrary", "arbitrary")),
        name="mod",
    )(c_all, w_mod, b_mod.reshape(depth, 6, 1, d))


def _proj_kernel(*refs, has_f):
    if has_f:
        x_ref, f_ref, modp_ref, refs = refs[0], refs[1], refs[2], refs[3:]
    else:
        x_ref, refs = refs[0], refs[1:]
    (mod_ref, nw_ref, ta_ref, tb_ref, w1_ref, wg_ref, qn_ref, wuq_ref, kvn_ref, wukn_ref, wuv_ref,
     rp_ref), refs = refs[:12], refs[12:]
    if has_f:
        xo_ref, refs = refs[0], refs[1:]
    qa_ref, ka_ref, va_ref, qb_ref, kb_ref, vb_ref, ga_ref, gb_ref = refs

    x = x_ref[...]
    if has_f:
        x = x + modp_ref[0, 5:6, :] * _load_token_rows(f_ref, x.shape[0])
        xo_ref[...] = x
    h = (_rms(x, nw_ref[...]) * (1.0 + mod_ref[0, 1:2, :]) + mod_ref[0, 0:1, :]).astype(BF16)

    t = _dot(h, w1_ref[...])
    qa_ref[...] = (_rope(t[:, W1_QA:W1_KA], ta_ref, 16) * (A_HEAD_DIM ** -0.5 * LOG2E)).astype(BF16)
    ka_ref[...] = _rope(t[:, W1_KA:W1_VA], ta_ref, 16).astype(BF16)
    va_ref[...] = t[:, W1_VA:W1_CQ].astype(BF16)

    cq = _rms(t[:, W1_CQ:W1_CKV], qn_ref[...]).astype(BF16)
    qb = _rope(_dot(cq, wuq_ref[...]), tb_ref, 8)
    qb_ref[...] = (qb * ((B_NOPE + B_ROPE) ** -0.5 * LOG2E)).astype(BF16)

    ckv = _rms(t[:, W1_CKV:W1_KR], kvn_ref[...]).astype(BF16)
    vb_ref[...] = _dot(ckv, wuv_ref[...]).astype(BF16)
    kr = t[:, W1_KR:W1_COLS]
    kr_hi = kr.astype(BF16)
    kr_lo = (kr - kr_hi.astype(F32)).astype(BF16)
    kb = _dot(ckv, wukn_ref[...]) + _dot(kr_hi, rp_ref[...]) + _dot(kr_lo, rp_ref[...])
    kb_ref[...] = _rope(kb, tb_ref, 8).astype(BF16)

    d = ga_ref.shape[-1]
    ga_ref[...] = jax.nn.sigmoid(_dot(h, wg_ref[:, :d])).astype(BF16)
    gb_ref[...] = jax.nn.sigmoid(_dot(h, wg_ref[:, d:])).astype(BF16)


def _proj_call(x, f, modp, mod, nw, tab_a, tab_b, w, *, n_lat, seq):
    n, d = x.shape
    nt = n // TM
    n_lat_t = n_lat // TM
    tps = seq // TM
    ctx_row = n_lat // seq
    has_f = f is not None

    def tok(i):
        return (i, 0)

    def modi(i):
        return (jnp.where(i < n_lat_t, i // tps, ctx_row), 0, 0)

    def tabi(i):
        return (0, jnp.where(i < n_lat_t, i % tps, tps), 0)

    def full(a):
        return pl.BlockSpec(a.shape, lambda i: (0,) * a.ndim, pipeline_mode=pl.Buffered(1))

    mod_spec = pl.BlockSpec((1, 8, d), modi)
    in_specs = [pl.BlockSpec((TM, d), tok)]
    args = [x]
    if has_f:
        in_specs += [pl.BlockSpec((TM * SUB, LANES), tok), mod_spec]
        args += [f, modp]
    in_specs += [mod_spec, full(nw), pl.BlockSpec((3, TM, LANES), tabi), pl.BlockSpec((3, TM, LANES), tabi)]
    args += [mod, nw, tab_a, tab_b]
    for k in ("w1", "wg", "qn", "wuq", "kvn", "wukn", "wuv", "rp"):
        in_specs.append(full(w[k]))
        args.append(w[k])

    widths = (512, 256, 256, 1024, 1024, 512, d, d)
    out_shape = [jax.ShapeDtypeStruct((n, wd), BF16) for wd in widths]
    out_specs = [pl.BlockSpec((TM, wd), tok) for wd in widths]
    if has_f:
        out_shape.insert(0, jax.ShapeDtypeStruct((n, d), F32))
        out_specs.insert(0, pl.BlockSpec((TM, d), tok))
    return pl.pallas_call(
        functools.partial(_proj_kernel, has_f=has_f),
        grid=(nt,), in_specs=in_specs, out_specs=out_specs, out_shape=out_shape,
        compiler_params=_params(("arbitrary",)),
        name="proj",
    )(*args)


def _window_kernel(sink_ref, band_ref, q_ref, kp_ref, kc_ref, kn_ref, kx_ref, vp_ref, vc_ref, vn_ref, vx_ref, o_ref,
                   *, nb):
    n = pl.program_id(1)
    rows = A_GROUP * BLOCK
    lo = lax.broadcasted_iota(jnp.int32, (BLOCK, LANES), 1) < A_HEAD_DIM
    col_blk = lax.broadcasted_iota(jnp.int32, (1, 3 * BLOCK), 1) // BLOCK
    off_prev = jnp.where(jnp.logical_or(n == 0, n >= nb), NEG, 0.0)
    off_cur = jnp.where(n >= nb, NEG, 0.0)
    off_next = jnp.where(n >= nb - 1, NEG, 0.0)
    bias = band_ref[...] + jnp.where(col_blk == 0, off_prev, jnp.where(col_blk == 1, off_cur, off_next))
    bias = jnp.concatenate([bias] * A_GROUP, axis=0)
    row_head = lax.broadcasted_iota(jnp.int32, (rows, 1), 0) // BLOCK
    q = q_ref[...]
    zero = jnp.zeros((BLOCK, LANES), BF16)
    for k in range(A_KV_HEADS):
        ksl = slice(k * LANES, (k + 1) * LANES)
        parts = []
        for j in range(A_GROUP // 2):
            blk = q[:, k * 2 * LANES + j * LANES:k * 2 * LANES + (j + 1) * LANES]
            parts += [jnp.where(lo, blk, zero), jnp.where(lo, zero, blk)]
        qs = jnp.concatenate(parts, axis=0)
        k_lat = jnp.concatenate([kp_ref[:, ksl], kc_ref[:, ksl], kn_ref[:, ksl]], axis=0)
        v_lat = jnp.concatenate([vp_ref[:, ksl], vc_ref[:, ksl], vn_ref[:, ksl]], axis=0)
        s_l = _dot_nt(qs, k_lat) + bias
        s_x = _dot_nt(qs, kx_ref[:, ksl])
        sink = jnp.zeros((rows, 1), F32)
        for g in range(A_GROUP):
            sink = jnp.where(row_head == g, sink_ref[k * A_GROUP + g] * LOG2E, sink)
        m = jnp.maximum(jnp.maximum(jnp.max(s_l, axis=-1, keepdims=True), jnp.max(s_x, axis=-1, keepdims=True)), sink)
        e_l = jnp.exp2(s_l - m)
        e_x = jnp.exp2(s_x - m)
        den = jnp.sum(e_l, axis=-1, keepdims=True) + jnp.sum(e_x, axis=-1, keepdims=True) + jnp.exp2(sink - m)
        o = (_dot(e_l.astype(BF16), v_lat) + _dot(e_x.astype(BF16), vx_ref[:, ksl])) / den
        for j in range(A_GROUP // 2):
            even = o[(2 * j) * BLOCK:(2 * j + 1) * BLOCK]
            odd = o[(2 * j + 1) * BLOCK:(2 * j + 2) * BLOCK]
            o_ref[:, k * 2 * LANES + j * LANES:k * 2 * LANES + (j + 1) * LANES] = jnp.where(lo, even, odd).astype(BF16)


def _window_call(sink, qa, ka, va, *, batch, seq, ctx_len, need_ctx):
    n = qa.shape[0] if need_ctx else batch * seq
    nb = seq // BLOCK
    ncb = ctx_len // BLOCK
    lat_blocks = batch * nb
    nq = nb + (ncb if need_ctx else 0)

    def qi(b, i):
        return (jnp.where(i < nb, b * nb + i, lat_blocks + b * ncb + (i - nb)), 0)

    def ki(off):
        return lambda b, i: (b * nb + jnp.clip(i + off, 0, nb - 1), 0)

    def xi(b, i):
        return (batch * seq // ctx_len + b, 0)

    kv_specs = [pl.BlockSpec((BLOCK, 2 * LANES), ki(-1)), pl.BlockSpec((BLOCK, 2 * LANES), ki(0)),
                pl.BlockSpec((BLOCK, 2 * LANES), ki(1)), pl.BlockSpec((ctx_len, 2 * LANES), xi)]
    return pl.pallas_call(
        functools.partial(_window_kernel, nb=nb),
        grid=(batch, nq),
        in_specs=[pl.BlockSpec(memory_space=pltpu.SMEM), pl.BlockSpec((BLOCK, 3 * BLOCK), lambda b, i: (0, 0)),
                  pl.BlockSpec((BLOCK, 4 * LANES), qi)] + kv_specs + kv_specs,
        out_specs=pl.BlockSpec((BLOCK, 4 * LANES), qi),
        out_shape=jax.ShapeDtypeStruct((n, 4 * LANES), BF16),
        compiler_params=_params(("arbitrary", "arbitrary")),
        name="window",
    )(sink, _window_band(), qa, ka, ka, ka, ka, va, va, va, va)


def _mla_kernel(q_ref, *refs, with_lat):
    if with_lat:
        kl_ref, kx_ref, vl_ref, vx_ref, o_ref, vaug_ref = refs
    else:
        kx_ref, vx_ref, o_ref, vaug_ref = refs
    n_ctx = vx_ref.shape[0]
    n_pairs = o_ref.shape[1] // LANES
    lo = lax.broadcasted_iota(jnp.int32, (o_ref.shape[0], LANES), 1) < B_V

    @pl.when(pl.program_id(2) == 0)
    def _():
        one = jnp.ones((1, LANES), BF16)
        for v_ref, start in ((vx_ref, 0),) + (((vl_ref, n_ctx),) if with_lat else ()):
            n = v_ref.shape[0]
            keep = lax.broadcasted_iota(jnp.int32, (n, LANES), 1) < B_V
            for pp in range(n_pairs):
                v = v_ref[:, pp * LANES:(pp + 1) * LANES]
                vaug_ref[2 * pp, start:start + n, :] = jnp.where(keep, v, one)
                vaug_ref[2 * pp + 1, start:start + n, :] = jnp.where(keep, one, v)

    def head(h):
        sl = slice(h * LANES, (h + 1) * LANES)
        q = q_ref[:, sl]
        s_x = _dot_nt(q, kx_ref[:, sl])
        m = jnp.max(s_x, axis=-1, keepdims=True)
        if with_lat:
            s_l = _dot_nt(q, kl_ref[:, sl])
            m = jnp.maximum(m, jnp.max(s_l, axis=-1, keepdims=True))
        o = _dot(jnp.exp2(s_x - m).astype(BF16), vaug_ref[h, :n_ctx, :])
        if with_lat:
            o = o + _dot(jnp.exp2(s_l - m).astype(BF16), vaug_ref[h, n_ctx:, :])
        return o / pltpu.roll(o, B_V, 1)

    for pp in range(n_pairs):
        o_ref[:, pp * LANES:(pp + 1) * LANES] = jnp.where(lo, head(2 * pp), head(2 * pp + 1)).astype(BF16)


def _mla_call(qb, kb, vb, *, batch, seq, ctx_len, latent):
    ctx0 = batch * seq // ctx_len
    groups = B_HEADS // (2 * MLA_PAIRS)
    qk_w = MLA_PAIRS * 2 * LANES
    v_w = MLA_PAIRS * LANES
    tq = TQ if latent else ctx_len
    nq = seq // TQ if latent else 1
    kx_spec = pl.BlockSpec((ctx_len, qk_w), lambda b, p, j: (ctx0 + b, p))
    vx_spec = pl.BlockSpec((ctx_len, v_w), lambda b, p, j: (ctx0 + b, p))
    if latent:
        in_specs = [pl.BlockSpec((tq, qk_w), lambda b, p, j: (b * nq + j, p)),
                    pl.BlockSpec((seq, qk_w), lambda b, p, j: (b, p)), kx_spec,
                    pl.BlockSpec((seq, v_w), lambda b, p, j: (b, p)), vx_spec]
        args = (qb, kb, kb, vb, vb)
    else:
        in_specs = [pl.BlockSpec((tq, qk_w), lambda b, p, j: (ctx0 + b, p)), kx_spec, vx_spec]
        args = (qb, kb, vb)
    return pl.pallas_call(
        functools.partial(_mla_kernel, with_lat=latent),
        grid=(batch, groups, nq),
        in_specs=in_specs,
        out_specs=pl.BlockSpec((tq, v_w), lambda b, p, j: (b * nq + j, p)),
        out_shape=jax.ShapeDtypeStruct((batch * nq * tq, groups * v_w), BF16),
        scratch_shapes=[pltpu.VMEM((2 * MLA_PAIRS, ctx_len + (seq if latent else 0), LANES), BF16)],
        compiler_params=_params(("arbitrary", "arbitrary", "arbitrary")),
        name="mla" if latent else "mla_ctx",
    )(*args)


def _top2(v):
    i1 = jnp.zeros_like(v[0])
    m1 = v[0]
    for i in range(1, EPG):
        u = v[i] > m1
        i1 = jnp.where(u, float(i), i1)
        m1 = jnp.where(u, v[i], m1)
    i2 = jnp.zeros_like(v[0])
    m2 = jnp.full_like(v[0], -jnp.inf)
    for i in range(EPG):
        cand = jnp.where(i1 == float(i), -jnp.inf, v[i])
        u = cand > m2
        i2 = jnp.where(u, float(i), i2)
        m2 = jnp.where(u, cand, m2)
    return i1, i2, m1, m2


def _pick(idx, vals):
    out = vals[0]
    for i in range(1, len(vals)):
        out = jnp.where(idx == float(i), vals[i], out)
    return out


def _merge_kernel(x_ref, ya_ref, yb_ref, ybx_ref, ga_ref, gb_ref, mod_ref, nw_ref, woa_ref, wob_ref, wout_ref, wr2_ref,
                  rb_ref, xo_ref, h_ref, meta_ref, *, n_lat_tiles):
    yb = yb_ref[...] if ybx_ref is None else jnp.where(pl.program_id(0) < n_lat_tiles, yb_ref[...], ybx_ref[...])
    a = _dot(ya_ref[...], woa_ref[...])
    b = _dot(yb, wob_ref[...])
    mix = (ga_ref[...].astype(F32) * a + gb_ref[...].astype(F32) * b).astype(BF16)
    x = x_ref[...] + mod_ref[0, 2:3, :] * _dot(mix, wout_ref[...])
    xo_ref[...] = x
    h = _rms(x, nw_ref[...]) * (1.0 + mod_ref[0, 4:5, :]) + mod_ref[0, 3:4, :]
    _store_token_rows(h_ref, h)

    h_hi = h.astype(BF16)
    h_lo = (h - h_hi.astype(F32)).astype(BF16)
    l_hi = _dot(h_hi, wr2_ref[...])
    logits = l_hi[:, :LANES] + l_hi[:, LANES:] + _dot(h_lo, wr2_ref[:, :LANES])
    sc = jax.nn.sigmoid(logits.T[:N_EXPERTS, :])
    sel = sc + rb_ref[...]
    sel_rows = [sel[e:e + 1, :] for e in range(N_EXPERTS)]
    sc_rows = [sc[e:e + 1, :] for e in range(N_EXPERTS)]
    best = jnp.zeros_like(sel_rows[0])
    best_v = None
    for g in range(N_GROUPS):
        _, _, m1, m2 = _top2(sel_rows[g * EPG:(g + 1) * EPG])
        gv = m1 + m2
        if best_v is None:
            best_v = gv
        else:
            u = gv > best_v
            best = jnp.where(u, float(g), best)
            best_v = jnp.where(u, gv, best_v)
    sel_g = [_pick(best, [sel_rows[g * EPG + i] for g in range(N_GROUPS)]) for i in range(EPG)]
    sc_g = [_pick(best, [sc_rows[g * EPG + i] for g in range(N_GROUPS)]) for i in range(EPG)]
    i1, i2, _, _ = _top2(sel_g)
    s1 = _pick(i1, sc_g)
    s2 = _pick(i2, sc_g)
    tot = s1 + s2
    first_low = i1 < i2
    e_lo = jnp.where(first_low, i1, i2)
    e_hi = jnp.where(first_low, i2, i1)
    w_lo = jnp.where(first_low, s1, s2) / tot
    w_hi = jnp.where(first_low, s2, s1) / tot
    pid = jnp.where(e_lo == 0.0, e_hi - 1.0, jnp.where(e_lo == 1.0, e_hi + 1.0, 5.0))
    bucket = best * float(len(PAIRS)) + pid
    t = sel.shape[1]
    meta_ref[0] = jnp.concatenate([w_lo, w_hi, bucket, jnp.zeros((SUB - 3, t), F32)], axis=0)


def _merge_call(x, ya, yb, yb_ctx, ga, gb, mod, nw, w, wr2, rb, *, n_rows, n_lat, seq):
    d = x.shape[1]
    tm = TM_MERGE
    nt = n_rows // tm
    n_lat_t = n_lat // tm
    tps = seq // tm
    ctx_row = n_lat // seq

    def tok(i):
        return (i, 0)

    def full(a):
        return pl.BlockSpec(a.shape, lambda i: (0,) * a.ndim)

    yb_specs = [pl.BlockSpec((tm, 512), lambda i: (jnp.minimum(i, n_lat_t - 1), 0))]
    yb_args = [yb]
    if yb_ctx is not None:
        yb_specs.append(pl.BlockSpec((tm, 512), lambda i: (jnp.maximum(i - n_lat_t, 0), 0)))
        yb_args.append(yb_ctx)

    def body(x_ref, ya_ref, *refs):
        if yb_ctx is None:
            refs = refs[:1] + (None,) + refs[1:]
        _merge_kernel(x_ref, ya_ref, *refs, n_lat_tiles=n_lat_t)

    return pl.pallas_call(
        body,
        grid=(nt,),
        in_specs=[pl.BlockSpec((tm, d), tok), pl.BlockSpec((tm, 512), tok)] + yb_specs + [
                  pl.BlockSpec((tm, d), tok), pl.BlockSpec((tm, d), tok),
                  pl.BlockSpec((1, 8, d), lambda i: (jnp.where(i < n_lat_t, i // tps, ctx_row), 0, 0)),
                  full(nw), full(w["woa"]), full(w["wob"]), full(w["wout"]), full(wr2), full(rb)],
        out_specs=[pl.BlockSpec((tm, d), tok), pl.BlockSpec((tm * SUB, LANES), tok),
                   pl.BlockSpec((1, SUB, tm), lambda i: (i, 0, 0))],
        out_shape=[jax.ShapeDtypeStruct((n_rows, d), F32), jax.ShapeDtypeStruct((n_rows * SUB, LANES), F32),
                   jax.ShapeDtypeStruct((nt, SUB, tm), F32)],
        compiler_params=_params(("arbitrary",)),
        name="merge",
    )(x, ya, *yb_args, ga, gb, mod, nw, w["woa"], w["wob"], w["wout"], wr2, rb)


def _moe_kernel(tok_ref, e1_ref, e2_ref, nv_ref, wts_ref, h_hbm, wg1_ref, wu1_ref, wd1_ref, wg2_ref, wu2_ref, wd2_ref,
                f_hbm, hbuf, obuf, gsem, ssem, *, n_tok, n_tiles):
    t = pl.program_id(0)
    slot = t % MOE_BUFS
    slot1 = (t + 1) % MOE_BUFS
    slot2 = (t + 2) % MOE_BUFS
    live = nv_ref[t] > 0
    prev_live = jnp.logical_and(t >= 1, nv_ref[jnp.maximum(t - 1, 0)] > 0)

    def gather_row(tile, s, r):
        tk = jnp.minimum(tok_ref[tile * TE + r], n_tok - 1)
        return pltpu.make_async_copy(h_hbm.at[pl.ds(tk * SUB, SUB)], hbuf.at[s, pl.ds(r * SUB, SUB)], gsem.at[s])

    def scatter_row(tile, s, r):
        tk = tok_ref[tile * TE + r]
        return pltpu.make_async_copy(obuf.at[s, pl.ds(r * SUB, SUB)], f_hbm.at[pl.ds(tk * SUB, SUB)], ssem.at[s])

    def wait_gather(s):
        pltpu.make_async_copy(h_hbm.at[pl.ds(0, TE * SUB)], hbuf.at[s], gsem.at[s]).wait()

    def wait_scatter(s):
        pltpu.make_async_copy(obuf.at[s], f_hbm.at[pl.ds(0, TE * SUB)], ssem.at[s]).wait()

    def start_rows(make_row, tile, s):
        def body(i, carry):
            make_row(tile, s, 2 * i).start(priority=0)
            make_row(tile, s, 2 * i + 1).start(priority=1)
            return carry
        lax.fori_loop(0, TE // 2, body, 0, unroll=4)

    @pl.when(t == 0)
    def _():
        obuf[...] = jnp.zeros(obuf.shape, F32)
        for half in range(2):
            fill = pltpu.make_async_copy(obuf.at[0], f_hbm.at[pl.ds((n_tok + half * TE) * SUB, TE * SUB)], ssem.at[0])
            fill.start()
            fill.wait()
        start_rows(gather_row, 0, 0)
        start_rows(gather_row, 1, 1)

    @pl.when(live)
    def _():
        wait_gather(slot)

        @pl.when(t >= 2)
        def _():
            wait_scatter(slot)

    @pl.when(live)
    def _():
        g_tile = jnp.minimum(t + 2, n_tiles - 1)
        s_tile = jnp.where(t == 0, n_tiles - 1, t - 1)
        issue = ([functools.partial(gather_row, g_tile, slot2, r) for r in range(TE)]
                 + [functools.partial(scatter_row, s_tile, slot2, r) for r in range(TE)])
        n_stage = 6
        per_stage = -(-len(issue) // n_stage)

        def issue_stage(k):
            for i, make in enumerate(issue[k * per_stage:(k + 1) * per_stage]):
                make().start(priority=i % 2)

        h = _load_token_rows(hbuf.at[slot], TE).astype(BF16)
        w_lo = jnp.broadcast_to(wts_ref[0, 0:1, :], (LANES, TE)).T[:, :1]
        w_hi = jnp.broadcast_to(wts_ref[0, 1:2, :], (LANES, TE)).T[:, :1]
        outs = []
        stage = 0
        for wg_ref, wu_ref, wd_ref, wt in ((wg1_ref, wu1_ref, wd1_ref, w_lo), (wg2_ref, wu2_ref, wd2_ref, w_hi)):
            hg = _dot(h, wg_ref[0, 0])
            issue_stage(stage)
            hu = _dot(h, wu_ref[0, 0])
            issue_stage(stage + 1)
            act = (hg * jax.nn.sigmoid(hg) * hu).astype(BF16)
            outs.append(wt * _dot(act, wd_ref[0, 0]))
            issue_stage(stage + 2)
            stage += 3
        _store_token_rows(obuf.at[slot], outs[0] + outs[1])

    @pl.when(jnp.logical_and(jnp.logical_not(live), prev_live))
    def _():
        wait_gather(slot)
        wait_gather(slot1)
        wait_scatter(slot1)

        @pl.when(t >= 2)
        def _():
            wait_scatter(slot)
        start_rows(scatter_row, t - 1, slot2)
        wait_scatter(slot2)


def _moe_call(tok, e1, e2, nv, wts, h3, wg, wu, wd, *, layer, n_tok):
    d = wg.shape[2]
    de = wg.shape[3]
    n_tiles = e1.shape[0]

    def wspec(shape, which):
        return pl.BlockSpec((1, 1) + shape, lambda t, tok, e1, e2, nv: (layer, (e1, e2)[which][t], 0, 0))

    grid_spec = pltpu.PrefetchScalarGridSpec(
        num_scalar_prefetch=4,
        grid=(n_tiles,),
        in_specs=[pl.BlockSpec((1, 2, TE), lambda t, tok, e1, e2, nv: (t, 0, 0)),
                  pl.BlockSpec(memory_space=pl.ANY),
                  wspec((d, de), 0), wspec((d, de), 0), wspec((de, d), 0),
                  wspec((d, de), 1), wspec((d, de), 1), wspec((de, d), 1)],
        out_specs=pl.BlockSpec(memory_space=pl.ANY),
        scratch_shapes=[pltpu.VMEM((MOE_BUFS, TE * SUB, LANES), F32), pltpu.VMEM((MOE_BUFS, TE * SUB, LANES), F32),
                        pltpu.SemaphoreType.DMA((MOE_BUFS,)), pltpu.SemaphoreType.DMA((MOE_BUFS,))],
    )
    return pl.pallas_call(
        functools.partial(_moe_kernel, n_tok=n_tok, n_tiles=n_tiles),
        grid_spec=grid_spec,
        out_shape=jax.ShapeDtypeStruct(((n_tok + 2 * TE) * SUB, LANES), F32),
        compiler_params=_params(("arbitrary",)),
        name="moe",
    )(tok, e1, e2, nv, wts, h3, wg, wu, wd, wg, wu, wd)


def _route_tables(meta, n_tok):
    w_lo = meta[:, 0, :].reshape(-1)
    w_hi = meta[:, 1, :].reshape(-1)
    bucket = meta[:, 2, :].reshape(-1).astype(jnp.int32)
    n_slots = n_tok + N_BUCKETS * TE
    n_tiles = n_slots // TE
    buckets = jnp.arange(N_BUCKETS, dtype=jnp.int32)
    counts = jnp.sum(bucket[:, None] == buckets[None, :], axis=0, dtype=jnp.int32)
    padded = ((counts + TE - 1) // TE) * TE
    pad_end = jnp.cumsum(padded)
    pad_start = pad_end - padded
    fill_i = jnp.arange(TE, dtype=jnp.int32)[None, :]
    fill_key = jnp.where(fill_i < (padded - counts)[:, None], 2 * buckets[:, None] + 1, 2 * N_BUCKETS).reshape(-1)
    keys = jnp.concatenate([2 * bucket, fill_key])
    ids = jnp.concatenate([jnp.arange(n_tok, dtype=jnp.int32), jnp.full((N_BUCKETS * TE,), -1, jnp.int32)])
    zeros = jnp.zeros((N_BUCKETS * TE,), F32)
    _, ids, w_lo, w_hi = lax.sort((keys, ids, jnp.concatenate([w_lo, zeros]), jnp.concatenate([w_hi, zeros])),
                                  num_keys=1, is_stable=True)
    slot = jnp.arange(n_slots, dtype=jnp.int32)
    tok = jnp.where(ids < 0, n_tok + ((slot // TE) % 2) * TE + slot % TE, ids)
    wts = jnp.stack([w_lo.reshape(n_tiles, TE), w_hi.reshape(n_tiles, TE)], axis=1)

    tile_start = jnp.arange(n_tiles, dtype=jnp.int32) * TE
    tb = jnp.sum(tile_start[:, None] >= pad_end[None, :], axis=1, dtype=jnp.int32)
    used = tb < N_BUCKETS
    onehot = (tb[:, None] == buckets[None, :]).astype(jnp.int32)
    nv = jnp.clip(jnp.sum(onehot * (counts + pad_start)[None, :], axis=1) - tile_start, 0, TE) * used
    tbe = jnp.where(used, tb, jnp.max(jnp.where(used, tb, 0)))
    pid = tbe % len(PAIRS)
    pair_lo = (pid >= 3).astype(jnp.int32) + (pid >= 5).astype(jnp.int32)
    pair_hi = pid + 1 - 2 * (pid >= 3).astype(jnp.int32) - (pid >= 5).astype(jnp.int32)
    e1 = (tbe // len(PAIRS)) * EPG + pair_lo
    e2 = (tbe // len(PAIRS)) * EPG + pair_hi
    return tok.astype(jnp.int32), e1.astype(jnp.int32), e2.astype(jnp.int32), nv.astype(jnp.int32), wts


def _final_kernel(x_ref, f_ref, mod_ref, nw_ref, o_ref):
    x = x_ref[...] + mod_ref[0, 5:6, :] * _load_token_rows(f_ref, x_ref.shape[0])
    o_ref[...] = _rms(x, nw_ref[...])


def _final_call(x, f, mod, nw, *, n_lat, seq):
    d = x.shape[1]
    tm = TM_MERGE
    tps = seq // tm
    return pl.pallas_call(
        _final_kernel,
        grid=(n_lat // tm,),
        in_specs=[pl.BlockSpec((tm, d), lambda i: (i, 0)), pl.BlockSpec((tm * SUB, LANES), lambda i: (i, 0)),
                  pl.BlockSpec((1, 8, d), lambda i: (i // tps, 0, 0)), pl.BlockSpec((1, d), lambda i: (0, 0))],
        out_specs=pl.BlockSpec((tm, d), lambda i: (i, 0)),
        out_shape=jax.ShapeDtypeStruct((n_lat, d), F32),
        compiler_params=_params(("arbitrary",)),
        name="final",
    )(x, f, mod, nw)


def _rope_table(pos_r, pos_c, dim, lane_of):
    d = dim // 2
    half = d // 2
    lane = np.arange(LANES)
    rl = lane_of(lane)
    is_rope = rl >= 0
    rl = np.maximum(rl, 0)
    use_col = rl >= d
    j = rl % half
    first = (rl % d) < half
    inv = (np.float32(ROPE_BASE) ** (-(2.0 * j).astype(np.float32) / np.float32(d))).astype(np.float32)
    pos = np.where(use_col[None, :], pos_c[:, None], pos_r[:, None]).astype(np.float32)
    ang = pos * inv[None, :]
    cos = np.where(is_rope[None, :], np.cos(ang), 1.0)
    sin = np.where(is_rope[None, :], np.sin(ang), 0.0)
    return np.stack([cos, np.where(first[None, :], -sin, 0.0), np.where(first[None, :], 0.0, sin)]).astype(np.float32)


def _tables(seq):
    t = np.arange(seq)
    rows, cols = t // GRID_W, t % GRID_W
    tab_a = _rope_table(rows, cols, A_HEAD_DIM, lambda lane: lane % A_HEAD_DIM)
    tab_b = _rope_table(rows, cols, B_ROPE,
                        lambda lane: np.where((lane >= B_NOPE) & (lane < B_NOPE + B_ROPE), lane - B_NOPE, -1))
    ident = np.stack([np.ones((TM, LANES), np.float32), np.zeros((TM, LANES), np.float32),
                      np.zeros((TM, LANES), np.float32)])
    return jnp.asarray(np.concatenate([tab_a, ident], axis=1)), jnp.asarray(np.concatenate([tab_b, ident], axis=1))


def _window_band():
    r = np.arange(BLOCK)[:, None]
    c = np.arange(3 * BLOCK)[None, :]
    dist = c - r
    return jnp.asarray(np.where((dist >= BLOCK - WINDOW) & (dist <= BLOCK + WINDOW), 0.0, NEG).astype(np.float32))


def _layer_weights(w_in, w_uq, w_ukv, q_norm, kv_norm, w_o_a, w_o_b, w_out):
    d = w_in.shape[0]
    o = 0
    qa_w = w_in[:, o:o + 512]; o += 512
    ka_w = w_in[:, o:o + 128]; o += 128
    va_w = w_in[:, o:o + 128]; o += 128
    cq_w = w_in[:, o:o + B_Q_RANK]; o += B_Q_RANK
    ckv_w = w_in[:, o:o + B_KV_RANK]; o += B_KV_RANK
    kr_w = w_in[:, o:o + B_ROPE]; o += B_ROPE
    g_w = w_in[:, o:]

    def dup(wk):
        wk = wk.reshape(d, A_KV_HEADS, 1, A_HEAD_DIM)
        return jnp.broadcast_to(wk, (d, A_KV_HEADS, 2, A_HEAD_DIM)).reshape(d, A_KV_HEADS * 2 * A_HEAD_DIM)

    w1 = jnp.concatenate([qa_w, dup(ka_w), dup(va_w), cq_w, ckv_w, jnp.pad(kr_w, ((0, 0), (0, LANES - B_ROPE)))], axis=1)
    wuq = jnp.pad(w_uq.reshape(B_Q_RANK, B_HEADS, B_NOPE + B_ROPE),
                  ((0, 0), (0, 0), (0, LANES - B_NOPE - B_ROPE))).reshape(B_Q_RANK, B_HEADS * LANES)
    ukv = w_ukv.reshape(B_KV_RANK, B_HEADS, B_NOPE + B_V)
    wukn = jnp.pad(ukv[:, :, :B_NOPE], ((0, 0), (0, 0), (0, LANES - B_NOPE))).reshape(B_KV_RANK, B_HEADS * LANES)
    wuv = ukv[:, :, B_NOPE:].reshape(B_KV_RANK, B_HEADS * B_V)
    src = jnp.arange(LANES)[:, None]
    dst = jnp.arange(B_HEADS * LANES)[None, :]
    rp = ((src < B_ROPE) & (dst % LANES == src + B_NOPE)).astype(BF16)
    return dict(w1=w1.astype(BF16), wg=g_w.astype(BF16), qn=q_norm.reshape(1, -1), wuq=wuq.astype(BF16),
                kvn=kv_norm.reshape(1, -1), wukn=wukn.astype(BF16), wuv=wuv.astype(BF16), rp=rp,
                woa=w_o_a.astype(BF16), wob=w_o_b.astype(BF16), wout=w_out.astype(BF16))


def kernel(x, c, ctx, c_ctx, w_mod, b_mod, norm_mix, norm_ffn, w_in, attn_sink, mla_q_norm, w_uq, mla_kv_norm, w_ukv,
           w_o_a, w_o_b, w_out, w_router, router_bias, w_expert_gate, w_expert_up, w_expert_down, final_norm):
    batch, seq, d = x.shape
    ctx_len = ctx.shape[1]
    depth = w_mod.shape[0]
    n_lat = batch * seq
    n_all = n_lat + batch * ctx_len
    for tile in (TM, TM_MERGE):
        assert seq % tile == 0 and (batch * ctx_len) % tile == 0
    assert d == SUB * LANES and seq % TQ == 0 and seq % GRID_W == 0 and batch + 1 <= MOD_ROWS

    c_all = jnp.concatenate([c, c_ctx[None, :], jnp.zeros((MOD_ROWS - batch - 1, d), F32)], axis=0)
    mod = _mod_call(c_all, w_mod, b_mod)
    mod = jnp.pad(mod.transpose(0, 2, 1, 3), ((0, 0), (0, 0), (0, 2), (0, 0)))

    tab_a, tab_b = _tables(seq)
    wr = jnp.pad(w_router.astype(F32), ((0, 0), (0, LANES - N_EXPERTS)))
    wr_hi = wr.astype(BF16)
    wr2 = jnp.concatenate([wr_hi, (wr - wr_hi.astype(F32)).astype(BF16)], axis=1)
    rb = router_bias.astype(F32).reshape(N_EXPERTS, 1)
    wg_all = w_expert_gate.astype(BF16)
    wu_all = w_expert_up.astype(BF16)
    wd_all = w_expert_down.astype(BF16)

    xs = jnp.concatenate([x.reshape(n_lat, d), ctx.reshape(batch * ctx_len, d)], axis=0)
    f = None
    for l in range(depth):
        need_ctx = l < depth - 1
        w = _layer_weights(w_in[l], w_uq[l], w_ukv[l], mla_q_norm[l], mla_kv_norm[l], w_o_a[l], w_o_b[l], w_out[l])
        outs = _proj_call(xs, f, mod[l - 1] if l else None, mod[l], norm_mix[l].reshape(1, d), tab_a, tab_b, w,
                          n_lat=n_lat, seq=seq)
        if l:
            xs, outs = outs[0], outs[1:]
        qa, ka, va, qb, kb, vb, ga, gb = outs
        ya = _window_call(attn_sink[l].astype(F32), qa, ka, va, batch=batch, seq=seq, ctx_len=ctx_len, need_ctx=need_ctx)
        yb = _mla_call(qb, kb, vb, batch=batch, seq=seq, ctx_len=ctx_len, latent=True)
        yb_ctx = _mla_call(qb, kb, vb, batch=batch, seq=seq, ctx_len=ctx_len, latent=False) if need_ctx else None
        n_rows = n_all if need_ctx else n_lat
        xs, h3, meta = _merge_call(xs, ya, yb, yb_ctx, ga, gb, mod[l], norm_ffn[l].reshape(1, d), w, wr2, rb,
                                   n_rows=n_rows, n_lat=n_lat, seq=seq)
        tok, e1, e2, nv, wts = _route_tables(meta, n_rows)
        f = _moe_call(tok, e1, e2, nv, wts, h3, wg_all, wu_all, wd_all, layer=l, n_tok=n_rows)
    out = _final_call(xs, f, mod[depth - 1], final_norm.reshape(1, d), n_lat=n_lat, seq=seq)
    return out.reshape(batch, seq, d)
```

```python
import functools

import numpy as np
import jax
import jax.numpy as jnp
from jax import lax
from jax.experimental import pallas as pl
from jax.experimental.pallas import tpu as pltpu

F32 = jnp.float32
BF16 = jnp.bfloat16

EPS = 1e-6
ROPE_BASE = 10000.0
GRID_W = 64
BLOCK = 128
A_HEADS, A_KV_HEADS, A_HEAD_DIM = 8, 2, 64
A_GROUP = A_HEADS // A_KV_HEADS
WINDOW = 128
B_HEADS, B_NOPE, B_ROPE, B_V = 8, 64, 32, 64
B_Q_RANK, B_KV_RANK = 256, 128
N_EXPERTS, N_GROUPS = 16, 4
EPG = N_EXPERTS // N_GROUPS
D_EXPERT = 512
PAIRS = ((0, 1), (0, 2), (0, 3), (1, 2), (1, 3), (2, 3))
N_BUCKETS = N_GROUPS * len(PAIRS)

LANES = 128
SUB = 8
TM = 512
TM_MERGE = 512
TQ = 512
MLA_PAIRS = 4
TE = 256
MOE_BUFS = 3
MOD_ROWS = 24
NEG = -1e30
LOG2E = 1.4426950408889634
VMEM_LIMIT = 56 * 1024 * 1024

W1_QA, W1_KA, W1_VA, W1_CQ, W1_CKV, W1_KR = 0, 512, 768, 1024, 1280, 1408
W1_COLS = 1536


def _params(sem, vmem=VMEM_LIMIT):
    return pltpu.CompilerParams(dimension_semantics=sem, vmem_limit_bytes=vmem)


def _dot(a, b):
    return jnp.dot(a, b, preferred_element_type=F32)


def _dot_nt(a, b):
    return lax.dot_general(a, b, (((1,), (1,)), ((), ())), preferred_element_type=F32)


def _rms(x, g):
    return x * lax.rsqrt(jnp.mean(x * x, axis=-1, keepdims=True) + EPS) * g


def _lane_tile(t, width):
    return jnp.concatenate([t] * (width // LANES), axis=1)


def _rope(t, tab_ref, shift):
    w = t.shape[-1]
    up = pltpu.roll(t, w - shift, 1)
    dn = pltpu.roll(t, shift, 1)
    return (t * _lane_tile(tab_ref[0], w) + up * _lane_tile(tab_ref[1], w)
            + dn * _lane_tile(tab_ref[2], w))


def _load_token_rows(ref, rows):
    return jnp.concatenate([ref[pl.ds(s, rows, stride=SUB), :] for s in range(SUB)], axis=1)


def _store_token_rows(ref, val):
    rows = val.shape[0]
    for s in range(SUB):
        ref[pl.ds(s, rows, stride=SUB), :] = val[:, s * LANES:(s + 1) * LANES]


def _mod_kernel(c_ref, w_ref, b_ref, o_ref):
    c = c_ref[...]
    a = (c * jax.nn.sigmoid(c)).astype(BF16)
    o_ref[0, 0] = _dot(a, w_ref[0].astype(BF16)) + b_ref[0, 0]


def _mod_call(c_all, w_mod, b_mod):
    depth, d, _ = w_mod.shape
    return pl.pallas_call(
        _mod_kernel,
        grid=(depth, 6),
        in_specs=[pl.BlockSpec((MOD_ROWS, d), lambda l, k: (0, 0)),
                  pl.BlockSpec((1, d, d), lambda l, k: (l, 0, k)),
                  pl.BlockSpec((1, 1, 1, d), lambda l, k: (l, k, 0, 0))],
        out_specs=pl.BlockSpec((1, 1, MOD_ROWS, d), lambda l, k: (l, k, 0, 0)),
        out_shape=jax.ShapeDtypeStruct((depth, 6, MOD_ROWS, d), F32),
        compiler_params=_params(("arbitrary", "arbitrary")),
        name="mod",
    )(c_all, w_mod, b_mod.reshape(depth, 6, 1, d))


def _proj_kernel(*refs, has_f):
    if has_f:
        x_ref, f_ref, modp_ref, refs = refs[0], refs[1], refs[2], refs[3:]
    else:
        x_ref, refs = refs[0], refs[1:]
    (mod_ref, nw_ref, ta_ref, tb_ref, w1_ref, wg_ref, qn_ref, wuq_ref, kvn_ref, wukn_ref, wuv_ref,
     rp_ref), refs = refs[:12], refs[12:]
    if has_f:
        xo_ref, refs = refs[0], refs[1:]
    qa_ref, ka_ref, va_ref, qb_ref, kb_ref, vb_ref, ga_ref, gb_ref = refs

    x = x_ref[...]
    if has_f:
        x = x + modp_ref[0, 5:6, :] * _load_token_rows(f_ref, x.shape[0])
        xo_ref[...] = x
    h = (_rms(x, nw_ref[...]) * (1.0 + mod_ref[0, 1:2, :]) + mod_ref[0, 0:1, :]).astype(BF16)

    t = _dot(h, w1_ref[...])
    qa_ref[...] = (_rope(t[:, W1_QA:W1_KA], ta_ref, 16) * (A_HEAD_DIM ** -0.5 * LOG2E)).astype(BF16)
    ka_ref[...] = _rope(t[:, W1_KA:W1_VA], ta_ref, 16).astype(BF16)
    va_ref[...] = t[:, W1_VA:W1_CQ].astype(BF16)

    cq = _rms(t[:, W1_CQ:W1_CKV], qn_ref[...]).astype(BF16)
    qb = _rope(_dot(cq, wuq_ref[...]), tb_ref, 8)
    qb_ref[...] = (qb * ((B_NOPE + B_ROPE) ** -0.5 * LOG2E)).astype(BF16)

    ckv = _rms(t[:, W1_CKV:W1_KR], kvn_ref[...]).astype(BF16)
    vb_ref[...] = _dot(ckv, wuv_ref[...]).astype(BF16)
    kr = t[:, W1_KR:W1_COLS]
    kr_hi = kr.astype(BF16)
    kr_lo = (kr - kr_hi.astype(F32)).astype(BF16)
    kb = _dot(ckv, wukn_ref[...]) + _dot(kr_hi, rp_ref[...]) + _dot(kr_lo, rp_ref[...])
    kb_ref[...] = _rope(kb, tb_ref, 8).astype(BF16)

    d = ga_ref.shape[-1]
    ga_ref[...] = jax.nn.sigmoid(_dot(h, wg_ref[:, :d])).astype(BF16)
    gb_ref[...] = jax.nn.sigmoid(_dot(h, wg_ref[:, d:])).astype(BF16)


def _proj_call(x, f, modp, mod, nw, tab_a, tab_b, w, *, n_lat, seq):
    n, d = x.shape
    nt = n // TM
    n_lat_t = n_lat // TM
    tps = seq // TM
    ctx_row = n_lat // seq
    has_f = f is not None

    def tok(i):
        return (i, 0)

    def modi(i):
        return (jnp.where(i < n_lat_t, i // tps, ctx_row), 0, 0)

    def tabi(i):
        return (0, jnp.where(i < n_lat_t, i % tps, tps), 0)

    def full(a):
        return pl.BlockSpec(a.shape, lambda i: (0,) * a.ndim, pipeline_mode=pl.Buffered(1))

    mod_spec = pl.BlockSpec((1, 8, d), modi)
    in_specs = [pl.BlockSpec((TM, d), tok)]
    args = [x]
    if has_f:
        in_specs += [pl.BlockSpec((TM * SUB, LANES), tok), mod_spec]
        args += [f, modp]
    in_specs += [mod_spec, full(nw), pl.BlockSpec((3, TM, LANES), tabi), pl.BlockSpec((3, TM, LANES), tabi)]
    args += [mod, nw, tab_a, tab_b]
    for k in ("w1", "wg", "qn", "wuq", "kvn", "wukn", "wuv", "rp"):
        in_specs.append(full(w[k]))
        args.append(w[k])

    widths = (512, 256, 256, 1024, 1024, 512, d, d)
    out_shape = [jax.ShapeDtypeStruct((n, wd), BF16) for wd in widths]
    out_specs = [pl.BlockSpec((TM, wd), tok) for wd in widths]
    if has_f:
        out_shape.insert(0, jax.ShapeDtypeStruct((n, d), F32))
        out_specs.insert(0, pl.BlockSpec((TM, d), tok))
    return pl.pallas_call(
        functools.partial(_proj_kernel, has_f=has_f),
        grid=(nt,), in_specs=in_specs, out_specs=out_specs, out_shape=out_shape,
        compiler_params=_params(("arbitrary",)),
        name="proj",
    )(*args)


def _window_kernel(sink_ref, band_ref, q_ref, kp_ref, kc_ref, kn_ref, kx_ref, vp_ref, vc_ref, vn_ref, vx_ref, o_ref,
                   *, nb):
    rows = A_GROUP * BLOCK
    lo = lax.broadcasted_iota(jnp.int32, (BLOCK, LANES), 1) < A_HEAD_DIM
    col_blk = lax.broadcasted_iota(jnp.int32, (1, 3 * BLOCK), 1) // BLOCK
    row_head = lax.broadcasted_iota(jnp.int32, (rows, 1), 0) // BLOCK
    zero = jnp.zeros((BLOCK, LANES), BF16)
    first, second = slice(0, BLOCK), slice(BLOCK, 2 * BLOCK)
    for sub in range(2):
        n = 2 * pl.program_id(1) + sub
        qrows = (first, second)[sub]
        off_prev = jnp.where(jnp.logical_or(n == 0, n >= nb), NEG, 0.0)
        off_cur = jnp.where(n >= nb, NEG, 0.0)
        off_next = jnp.where(n >= nb - 1, NEG, 0.0)
        bias = band_ref[...] + jnp.where(col_blk == 0, off_prev, jnp.where(col_blk == 1, off_cur, off_next))
        bias = jnp.concatenate([bias] * A_GROUP, axis=0)
        for k in range(A_KV_HEADS):
            ksl = slice(k * LANES, (k + 1) * LANES)
            parts = []
            for j in range(A_GROUP // 2):
                blk = q_ref[qrows, k * 2 * LANES + j * LANES:k * 2 * LANES + (j + 1) * LANES]
                parts += [jnp.where(lo, blk, zero), jnp.where(lo, zero, blk)]
            qs = jnp.concatenate(parts, axis=0)
            if sub == 0:
                k_lat = jnp.concatenate([kp_ref[:, ksl], kc_ref[:, ksl]], axis=0)
                v_lat = jnp.concatenate([vp_ref[:, ksl], vc_ref[:, ksl]], axis=0)
            else:
                k_lat = jnp.concatenate([kc_ref[:, ksl], kn_ref[:, ksl]], axis=0)
                v_lat = jnp.concatenate([vc_ref[:, ksl], vn_ref[:, ksl]], axis=0)
            s_l = _dot_nt(qs, k_lat) + bias
            s_x = _dot_nt(qs, kx_ref[:, ksl])
            sink = jnp.zeros((rows, 1), F32)
            for g in range(A_GROUP):
                sink = jnp.where(row_head == g, sink_ref[k * A_GROUP + g] * LOG2E, sink)
            m = jnp.maximum(jnp.maximum(jnp.max(s_l, axis=-1, keepdims=True), jnp.max(s_x, axis=-1, keepdims=True)),
                            sink)
            e_l = jnp.exp2(s_l - m)
            e_x = jnp.exp2(s_x - m)
            den = jnp.sum(e_l, axis=-1, keepdims=True) + jnp.sum(e_x, axis=-1, keepdims=True) + jnp.exp2(sink - m)
            o = (_dot(e_l.astype(BF16), v_lat) + _dot(e_x.astype(BF16), vx_ref[:, ksl])) / den
            for j in range(A_GROUP // 2):
                even = o[(2 * j) * BLOCK:(2 * j + 1) * BLOCK]
                odd = o[(2 * j + 1) * BLOCK:(2 * j + 2) * BLOCK]
                o_ref[qrows, k * 2 * LANES + j * LANES:k * 2 * LANES + (j + 1) * LANES] = (
                    jnp.where(lo, even, odd).astype(BF16))


def _window_call(sink, qa, ka, va, *, batch, seq, ctx_len, need_ctx):
    n = qa.shape[0] if need_ctx else batch * seq
    nb = seq // BLOCK
    ncb = ctx_len // BLOCK
    assert nb % 2 == 0 and ncb % 2 == 0
    nb2, ncb2 = nb // 2, ncb // 2
    nq = nb2 + (ncb2 if need_ctx else 0)

    def qi(b, i):
        return (jnp.where(i < nb2, b * nb2 + i, batch * nb2 + b * ncb2 + (i - nb2)), 0)

    def ki(off):
        return lambda b, i: (b * nb + jnp.clip(2 * i + off, 0, nb - 1), 0)

    def kc(b, i):
        return (b * nb2 + jnp.minimum(i, nb2 - 1), 0)

    def xi(b, i):
        return (batch * seq // ctx_len + b, 0)

    kv_specs = [pl.BlockSpec((BLOCK, 2 * LANES), ki(-1)), pl.BlockSpec((2 * BLOCK, 2 * LANES), kc),
                pl.BlockSpec((BLOCK, 2 * LANES), ki(2)), pl.BlockSpec((ctx_len, 2 * LANES), xi)]
    return pl.pallas_call(
        functools.partial(_window_kernel, nb=nb),
        grid=(batch, nq),
        in_specs=[pl.BlockSpec(memory_space=pltpu.SMEM), pl.BlockSpec((BLOCK, 3 * BLOCK), lambda b, i: (0, 0)),
                  pl.BlockSpec((2 * BLOCK, 4 * LANES), qi)] + kv_specs + kv_specs,
        out_specs=pl.BlockSpec((2 * BLOCK, 4 * LANES), qi),
        out_shape=jax.ShapeDtypeStruct((n, 4 * LANES), BF16),
        compiler_params=_params(("arbitrary", "arbitrary")),
        name="window",
    )(sink, _window_band(), qa, ka, ka, ka, ka, va, va, va, va)


def _mla_kernel(q_ref, *refs, with_lat):
    if with_lat:
        kl_ref, kx_ref, vl_ref, vx_ref, o_ref, vaug_ref = refs
    else:
        kx_ref, vx_ref, o_ref, vaug_ref = refs
    n_ctx = vx_ref.shape[0]
    n_pairs = o_ref.shape[1] // LANES
    lo = lax.broadcasted_iota(jnp.int32, (o_ref.shape[0], LANES), 1) < B_V

    @pl.when(pl.program_id(2) == 0)
    def _():
        one = jnp.ones((1, LANES), BF16)
        for v_ref, start in ((vx_ref, 0),) + (((vl_ref, n_ctx),) if with_lat else ()):
            n = v_ref.shape[0]
            keep = lax.broadcasted_iota(jnp.int32, (n, LANES), 1) < B_V
            for pp in range(n_pairs):
                v = v_ref[:, pp * LANES:(pp + 1) * LANES]
                vaug_ref[2 * pp, start:start + n, :] = jnp.where(keep, v, one)
                vaug_ref[2 * pp + 1, start:start + n, :] = jnp.where(keep, one, v)

    def head(h):
        sl = slice(h * LANES, (h + 1) * LANES)
        q = q_ref[:, sl]
        s_x = _dot_nt(q, kx_ref[:, sl])
        m = jnp.max(s_x, axis=-1, keepdims=True)
        if with_lat:
            s_l = _dot_nt(q, kl_ref[:, sl])
            m = jnp.maximum(m, jnp.max(s_l, axis=-1, keepdims=True))
        o = _dot(jnp.exp2(s_x - m).astype(BF16), vaug_ref[h, :n_ctx, :])
        if with_lat:
            o = o + _dot(jnp.exp2(s_l - m).astype(BF16), vaug_ref[h, n_ctx:, :])
        return o / pltpu.roll(o, B_V, 1)

    for pp in range(n_pairs):
        o_ref[:, pp * LANES:(pp + 1) * LANES] = jnp.where(lo, head(2 * pp), head(2 * pp + 1)).astype(BF16)


def _mla_call(qb, kb, vb, *, batch, seq, ctx_len, latent):
    ctx0 = batch * seq // ctx_len
    groups = B_HEADS // (2 * MLA_PAIRS)
    qk_w = MLA_PAIRS * 2 * LANES
    v_w = MLA_PAIRS * LANES
    tq = TQ if latent else ctx_len
    nq = seq // TQ if latent else 1
    kx_spec = pl.BlockSpec((ctx_len, qk_w), lambda b, p, j: (ctx0 + b, p))
    vx_spec = pl.BlockSpec((ctx_len, v_w), lambda b, p, j: (ctx0 + b, p))
    if latent:
        in_specs = [pl.BlockSpec((tq, qk_w), lambda b, p, j: (b * nq + j, p)),
                    pl.BlockSpec((seq, qk_w), lambda b, p, j: (b, p)), kx_spec,
                    pl.BlockSpec((seq, v_w), lambda b, p, j: (b, p)), vx_spec]
        args = (qb, kb, kb, vb, vb)
    else:
        in_specs = [pl.BlockSpec((tq, qk_w), lambda b, p, j: (ctx0 + b, p)), kx_spec, vx_spec]
        args = (qb, kb, vb)
    return pl.pallas_call(
        functools.partial(_mla_kernel, with_lat=latent),
        grid=(batch, groups, nq),
        in_specs=in_specs,
        out_specs=pl.BlockSpec((tq, v_w), lambda b, p, j: (b * nq + j, p)),
        out_shape=jax.ShapeDtypeStruct((batch * nq * tq, groups * v_w), BF16),
        scratch_shapes=[pltpu.VMEM((2 * MLA_PAIRS, ctx_len + (seq if latent else 0), LANES), BF16)],
        compiler_params=_params(("arbitrary", "arbitrary", "arbitrary")),
        name="mla" if latent else "mla_ctx",
    )(*args)


def _top2(v):
    i1 = jnp.zeros_like(v[0])
    m1 = v[0]
    for i in range(1, EPG):
        u = v[i] > m1
        i1 = jnp.where(u, float(i), i1)
        m1 = jnp.where(u, v[i], m1)
    i2 = jnp.zeros_like(v[0])
    m2 = jnp.full_like(v[0], -jnp.inf)
    for i in range(EPG):
        cand = jnp.where(i1 == float(i), -jnp.inf, v[i])
        u = cand > m2
        i2 = jnp.where(u, float(i), i2)
        m2 = jnp.where(u, cand, m2)
    return i1, i2, m1, m2


def _pick(idx, vals):
    out = vals[0]
    for i in range(1, len(vals)):
        out = jnp.where(idx == float(i), vals[i], out)
    return out


def _merge_kernel(x_ref, ya_ref, yb_ref, ybx_ref, ga_ref, gb_ref, mod_ref, nw_ref, woa_ref, wob_ref, wout_ref, wr2_ref,
                  rb_ref, xo_ref, h_ref, meta_ref, *, n_lat_tiles):
    yb = yb_ref[...] if ybx_ref is None else jnp.where(pl.program_id(0) < n_lat_tiles, yb_ref[...], ybx_ref[...])
    a = _dot(ya_ref[...], woa_ref[...])
    b = _dot(yb, wob_ref[...])
    mix = (ga_ref[...].astype(F32) * a + gb_ref[...].astype(F32) * b).astype(BF16)
    x = x_ref[...] + mod_ref[0, 2:3, :] * _dot(mix, wout_ref[...])
    xo_ref[...] = x
    h = _rms(x, nw_ref[...]) * (1.0 + mod_ref[0, 4:5, :]) + mod_ref[0, 3:4, :]
    _store_token_rows(h_ref, h)

    h_hi = h.astype(BF16)
    h_lo = (h - h_hi.astype(F32)).astype(BF16)
    l_hi = _dot(h_hi, wr2_ref[...])
    logits = l_hi[:, :LANES] + l_hi[:, LANES:] + _dot(h_lo, wr2_ref[:, :LANES])
    sc = jax.nn.sigmoid(logits.T[:N_EXPERTS, :])
    sel = sc + rb_ref[...]
    sel_rows = [sel[e:e + 1, :] for e in range(N_EXPERTS)]
    sc_rows = [sc[e:e + 1, :] for e in range(N_EXPERTS)]
    best = jnp.zeros_like(sel_rows[0])
    best_v = None
    for g in range(N_GROUPS):
        _, _, m1, m2 = _top2(sel_rows[g * EPG:(g + 1) * EPG])
        gv = m1 + m2
        if best_v is None:
            best_v = gv
        else:
            u = gv > best_v
            best = jnp.where(u, float(g), best)
            best_v = jnp.where(u, gv, best_v)
    sel_g = [_pick(best, [sel_rows[g * EPG + i] for g in range(N_GROUPS)]) for i in range(EPG)]
    sc_g = [_pick(best, [sc_rows[g * EPG + i] for g in range(N_GROUPS)]) for i in range(EPG)]
    i1, i2, _, _ = _top2(sel_g)
    s1 = _pick(i1, sc_g)
    s2 = _pick(i2, sc_g)
    tot = s1 + s2
    first_low = i1 < i2
    e_lo = jnp.where(first_low, i1, i2)
    e_hi = jnp.where(first_low, i2, i1)
    w_lo = jnp.where(first_low, s1, s2) / tot
    w_hi = jnp.where(first_low, s2, s1) / tot
    pid = jnp.where(e_lo == 0.0, e_hi - 1.0, jnp.where(e_lo == 1.0, e_hi + 1.0, 5.0))
    bucket = best * float(len(PAIRS)) + pid
    t = sel.shape[1]
    meta_ref[0] = jnp.concatenate([w_lo, w_hi, bucket, jnp.zeros((SUB - 3, t), F32)], axis=0)


def _merge_call(x, ya, yb, yb_ctx, ga, gb, mod, nw, w, wr2, rb, *, n_rows, n_lat, seq):
    d = x.shape[1]
    tm = TM_MERGE
    nt = n_rows // tm
    n_lat_t = n_lat // tm
    tps = seq // tm
    ctx_row = n_lat // seq

    def tok(i):
        return (i, 0)

    def full(a):
        return pl.BlockSpec(a.shape, lambda i: (0,) * a.ndim)

    yb_specs = [pl.BlockSpec((tm, 512), lambda i: (jnp.minimum(i, n_lat_t - 1), 0))]
    yb_args = [yb]
    if yb_ctx is not None:
        yb_specs.append(pl.BlockSpec((tm, 512), lambda i: (jnp.maximum(i - n_lat_t, 0), 0)))
        yb_args.append(yb_ctx)

    def body(x_ref, ya_ref, *refs):
        if yb_ctx is None:
            refs = refs[:1] + (None,) + refs[1:]
        _merge_kernel(x_ref, ya_ref, *refs, n_lat_tiles=n_lat_t)

    return pl.pallas_call(
        body,
        grid=(nt,),
        in_specs=[pl.BlockSpec((tm, d), tok), pl.BlockSpec((tm, 512), tok)] + yb_specs + [
                  pl.BlockSpec((tm, d), tok), pl.BlockSpec((tm, d), tok),
                  pl.BlockSpec((1, 8, d), lambda i: (jnp.where(i < n_lat_t, i // tps, ctx_row), 0, 0)),
                  full(nw), full(w["woa"]), full(w["wob"]), full(w["wout"]), full(wr2), full(rb)],
        out_specs=[pl.BlockSpec((tm, d), tok), pl.BlockSpec((tm * SUB, LANES), tok),
                   pl.BlockSpec((1, SUB, tm), lambda i: (i, 0, 0))],
        out_shape=[jax.ShapeDtypeStruct((n_rows, d), F32), jax.ShapeDtypeStruct((n_rows * SUB, LANES), F32),
                   jax.ShapeDtypeStruct((nt, SUB, tm), F32)],
        compiler_params=_params(("arbitrary",)),
        name="merge",
    )(x, ya, *yb_args, ga, gb, mod, nw, w["woa"], w["wob"], w["wout"], wr2, rb)


def _moe_kernel(tok_ref, e1_ref, e2_ref, nv_ref, wts_ref, h_hbm, wg1_ref, wu1_ref, wd1_ref, wg2_ref, wu2_ref, wd2_ref,
                f_hbm, hbuf, obuf, gsem, ssem, *, n_tok, n_tiles):
    t = pl.program_id(0)
    slot = t % MOE_BUFS
    slot1 = (t + 1) % MOE_BUFS
    slot2 = (t + 2) % MOE_BUFS
    live = nv_ref[t] > 0
    prev_live = jnp.logical_and(t >= 1, nv_ref[jnp.maximum(t - 1, 0)] > 0)

    def gather_row(tile, s, r):
        tk = jnp.minimum(tok_ref[tile * TE + r], n_tok - 1)
        return pltpu.make_async_copy(h_hbm.at[pl.ds(tk * SUB, SUB)], hbuf.at[s, pl.ds(r * SUB, SUB)], gsem.at[s])

    def scatter_row(tile, s, r):
        tk = tok_ref[tile * TE + r]
        return pltpu.make_async_copy(obuf.at[s, pl.ds(r * SUB, SUB)], f_hbm.at[pl.ds(tk * SUB, SUB)], ssem.at[s])

    def wait_gather(s):
        pltpu.make_async_copy(h_hbm.at[pl.ds(0, TE * SUB)], hbuf.at[s], gsem.at[s]).wait()

    def wait_scatter(s):
        pltpu.make_async_copy(obuf.at[s], f_hbm.at[pl.ds(0, TE * SUB)], ssem.at[s]).wait()

    def start_rows(make_row, tile, s):
        def body(i, carry):
            make_row(tile, s, 2 * i).start(priority=0)
            make_row(tile, s, 2 * i + 1).start(priority=1)
            return carry
        lax.fori_loop(0, TE // 2, body, 0, unroll=4)

    @pl.when(t == 0)
    def _():
        obuf[...] = jnp.zeros(obuf.shape, F32)
        for half in range(2):
            fill = pltpu.make_async_copy(obuf.at[0], f_hbm.at[pl.ds((n_tok + half * TE) * SUB, TE * SUB)], ssem.at[0])
            fill.start()
            fill.wait()
        start_rows(gather_row, 0, 0)
        start_rows(gather_row, 1, 1)

    @pl.when(live)
    def _():
        wait_gather(slot)

        @pl.when(t >= 2)
        def _():
            wait_scatter(slot)

    @pl.when(live)
    def _():
        g_tile = jnp.minimum(t + 2, n_tiles - 1)
        s_tile = jnp.where(t == 0, n_tiles - 1, t - 1)
        issue = ([functools.partial(gather_row, g_tile, slot2, r) for r in range(TE)]
                 + [functools.partial(scatter_row, s_tile, slot2, r) for r in range(TE)])
        n_stage = 6
        per_stage = -(-len(issue) // n_stage)

        def issue_stage(k):
            for i, make in enumerate(issue[k * per_stage:(k + 1) * per_stage]):
                make().start(priority=i % 2)

        h = _load_token_rows(hbuf.at[slot], TE).astype(BF16)
        w_lo = jnp.broadcast_to(wts_ref[0, 0:1, :], (LANES, TE)).T[:, :1]
        w_hi = jnp.broadcast_to(wts_ref[0, 1:2, :], (LANES, TE)).T[:, :1]
        outs = []
        stage = 0
        for wg_ref, wu_ref, wd_ref, wt in ((wg1_ref, wu1_ref, wd1_ref, w_lo), (wg2_ref, wu2_ref, wd2_ref, w_hi)):
            hg = _dot(h, wg_ref[0, 0])
            issue_stage(stage)
            hu = _dot(h, wu_ref[0, 0])
            issue_stage(stage + 1)
            act = (hg * jax.nn.sigmoid(hg) * hu).astype(BF16)
            outs.append(wt * _dot(act, wd_ref[0, 0]))
            issue_stage(stage + 2)
            stage += 3
        _store_token_rows(obuf.at[slot], outs[0] + outs[1])

    @pl.when(jnp.logical_and(jnp.logical_not(live), prev_live))
    def _():
        wait_gather(slot)
        wait_gather(slot1)
        wait_scatter(slot1)

        @pl.when(t >= 2)
        def _():
            wait_scatter(slot)
        start_rows(scatter_row, t - 1, slot2)
        wait_scatter(slot2)


def _moe_call(tok, e1, e2, nv, wts, h3, wg, wu, wd, *, layer, n_tok):
    d = wg.shape[2]
    de = wg.shape[3]
    n_tiles = e1.shape[0]

    def wspec(shape, which):
        return pl.BlockSpec((1, 1) + shape, lambda t, tok, e1, e2, nv: (layer, (e1, e2)[which][t], 0, 0))

    grid_spec = pltpu.PrefetchScalarGridSpec(
        num_scalar_prefetch=4,
        grid=(n_tiles,),
        in_specs=[pl.BlockSpec((1, 2, TE), lambda t, tok, e1, e2, nv: (t, 0, 0)),
                  pl.BlockSpec(memory_space=pl.ANY),
                  wspec((d, de), 0), wspec((d, de), 0), wspec((de, d), 0),
                  wspec((d, de), 1), wspec((d, de), 1), wspec((de, d), 1)],
        out_specs=pl.BlockSpec(memory_space=pl.ANY),
        scratch_shapes=[pltpu.VMEM((MOE_BUFS, TE * SUB, LANES), F32), pltpu.VMEM((MOE_BUFS, TE * SUB, LANES), F32),
                        pltpu.SemaphoreType.DMA((MOE_BUFS,)), pltpu.SemaphoreType.DMA((MOE_BUFS,))],
    )
    return pl.pallas_call(
        functools.partial(_moe_kernel, n_tok=n_tok, n_tiles=n_tiles),
        grid_spec=grid_spec,
        out_shape=jax.ShapeDtypeStruct(((n_tok + 2 * TE) * SUB, LANES), F32),
        compiler_params=_params(("arbitrary",)),
        name="moe",
    )(tok, e1, e2, nv, wts, h3, wg, wu, wd, wg, wu, wd)


def _route_tables(meta, n_tok):
    w_lo = meta[:, 0, :].reshape(-1)
    w_hi = meta[:, 1, :].reshape(-1)
    bucket = meta[:, 2, :].reshape(-1).astype(jnp.int32)
    n_slots = n_tok + N_BUCKETS * TE
    n_tiles = n_slots // TE
    buckets = jnp.arange(N_BUCKETS, dtype=jnp.int32)
    counts = jnp.sum(bucket[:, None] == buckets[None, :], axis=0, dtype=jnp.int32)
    padded = ((counts + TE - 1) // TE) * TE
    pad_end = jnp.cumsum(padded)
    pad_start = pad_end - padded
    fill_i = jnp.arange(TE, dtype=jnp.int32)[None, :]
    fill_key = jnp.where(fill_i < (padded - counts)[:, None], 2 * buckets[:, None] + 1, 2 * N_BUCKETS).reshape(-1)
    keys = jnp.concatenate([2 * bucket, fill_key])
    ids = jnp.concatenate([jnp.arange(n_tok, dtype=jnp.int32), jnp.full((N_BUCKETS * TE,), -1, jnp.int32)])
    zeros = jnp.zeros((N_BUCKETS * TE,), F32)
    _, ids, w_lo, w_hi = lax.sort((keys, ids, jnp.concatenate([w_lo, zeros]), jnp.concatenate([w_hi, zeros])),
                                  num_keys=1, is_stable=True)
    slot = jnp.arange(n_slots, dtype=jnp.int32)
    tok = jnp.where(ids < 0, n_tok + ((slot // TE) % 2) * TE + slot % TE, ids)
    wts = jnp.stack([w_lo.reshape(n_tiles, TE), w_hi.reshape(n_tiles, TE)], axis=1)

    tile_start = jnp.arange(n_tiles, dtype=jnp.int32) * TE
    tb = jnp.sum(tile_start[:, None] >= pad_end[None, :], axis=1, dtype=jnp.int32)
    used = tb < N_BUCKETS
    onehot = (tb[:, None] == buckets[None, :]).astype(jnp.int32)
    nv = jnp.clip(jnp.sum(onehot * (counts + pad_start)[None, :], axis=1) - tile_start, 0, TE) * used
    tbe = jnp.where(used, tb, jnp.max(jnp.where(used, tb, 0)))
    pid = tbe % len(PAIRS)
    pair_lo = (pid >= 3).astype(jnp.int32) + (pid >= 5).astype(jnp.int32)
    pair_hi = pid + 1 - 2 * (pid >= 3).astype(jnp.int32) - (pid >= 5).astype(jnp.int32)
    e1 = (tbe // len(PAIRS)) * EPG + pair_lo
    e2 = (tbe // len(PAIRS)) * EPG + pair_hi
    return tok.astype(jnp.int32), e1.astype(jnp.int32), e2.astype(jnp.int32), nv.astype(jnp.int32), wts


def _final_kernel(x_ref, f_ref, mod_ref, nw_ref, o_ref):
    x = x_ref[...] + mod_ref[0, 5:6, :] * _load_token_rows(f_ref, x_ref.shape[0])
    o_ref[...] = _rms(x, nw_ref[...])


def _final_call(x, f, mod, nw, *, n_lat, seq):
    d = x.shape[1]
    tm = TM_MERGE
    tps = seq // tm
    return pl.pallas_call(
        _final_kernel,
        grid=(n_lat // tm,),
        in_specs=[pl.BlockSpec((tm, d), lambda i: (i, 0)), pl.BlockSpec((tm * SUB, LANES), lambda i: (i, 0)),
                  pl.BlockSpec((1, 8, d), lambda i: (i // tps, 0, 0)), pl.BlockSpec((1, d), lambda i: (0, 0))],
        out_specs=pl.BlockSpec((tm, d), lambda i: (i, 0)),
        out_shape=jax.ShapeDtypeStruct((n_lat, d), F32),
        compiler_params=_params(("arbitrary",)),
        name="final",
    )(x, f, mod, nw)


def _rope_table(pos_r, pos_c, dim, lane_of):
    d = dim // 2
    half = d // 2
    lane = np.arange(LANES)
    rl = lane_of(lane)
    is_rope = rl >= 0
    rl = np.maximum(rl, 0)
    use_col = rl >= d
    j = rl % half
    first = (rl % d) < half
    inv = (np.float32(ROPE_BASE) ** (-(2.0 * j).astype(np.float32) / np.float32(d))).astype(np.float32)
    pos = np.where(use_col[None, :], pos_c[:, None], pos_r[:, None]).astype(np.float32)
    ang = pos * inv[None, :]
    cos = np.where(is_rope[None, :], np.cos(ang), 1.0)
    sin = np.where(is_rope[None, :], np.sin(ang), 0.0)
    return np.stack([cos, np.where(first[None, :], -sin, 0.0), np.where(first[None, :], 0.0, sin)]).astype(np.float32)


def _tables(seq):
    t = np.arange(seq)
    rows, cols = t // GRID_W, t % GRID_W
    tab_a = _rope_table(rows, cols, A_HEAD_DIM, lambda lane: lane % A_HEAD_DIM)
    tab_b = _rope_table(rows, cols, B_ROPE,
                        lambda lane: np.where((lane >= B_NOPE) & (lane < B_NOPE + B_ROPE), lane - B_NOPE, -1))
    ident = np.stack([np.ones((TM, LANES), np.float32), np.zeros((TM, LANES), np.float32),
                      np.zeros((TM, LANES), np.float32)])
    return jnp.asarray(np.concatenate([tab_a, ident], axis=1)), jnp.asarray(np.concatenate([tab_b, ident], axis=1))


def _window_band():
    r = np.arange(BLOCK)[:, None]
    c = np.arange(3 * BLOCK)[None, :]
    dist = c - r
    return jnp.asarray(np.where((dist >= BLOCK - WINDOW) & (dist <= BLOCK + WINDOW), 0.0, NEG).astype(np.float32))


def _layer_weights(w_in, w_uq, w_ukv, q_norm, kv_norm, w_o_a, w_o_b, w_out):
    d = w_in.shape[0]
    o = 0
    qa_w = w_in[:, o:o + 512]; o += 512
    ka_w = w_in[:, o:o + 128]; o += 128
    va_w = w_in[:, o:o + 128]; o += 128
    cq_w = w_in[:, o:o + B_Q_RANK]; o += B_Q_RANK
    ckv_w = w_in[:, o:o + B_KV_RANK]; o += B_KV_RANK
    kr_w = w_in[:, o:o + B_ROPE]; o += B_ROPE
    g_w = w_in[:, o:]

    def dup(wk):
        wk = wk.reshape(d, A_KV_HEADS, 1, A_HEAD_DIM)
        return jnp.broadcast_to(wk, (d, A_KV_HEADS, 2, A_HEAD_DIM)).reshape(d, A_KV_HEADS * 2 * A_HEAD_DIM)

    w1 = jnp.concatenate([qa_w, dup(ka_w), dup(va_w), cq_w, ckv_w, jnp.pad(kr_w, ((0, 0), (0, LANES - B_ROPE)))], axis=1)
    wuq = jnp.pad(w_uq.reshape(B_Q_RANK, B_HEADS, B_NOPE + B_ROPE),
                  ((0, 0), (0, 0), (0, LANES - B_NOPE - B_ROPE))).reshape(B_Q_RANK, B_HEADS * LANES)
    ukv = w_ukv.reshape(B_KV_RANK, B_HEADS, B_NOPE + B_V)
    wukn = jnp.pad(ukv[:, :, :B_NOPE], ((0, 0), (0, 0), (0, LANES - B_NOPE))).reshape(B_KV_RANK, B_HEADS * LANES)
    wuv = ukv[:, :, B_NOPE:].reshape(B_KV_RANK, B_HEADS * B_V)
    src = jnp.arange(LANES)[:, None]
    dst = jnp.arange(B_HEADS * LANES)[None, :]
    rp = ((src < B_ROPE) & (dst % LANES == src + B_NOPE)).astype(BF16)
    return dict(w1=w1.astype(BF16), wg=g_w.astype(BF16), qn=q_norm.reshape(1, -1), wuq=wuq.astype(BF16),
                kvn=kv_norm.reshape(1, -1), wukn=wukn.astype(BF16), wuv=wuv.astype(BF16), rp=rp,
                woa=w_o_a.astype(BF16), wob=w_o_b.astype(BF16), wout=w_out.astype(BF16))


def kernel(x, c, ctx, c_ctx, w_mod, b_mod, norm_mix, norm_ffn, w_in, attn_sink, mla_q_norm, w_uq, mla_kv_norm, w_ukv,
           w_o_a, w_o_b, w_out, w_router, router_bias, w_expert_gate, w_expert_up, w_expert_down, final_norm):
    batch, seq, d = x.shape
    ctx_len = ctx.shape[1]
    depth = w_mod.shape[0]
    n_lat = batch * seq
    n_all = n_lat + batch * ctx_len
    for tile in (TM, TM_MERGE):
        assert seq % tile == 0 and (batch * ctx_len) % tile == 0
    assert d == SUB * LANES and seq % TQ == 0 and seq % GRID_W == 0 and batch + 1 <= MOD_ROWS

    c_all = jnp.concatenate([c, c_ctx[None, :], jnp.zeros((MOD_ROWS - batch - 1, d), F32)], axis=0)
    mod = _mod_call(c_all, w_mod, b_mod)
    mod = jnp.pad(mod.transpose(0, 2, 1, 3), ((0, 0), (0, 0), (0, 2), (0, 0)))

    tab_a, tab_b = _tables(seq)
    wr = jnp.pad(w_router.astype(F32), ((0, 0), (0, LANES - N_EXPERTS)))
    wr_hi = wr.astype(BF16)
    wr2 = jnp.concatenate([wr_hi, (wr - wr_hi.astype(F32)).astype(BF16)], axis=1)
    rb = router_bias.astype(F32).reshape(N_EXPERTS, 1)
    wg_all = w_expert_gate.astype(BF16)
    wu_all = w_expert_up.astype(BF16)
    wd_all = w_expert_down.astype(BF16)

    xs = jnp.concatenate([x.reshape(n_lat, d), ctx.reshape(batch * ctx_len, d)], axis=0)
    f = None
    for l in range(depth):
        need_ctx = l < depth - 1
        w = _layer_weights(w_in[l], w_uq[l], w_ukv[l], mla_q_norm[l], mla_kv_norm[l], w_o_a[l], w_o_b[l], w_out[l])
        outs = _proj_call(xs, f, mod[l - 1] if l else None, mod[l], norm_mix[l].reshape(1, d), tab_a, tab_b, w,
                          n_lat=n_lat, seq=seq)
        if l:
            xs, outs = outs[0], outs[1:]
        qa, ka, va, qb, kb, vb, ga, gb = outs
        ya = _window_call(attn_sink[l].astype(F32), qa, ka, va, batch=batch, seq=seq, ctx_len=ctx_len, need_ctx=need_ctx)
        yb = _mla_call(qb, kb, vb, batch=batch, seq=seq, ctx_len=ctx_len, latent=True)
        yb_ctx = _mla_call(qb, kb, vb, batch=batch, seq=seq, ctx_len=ctx_len, latent=False) if need_ctx else None
        n_rows = n_all if need_ctx else n_lat
        xs, h3, meta = _merge_call(xs, ya, yb, yb_ctx, ga, gb, mod[l], norm_ffn[l].reshape(1, d), w, wr2, rb,
                                   n_rows=n_rows, n_lat=n_lat, seq=seq)
        tok, e1, e2, nv, wts = _route_tables(meta, n_rows)
        f = _moe_call(tok, e1, e2, nv, wts, h3, wg_all, wu_all, wd_all, layer=l, n_tok=n_rows)
    out = _final_call(xs, f, mod[depth - 1], final_norm.reshape(1, d), n_lat=n_lat, seq=seq)
    return out.reshape(batch, seq, d)
```

```python
import functools

import numpy as np
import jax
import jax.numpy as jnp
from jax import lax
from jax.experimental import pallas as pl
from jax.experimental.pallas import tpu as pltpu

F32 = jnp.float32
BF16 = jnp.bfloat16

EPS = 1e-6
ROPE_BASE = 10000.0
GRID_W = 64
BLOCK = 128
A_HEADS, A_KV_HEADS, A_HEAD_DIM = 8, 2, 64
A_GROUP = A_HEADS // A_KV_HEADS
WINDOW = 128
B_HEADS, B_NOPE, B_ROPE, B_V = 8, 64, 32, 64
B_Q_RANK, B_KV_RANK = 256, 128
N_EXPERTS, N_GROUPS = 16, 4
EPG = N_EXPERTS // N_GROUPS
D_EXPERT = 512
PAIRS = ((0, 1), (0, 2), (0, 3), (1, 2), (1, 3), (2, 3))
N_BUCKETS = N_GROUPS * len(PAIRS)

LANES = 128
SUB = 8
TM = 512
TM_MERGE = 512
TQ = 512
MLA_PAIRS = 4
TE = 256
MOE_BUFS = 3
MOD_ROWS = 24
NEG = -1e30
LOG2E = 1.4426950408889634
VMEM_LIMIT = 56 * 1024 * 1024

W1_QA, W1_KA, W1_VA, W1_CQ, W1_CKV, W1_KR = 0, 512, 768, 1024, 1280, 1408
W1_COLS = 1536


def _params(sem, vmem=VMEM_LIMIT):
    return pltpu.CompilerParams(dimension_semantics=sem, vmem_limit_bytes=vmem)


def _dot(a, b):
    return jnp.dot(a, b, preferred_element_type=F32)


def _dot_nt(a, b):
    return lax.dot_general(a, b, (((1,), (1,)), ((), ())), preferred_element_type=F32)


def _rms(x, g):
    return x * lax.rsqrt(jnp.mean(x * x, axis=-1, keepdims=True) + EPS) * g


def _lane_tile(t, width):
    return jnp.concatenate([t] * (width // LANES), axis=1)


def _rope(t, tab_ref, shift):
    w = t.shape[-1]
    up = pltpu.roll(t, w - shift, 1)
    dn = pltpu.roll(t, shift, 1)
    return (t * _lane_tile(tab_ref[0], w) + up * _lane_tile(tab_ref[1], w)
            + dn * _lane_tile(tab_ref[2], w))


def _load_token_rows(ref, rows):
    return jnp.concatenate([ref[pl.ds(s, rows, stride=SUB), :] for s in range(SUB)], axis=1)


def _store_token_rows(ref, val):
    rows = val.shape[0]
    for s in range(SUB):
        ref[pl.ds(s, rows, stride=SUB), :] = val[:, s * LANES:(s + 1) * LANES]


def _mod_kernel(c_ref, w_ref, b_ref, o_ref):
    c = c_ref[...]
    a = (c * jax.nn.sigmoid(c)).astype(BF16)
    o_ref[0, 0] = _dot(a, w_ref[0].astype(BF16)) + b_ref[0, 0]


def _mod_call(c_all, w_mod, b_mod):
    depth, d, _ = w_mod.shape
    return pl.pallas_call(
        _mod_kernel,
        grid=(depth, 6),
        in_specs=[pl.BlockSpec((MOD_ROWS, d), lambda l, k: (0, 0)),
                  pl.BlockSpec((1, d, d), lambda l, k: (l, 0, k)),
                  pl.BlockSpec((1, 1, 1, d), lambda l, k: (l, k, 0, 0))],
        out_specs=pl.BlockSpec((1, 1, MOD_ROWS, d), lambda l, k: (l, k, 0, 0)),
        out_shape=jax.ShapeDtypeStruct((depth, 6, MOD_ROWS, d), F32),
        compiler_params=_params(("arbitrary", "arbitrary")),
        name="mod",
    )(c_all, w_mod, b_mod.reshape(depth, 6, 1, d))


def _proj_kernel(*refs, has_f, n_lat_tiles):
    if has_f:
        x_ref, f_ref, modp_ref, refs = refs[0], refs[1], refs[2], refs[3:]
    else:
        x_ref, xc_ref, refs = refs[0], refs[1], refs[2:]
    (mod_ref, nw_ref, ta_ref, tb_ref, w1_ref, wg_ref, qn_ref, wuq_ref, kvn_ref, wukn_ref, wuv_ref,
     rp_ref), refs = refs[:12], refs[12:]
    xo_ref, qa_ref, ka_ref, va_ref, qb_ref, kb_ref, vb_ref, ga_ref, gb_ref = refs

    if has_f:
        x = x_ref[...] + modp_ref[0, 5:6, :] * _load_token_rows(f_ref, x_ref.shape[0])
    else:
        x = jnp.where(pl.program_id(0) < n_lat_tiles, x_ref[...], xc_ref[...])
    xo_ref[...] = x
    h = (_rms(x, nw_ref[...]) * (1.0 + mod_ref[0, 1:2, :]) + mod_ref[0, 0:1, :]).astype(BF16)

    t = _dot(h, w1_ref[...])
    qa_ref[...] = (_rope(t[:, W1_QA:W1_KA], ta_ref, 16) * (A_HEAD_DIM ** -0.5 * LOG2E)).astype(BF16)
    ka_ref[...] = _rope(t[:, W1_KA:W1_VA], ta_ref, 16).astype(BF16)
    va_ref[...] = t[:, W1_VA:W1_CQ].astype(BF16)

    cq = _rms(t[:, W1_CQ:W1_CKV], qn_ref[...]).astype(BF16)
    qb = _rope(_dot(cq, wuq_ref[...]), tb_ref, 8)
    qb_ref[...] = (qb * ((B_NOPE + B_ROPE) ** -0.5 * LOG2E)).astype(BF16)

    ckv = _rms(t[:, W1_CKV:W1_KR], kvn_ref[...]).astype(BF16)
    vb_ref[...] = _dot(ckv, wuv_ref[...]).astype(BF16)
    kr = t[:, W1_KR:W1_COLS]
    kr_hi = kr.astype(BF16)
    kr_lo = (kr - kr_hi.astype(F32)).astype(BF16)
    kb = _dot(ckv, wukn_ref[...]) + _dot(kr_hi, rp_ref[...]) + _dot(kr_lo, rp_ref[...])
    kb_ref[...] = _rope(kb, tb_ref, 8).astype(BF16)

    d = ga_ref.shape[-1]
    ga_ref[...] = jax.nn.sigmoid(_dot(h, wg_ref[:, :d])).astype(BF16)
    gb_ref[...] = jax.nn.sigmoid(_dot(h, wg_ref[:, d:])).astype(BF16)


def _proj_call(x, f, modp, mod, nw, tab_a, tab_b, w, *, n_lat, seq):
    has_f = f is not None
    n = x.shape[0] if has_f else x[0].shape[0] + x[1].shape[0]
    d = mod.shape[-1]
    nt = n // TM
    n_lat_t = n_lat // TM
    tps = seq // TM
    ctx_row = n_lat // seq

    def tok(i):
        return (i, 0)

    def modi(i):
        return (jnp.where(i < n_lat_t, i // tps, ctx_row), 0, 0)

    def tabi(i):
        return (0, jnp.where(i < n_lat_t, i % tps, tps), 0)

    def full(a):
        return pl.BlockSpec(a.shape, lambda i: (0,) * a.ndim, pipeline_mode=pl.Buffered(1))

    mod_spec = pl.BlockSpec((1, 8, d), modi)
    if has_f:
        in_specs = [pl.BlockSpec((TM, d), tok), pl.BlockSpec((TM * SUB, LANES), tok), mod_spec]
        args = [x, f, modp]
    else:
        in_specs = [pl.BlockSpec((TM, d), lambda i: (jnp.minimum(i, n_lat_t - 1), 0)),
                    pl.BlockSpec((TM, d), lambda i: (jnp.maximum(i - n_lat_t, 0), 0))]
        args = list(x)
    in_specs += [mod_spec, full(nw), pl.BlockSpec((3, TM, LANES), tabi), pl.BlockSpec((3, TM, LANES), tabi)]
    args += [mod, nw, tab_a, tab_b]
    for k in ("w1", "wg", "qn", "wuq", "kvn", "wukn", "wuv", "rp"):
        in_specs.append(full(w[k]))
        args.append(w[k])

    widths = (512, 256, 256, 1024, 1024, 512, d, d)
    out_shape = [jax.ShapeDtypeStruct((n, d), F32)] + [jax.ShapeDtypeStruct((n, wd), BF16) for wd in widths]
    out_specs = [pl.BlockSpec((TM, d), tok)] + [pl.BlockSpec((TM, wd), tok) for wd in widths]
    return pl.pallas_call(
        functools.partial(_proj_kernel, has_f=has_f, n_lat_tiles=n_lat_t),
        grid=(nt,), in_specs=in_specs, out_specs=out_specs, out_shape=out_shape,
        compiler_params=_params(("arbitrary",)),
        name="proj",
    )(*args)


def _window_kernel(sink_ref, band_ref, q_ref, kp_ref, kc_ref, kn_ref, kx_ref, vp_ref, vc_ref, vn_ref, vx_ref, o_ref,
                   *, nb):
    rows = A_GROUP * BLOCK
    lo = lax.broadcasted_iota(jnp.int32, (BLOCK, LANES), 1) < A_HEAD_DIM
    col_blk = lax.broadcasted_iota(jnp.int32, (1, 3 * BLOCK), 1) // BLOCK
    row_head = lax.broadcasted_iota(jnp.int32, (rows, 1), 0) // BLOCK
    zero = jnp.zeros((BLOCK, LANES), BF16)
    first, second = slice(0, BLOCK), slice(BLOCK, 2 * BLOCK)
    for sub in range(2):
        n = 2 * pl.program_id(1) + sub
        qrows = (first, second)[sub]
        off_prev = jnp.where(jnp.logical_or(n == 0, n >= nb), NEG, 0.0)
        off_cur = jnp.where(n >= nb, NEG, 0.0)
        off_next = jnp.where(n >= nb - 1, NEG, 0.0)
        bias = band_ref[...] + jnp.where(col_blk == 0, off_prev, jnp.where(col_blk == 1, off_cur, off_next))
        bias = jnp.concatenate([bias] * A_GROUP, axis=0)
        for k in range(A_KV_HEADS):
            ksl = slice(k * LANES, (k + 1) * LANES)
            parts = []
            for j in range(A_GROUP // 2):
                blk = q_ref[qrows, k * 2 * LANES + j * LANES:k * 2 * LANES + (j + 1) * LANES]
                parts += [jnp.where(lo, blk, zero), jnp.where(lo, zero, blk)]
            qs = jnp.concatenate(parts, axis=0)
            if sub == 0:
                k_lat = jnp.concatenate([kp_ref[:, ksl], kc_ref[:, ksl]], axis=0)
                v_lat = jnp.concatenate([vp_ref[:, ksl], vc_ref[:, ksl]], axis=0)
            else:
                k_lat = jnp.concatenate([kc_ref[:, ksl], kn_ref[:, ksl]], axis=0)
                v_lat = jnp.concatenate([vc_ref[:, ksl], vn_ref[:, ksl]], axis=0)
            s_l = _dot_nt(qs, k_lat) + bias
            s_x = _dot_nt(qs, kx_ref[:, ksl])
            sink = jnp.zeros((rows, 1), F32)
            for g in range(A_GROUP):
                sink = jnp.where(row_head == g, sink_ref[k * A_GROUP + g] * LOG2E, sink)
            m = jnp.maximum(jnp.maximum(jnp.max(s_l, axis=-1, keepdims=True), jnp.max(s_x, axis=-1, keepdims=True)),
                            sink)
            e_l = jnp.exp2(s_l - m)
            e_x = jnp.exp2(s_x - m)
            den = jnp.sum(e_l, axis=-1, keepdims=True) + jnp.sum(e_x, axis=-1, keepdims=True) + jnp.exp2(sink - m)
            o = (_dot(e_l.astype(BF16), v_lat) + _dot(e_x.astype(BF16), vx_ref[:, ksl])) / den
            for j in range(A_GROUP // 2):
                even = o[(2 * j) * BLOCK:(2 * j + 1) * BLOCK]
                odd = o[(2 * j + 1) * BLOCK:(2 * j + 2) * BLOCK]
                o_ref[qrows, k * 2 * LANES + j * LANES:k * 2 * LANES + (j + 1) * LANES] = (
                    jnp.where(lo, even, odd).astype(BF16))


def _window_call(sink, qa, ka, va, *, batch, seq, ctx_len, need_ctx):
    n = qa.shape[0] if need_ctx else batch * seq
    nb = seq // BLOCK
    ncb = ctx_len // BLOCK
    assert nb % 2 == 0 and ncb % 2 == 0
    nb2, ncb2 = nb // 2, ncb // 2
    nq = nb2 + (ncb2 if need_ctx else 0)

    def qi(b, i):
        return (jnp.where(i < nb2, b * nb2 + i, batch * nb2 + b * ncb2 + (i - nb2)), 0)

    def ki(off):
        return lambda b, i: (b * nb + jnp.clip(2 * i + off, 0, nb - 1), 0)

    def kc(b, i):
        return (b * nb2 + jnp.minimum(i, nb2 - 1), 0)

    def xi(b, i):
        return (batch * seq // ctx_len + b, 0)

    kv_specs = [pl.BlockSpec((BLOCK, 2 * LANES), ki(-1)), pl.BlockSpec((2 * BLOCK, 2 * LANES), kc),
                pl.BlockSpec((BLOCK, 2 * LANES), ki(2)), pl.BlockSpec((ctx_len, 2 * LANES), xi)]
    return pl.pallas_call(
        functools.partial(_window_kernel, nb=nb),
        grid=(batch, nq),
        in_specs=[pl.BlockSpec(memory_space=pltpu.SMEM), pl.BlockSpec((BLOCK, 3 * BLOCK), lambda b, i: (0, 0)),
                  pl.BlockSpec((2 * BLOCK, 4 * LANES), qi)] + kv_specs + kv_specs,
        out_specs=pl.BlockSpec((2 * BLOCK, 4 * LANES), qi),
        out_shape=jax.ShapeDtypeStruct((n, 4 * LANES), BF16),
        compiler_params=_params(("arbitrary", "arbitrary")),
        name="window",
    )(sink, _window_band(), qa, ka, ka, ka, ka, va, va, va, va)


def _mla_kernel(q_ref, *refs, with_lat):
    if with_lat:
        kl_ref, kx_ref, vl_ref, vx_ref, o_ref, vaug_ref = refs
    else:
        kx_ref, vx_ref, o_ref, vaug_ref = refs
    n_ctx = vx_ref.shape[0]
    n_pairs = o_ref.shape[1] // LANES
    lo = lax.broadcasted_iota(jnp.int32, (o_ref.shape[0], LANES), 1) < B_V

    @pl.when(pl.program_id(2) == 0)
    def _():
        one = jnp.ones((1, LANES), BF16)
        for v_ref, start in ((vx_ref, 0),) + (((vl_ref, n_ctx),) if with_lat else ()):
            n = v_ref.shape[0]
            keep = lax.broadcasted_iota(jnp.int32, (n, LANES), 1) < B_V
            for pp in range(n_pairs):
                v = v_ref[:, pp * LANES:(pp + 1) * LANES]
                vaug_ref[2 * pp, start:start + n, :] = jnp.where(keep, v, one)
                vaug_ref[2 * pp + 1, start:start + n, :] = jnp.where(keep, one, v)

    def head(h):
        sl = slice(h * LANES, (h + 1) * LANES)
        q = q_ref[:, sl]
        s_x = _dot_nt(q, kx_ref[:, sl])
        m = jnp.max(s_x, axis=-1, keepdims=True)
        if with_lat:
            s_l = _dot_nt(q, kl_ref[:, sl])
            m = jnp.maximum(m, jnp.max(s_l, axis=-1, keepdims=True))
        o = _dot(jnp.exp2(s_x - m).astype(BF16), vaug_ref[h, :n_ctx, :])
        if with_lat:
            o = o + _dot(jnp.exp2(s_l - m).astype(BF16), vaug_ref[h, n_ctx:, :])
        return o / pltpu.roll(o, B_V, 1)

    for pp in range(n_pairs):
        o_ref[:, pp * LANES:(pp + 1) * LANES] = jnp.where(lo, head(2 * pp), head(2 * pp + 1)).astype(BF16)


def _mla_call(qb, kb, vb, *, batch, seq, ctx_len, latent):
    ctx0 = batch * seq // ctx_len
    groups = B_HEADS // (2 * MLA_PAIRS)
    qk_w = MLA_PAIRS * 2 * LANES
    v_w = MLA_PAIRS * LANES
    tq = TQ if latent else ctx_len
    nq = seq // TQ if latent else 1
    kx_spec = pl.BlockSpec((ctx_len, qk_w), lambda b, p, j: (ctx0 + b, p))
    vx_spec = pl.BlockSpec((ctx_len, v_w), lambda b, p, j: (ctx0 + b, p))
    if latent:
        in_specs = [pl.BlockSpec((tq, qk_w), lambda b, p, j: (b * nq + j, p)),
                    pl.BlockSpec((seq, qk_w), lambda b, p, j: (b, p)), kx_spec,
                    pl.BlockSpec((seq, v_w), lambda b, p, j: (b, p)), vx_spec]
        args = (qb, kb, kb, vb, vb)
    else:
        in_specs = [pl.BlockSpec((tq, qk_w), lambda b, p, j: (ctx0 + b, p)), kx_spec, vx_spec]
        args = (qb, kb, vb)
    return pl.pallas_call(
        functools.partial(_mla_kernel, with_lat=latent),
        grid=(batch, groups, nq),
        in_specs=in_specs,
        out_specs=pl.BlockSpec((tq, v_w), lambda b, p, j: (b * nq + j, p)),
        out_shape=jax.ShapeDtypeStruct((batch * nq * tq, groups * v_w), BF16),
        scratch_shapes=[pltpu.VMEM((2 * MLA_PAIRS, ctx_len + (seq if latent else 0), LANES), BF16)],
        compiler_params=_params(("arbitrary", "arbitrary", "arbitrary")),
        name="mla" if latent else "mla_ctx",
    )(*args)


def _top2(v):
    i1 = jnp.zeros_like(v[0])
    m1 = v[0]
    for i in range(1, EPG):
        u = v[i] > m1
        i1 = jnp.where(u, float(i), i1)
        m1 = jnp.where(u, v[i], m1)
    i2 = jnp.zeros_like(v[0])
    m2 = jnp.full_like(v[0], -jnp.inf)
    for i in range(EPG):
        cand = jnp.where(i1 == float(i), -jnp.inf, v[i])
        u = cand > m2
        i2 = jnp.where(u, float(i), i2)
        m2 = jnp.where(u, cand, m2)
    return i1, i2, m1, m2


def _pick(idx, vals):
    out = vals[0]
    for i in range(1, len(vals)):
        out = jnp.where(idx == float(i), vals[i], out)
    return out


def _merge_kernel(x_ref, ya_ref, yb_ref, ybx_ref, ga_ref, gb_ref, mod_ref, nw_ref, woa_ref, wob_ref, wout_ref, wr2_ref,
                  rb_ref, xo_ref, h_ref, meta_ref, *, n_lat_tiles):
    yb = yb_ref[...] if ybx_ref is None else jnp.where(pl.program_id(0) < n_lat_tiles, yb_ref[...], ybx_ref[...])
    a = _dot(ya_ref[...], woa_ref[...])
    b = _dot(yb, wob_ref[...])
    mix = (ga_ref[...].astype(F32) * a + gb_ref[...].astype(F32) * b).astype(BF16)
    x = x_ref[...] + mod_ref[0, 2:3, :] * _dot(mix, wout_ref[...])
    xo_ref[...] = x
    h = _rms(x, nw_ref[...]) * (1.0 + mod_ref[0, 4:5, :]) + mod_ref[0, 3:4, :]
    _store_token_rows(h_ref, h)

    h_hi = h.astype(BF16)
    h_lo = (h - h_hi.astype(F32)).astype(BF16)
    l_hi = _dot(h_hi, wr2_ref[...])
    logits = l_hi[:, :LANES] + l_hi[:, LANES:] + _dot(h_lo, wr2_ref[:, :LANES])
    sc = jax.nn.sigmoid(logits.T[:N_EXPERTS, :])
    sel = sc + rb_ref[...]
    sel_rows = [sel[e:e + 1, :] for e in range(N_EXPERTS)]
    sc_rows = [sc[e:e + 1, :] for e in range(N_EXPERTS)]
    best = jnp.zeros_like(sel_rows[0])
    best_v = None
    for g in range(N_GROUPS):
        _, _, m1, m2 = _top2(sel_rows[g * EPG:(g + 1) * EPG])
        gv = m1 + m2
        if best_v is None:
            best_v = gv
        else:
            u = gv > best_v
            best = jnp.where(u, float(g), best)
            best_v = jnp.where(u, gv, best_v)
    sel_g = [_pick(best, [sel_rows[g * EPG + i] for g in range(N_GROUPS)]) for i in range(EPG)]
    sc_g = [_pick(best, [sc_rows[g * EPG + i] for g in range(N_GROUPS)]) for i in range(EPG)]
    i1, i2, _, _ = _top2(sel_g)
    s1 = _pick(i1, sc_g)
    s2 = _pick(i2, sc_g)
    tot = s1 + s2
    first_low = i1 < i2
    e_lo = jnp.where(first_low, i1, i2)
    e_hi = jnp.where(first_low, i2, i1)
    w_lo = jnp.where(first_low, s1, s2) / tot
    w_hi = jnp.where(first_low, s2, s1) / tot
    pid = jnp.where(e_lo == 0.0, e_hi - 1.0, jnp.where(e_lo == 1.0, e_hi + 1.0, 5.0))
    bucket = best * float(len(PAIRS)) + pid
    t = sel.shape[1]
    meta_ref[0] = jnp.concatenate([w_lo, w_hi, bucket, jnp.zeros((SUB - 3, t), F32)], axis=0)


def _merge_call(x, ya, yb, yb_ctx, ga, gb, mod, nw, w, wr2, rb, *, n_rows, n_lat, seq):
    d = x.shape[1]
    tm = TM_MERGE
    nt = n_rows // tm
    n_lat_t = n_lat // tm
    tps = seq // tm
    ctx_row = n_lat // seq

    def tok(i):
        return (i, 0)

    def full(a):
        return pl.BlockSpec(a.shape, lambda i: (0,) * a.ndim)

    yb_specs = [pl.BlockSpec((tm, 512), lambda i: (jnp.minimum(i, n_lat_t - 1), 0))]
    yb_args = [yb]
    if yb_ctx is not None:
        yb_specs.append(pl.BlockSpec((tm, 512), lambda i: (jnp.maximum(i - n_lat_t, 0), 0)))
        yb_args.append(yb_ctx)

    def body(x_ref, ya_ref, *refs):
        if yb_ctx is None:
            refs = refs[:1] + (None,) + refs[1:]
        _merge_kernel(x_ref, ya_ref, *refs, n_lat_tiles=n_lat_t)

    return pl.pallas_call(
        body,
        grid=(nt,),
        in_specs=[pl.BlockSpec((tm, d), tok), pl.BlockSpec((tm, 512), tok)] + yb_specs + [
                  pl.BlockSpec((tm, d), tok), pl.BlockSpec((tm, d), tok),
                  pl.BlockSpec((1, 8, d), lambda i: (jnp.where(i < n_lat_t, i // tps, ctx_row), 0, 0)),
                  full(nw), full(w["woa"]), full(w["wob"]), full(w["wout"]), full(wr2), full(rb)],
        out_specs=[pl.BlockSpec((tm, d), tok), pl.BlockSpec((tm * SUB, LANES), tok),
                   pl.BlockSpec((1, SUB, tm), lambda i: (i, 0, 0))],
        out_shape=[jax.ShapeDtypeStruct((n_rows, d), F32), jax.ShapeDtypeStruct((n_rows * SUB, LANES), F32),
                   jax.ShapeDtypeStruct((nt, SUB, tm), F32)],
        compiler_params=_params(("arbitrary",)),
        name="merge",
    )(x, ya, *yb_args, ga, gb, mod, nw, w["woa"], w["wob"], w["wout"], wr2, rb)


def _moe_kernel(tok_ref, e1_ref, e2_ref, nv_ref, wts_ref, h_hbm, wg1_ref, wu1_ref, wd1_ref, wg2_ref, wu2_ref, wd2_ref,
                f_hbm, hbuf, obuf, wgu_ref, wdn_ref, gsem, ssem, *, n_tok, n_tiles):
    t = pl.program_id(0)
    slot = t % MOE_BUFS
    slot1 = (t + 1) % MOE_BUFS
    slot2 = (t + 2) % MOE_BUFS
    live = nv_ref[t] > 0
    prev_live = jnp.logical_and(t >= 1, nv_ref[jnp.maximum(t - 1, 0)] > 0)

    t_prev = jnp.maximum(t - 1, 0)
    new_pair = jnp.logical_or(t == 0, jnp.logical_or(e1_ref[t] != e1_ref[t_prev], e2_ref[t] != e2_ref[t_prev]))

    @pl.when(jnp.logical_and(live, new_pair))
    def _():
        for i, w_ref in enumerate((wg1_ref, wu1_ref, wg2_ref, wu2_ref)):
            wgu_ref[i] = w_ref[0, 0].astype(BF16)
        for i, w_ref in enumerate((wd1_ref, wd2_ref)):
            wdn_ref[i] = w_ref[0, 0].astype(BF16)

    def gather_row(tile, s, r):
        tk = jnp.minimum(tok_ref[tile * TE + r], n_tok - 1)
        return pltpu.make_async_copy(h_hbm.at[pl.ds(tk * SUB, SUB)], hbuf.at[s, pl.ds(r * SUB, SUB)], gsem.at[s])

    def scatter_row(tile, s, r):
        tk = tok_ref[tile * TE + r]
        return pltpu.make_async_copy(obuf.at[s, pl.ds(r * SUB, SUB)], f_hbm.at[pl.ds(tk * SUB, SUB)], ssem.at[s])

    def wait_gather(s):
        pltpu.make_async_copy(h_hbm.at[pl.ds(0, TE * SUB)], hbuf.at[s], gsem.at[s]).wait()

    def wait_scatter(s):
        pltpu.make_async_copy(obuf.at[s], f_hbm.at[pl.ds(0, TE * SUB)], ssem.at[s]).wait()

    def start_rows(make_row, tile, s):
        def body(i, carry):
            make_row(tile, s, 2 * i).start(priority=0)
            make_row(tile, s, 2 * i + 1).start(priority=1)
            return carry
        lax.fori_loop(0, TE // 2, body, 0, unroll=4)

    @pl.when(t == 0)
    def _():
        obuf[...] = jnp.zeros(obuf.shape, F32)
        for half in range(2):
            fill = pltpu.make_async_copy(obuf.at[0], f_hbm.at[pl.ds((n_tok + half * TE) * SUB, TE * SUB)], ssem.at[0])
            fill.start()
            fill.wait()
        start_rows(gather_row, 0, 0)
        start_rows(gather_row, 1, 1)

    @pl.when(live)
    def _():
        wait_gather(slot)

        @pl.when(t >= 2)
        def _():
            wait_scatter(slot)

    @pl.when(live)
    def _():
        g_tile = jnp.minimum(t + 2, n_tiles - 1)
        s_tile = jnp.where(t == 0, n_tiles - 1, t - 1)
        issue = ([functools.partial(gather_row, g_tile, slot2, r) for r in range(TE)]
                 + [functools.partial(scatter_row, s_tile, slot2, r) for r in range(TE)])
        n_stage = 6
        per_stage = -(-len(issue) // n_stage)

        def issue_stage(k):
            for i, make in enumerate(issue[k * per_stage:(k + 1) * per_stage]):
                make().start(priority=i % 2)

        h = _load_token_rows(hbuf.at[slot], TE).astype(BF16)
        w_lo = jnp.broadcast_to(wts_ref[0, 0:1, :], (LANES, TE)).T[:, :1]
        w_hi = jnp.broadcast_to(wts_ref[0, 1:2, :], (LANES, TE)).T[:, :1]
        outs = []
        stage = 0
        for which, wt in ((0, w_lo), (1, w_hi)):
            hg = _dot(h, wgu_ref[2 * which])
            issue_stage(stage)
            hu = _dot(h, wgu_ref[2 * which + 1])
            issue_stage(stage + 1)
            act = (hg * jax.nn.sigmoid(hg) * hu).astype(BF16)
            outs.append(wt * _dot(act, wdn_ref[which]))
            issue_stage(stage + 2)
            stage += 3
        _store_token_rows(obuf.at[slot], outs[0] + outs[1])

    @pl.when(jnp.logical_and(jnp.logical_not(live), prev_live))
    def _():
        wait_gather(slot)
        wait_gather(slot1)
        wait_scatter(slot1)

        @pl.when(t >= 2)
        def _():
            wait_scatter(slot)
        start_rows(scatter_row, t - 1, slot2)
        wait_scatter(slot2)


def _moe_call(tok, e1, e2, nv, wts, h3, wg, wu, wd, *, layer, n_tok):
    d = wg.shape[2]
    de = wg.shape[3]
    n_tiles = e1.shape[0]

    def wspec(shape, which):
        return pl.BlockSpec((1, 1) + shape, lambda t, tok, e1, e2, nv: (layer, (e1, e2)[which][t], 0, 0))

    grid_spec = pltpu.PrefetchScalarGridSpec(
        num_scalar_prefetch=4,
        grid=(n_tiles,),
        in_specs=[pl.BlockSpec((1, 2, TE), lambda t, tok, e1, e2, nv: (t, 0, 0)),
                  pl.BlockSpec(memory_space=pl.ANY),
                  wspec((d, de), 0), wspec((d, de), 0), wspec((de, d), 0),
                  wspec((d, de), 1), wspec((d, de), 1), wspec((de, d), 1)],
        out_specs=pl.BlockSpec(memory_space=pl.ANY),
        scratch_shapes=[pltpu.VMEM((MOE_BUFS, TE * SUB, LANES), F32), pltpu.VMEM((MOE_BUFS, TE * SUB, LANES), F32),
                        pltpu.VMEM((4, d, de), BF16), pltpu.VMEM((2, de, d), BF16),
                        pltpu.SemaphoreType.DMA((MOE_BUFS,)), pltpu.SemaphoreType.DMA((MOE_BUFS,))],
    )
    return pl.pallas_call(
        functools.partial(_moe_kernel, n_tok=n_tok, n_tiles=n_tiles),
        grid_spec=grid_spec,
        out_shape=jax.ShapeDtypeStruct(((n_tok + 2 * TE) * SUB, LANES), F32),
        compiler_params=_params(("arbitrary",)),
        name="moe",
    )(tok, e1, e2, nv, wts, h3, wg, wu, wd, wg, wu, wd)


def _route_tables(meta, n_tok):
    w_lo = meta[:, 0, :].reshape(-1)
    w_hi = meta[:, 1, :].reshape(-1)
    bucket = meta[:, 2, :].reshape(-1).astype(jnp.int32)
    n_slots = n_tok + N_BUCKETS * TE
    n_tiles = n_slots // TE
    buckets = jnp.arange(N_BUCKETS, dtype=jnp.int32)
    counts = jnp.sum(bucket[:, None] == buckets[None, :], axis=0, dtype=jnp.int32)
    padded = ((counts + TE - 1) // TE) * TE
    pad_end = jnp.cumsum(padded)
    pad_start = pad_end - padded
    fill_i = jnp.arange(TE, dtype=jnp.int32)[None, :]
    fill_key = jnp.where(fill_i < (padded - counts)[:, None], 2 * buckets[:, None] + 1, 2 * N_BUCKETS).reshape(-1)
    keys = jnp.concatenate([2 * bucket, fill_key])
    ids = jnp.concatenate([jnp.arange(n_tok, dtype=jnp.int32), jnp.full((N_BUCKETS * TE,), -1, jnp.int32)])
    zeros = jnp.zeros((N_BUCKETS * TE,), F32)
    _, ids, w_lo, w_hi = lax.sort((keys, ids, jnp.concatenate([w_lo, zeros]), jnp.concatenate([w_hi, zeros])),
                                  num_keys=1, is_stable=True)
    slot = jnp.arange(n_slots, dtype=jnp.int32)
    tok = jnp.where(ids < 0, n_tok + ((slot // TE) % 2) * TE + slot % TE, ids)
    wts = jnp.stack([w_lo.reshape(n_tiles, TE), w_hi.reshape(n_tiles, TE)], axis=1)

    tile_start = jnp.arange(n_tiles, dtype=jnp.int32) * TE
    tb = jnp.sum(tile_start[:, None] >= pad_end[None, :], axis=1, dtype=jnp.int32)
    used = tb < N_BUCKETS
    onehot = (tb[:, None] == buckets[None, :]).astype(jnp.int32)
    nv = jnp.clip(jnp.sum(onehot * (counts + pad_start)[None, :], axis=1) - tile_start, 0, TE) * used
    tbe = jnp.where(used, tb, jnp.max(jnp.where(used, tb, 0)))
    pid = tbe % len(PAIRS)
    pair_lo = (pid >= 3).astype(jnp.int32) + (pid >= 5).astype(jnp.int32)
    pair_hi = pid + 1 - 2 * (pid >= 3).astype(jnp.int32) - (pid >= 5).astype(jnp.int32)
    e1 = (tbe // len(PAIRS)) * EPG + pair_lo
    e2 = (tbe // len(PAIRS)) * EPG + pair_hi
    return tok.astype(jnp.int32), e1.astype(jnp.int32), e2.astype(jnp.int32), nv.astype(jnp.int32), wts


def _final_kernel(x_ref, f_ref, mod_ref, nw_ref, o_ref):
    x = x_ref[...] + mod_ref[0, 5:6, :] * _load_token_rows(f_ref, x_ref.shape[0])
    o_ref[...] = _rms(x, nw_ref[...])


def _final_call(x, f, mod, nw, *, n_lat, seq):
    d = x.shape[1]
    tm = TM_MERGE
    tps = seq // tm
    return pl.pallas_call(
        _final_kernel,
        grid=(n_lat // tm,),
        in_specs=[pl.BlockSpec((tm, d), lambda i: (i, 0)), pl.BlockSpec((tm * SUB, LANES), lambda i: (i, 0)),
                  pl.BlockSpec((1, 8, d), lambda i: (i // tps, 0, 0)), pl.BlockSpec((1, d), lambda i: (0, 0))],
        out_specs=pl.BlockSpec((tm, d), lambda i: (i, 0)),
        out_shape=jax.ShapeDtypeStruct((n_lat, d), F32),
        compiler_params=_params(("arbitrary",)),
        name="final",
    )(x, f, mod, nw)


def _rope_table(pos_r, pos_c, dim, lane_of):
    d = dim // 2
    half = d // 2
    lane = np.arange(LANES)
    rl = lane_of(lane)
    is_rope = rl >= 0
    rl = np.maximum(rl, 0)
    use_col = rl >= d
    j = rl % half
    first = (rl % d) < half
    inv = (np.float32(ROPE_BASE) ** (-(2.0 * j).astype(np.float32) / np.float32(d))).astype(np.float32)
    pos = np.where(use_col[None, :], pos_c[:, None], pos_r[:, None]).astype(np.float32)
    ang = pos * inv[None, :]
    cos = np.where(is_rope[None, :], np.cos(ang), 1.0)
    sin = np.where(is_rope[None, :], np.sin(ang), 0.0)
    return np.stack([cos, np.where(first[None, :], -sin, 0.0), np.where(first[None, :], 0.0, sin)]).astype(np.float32)


def _tables(seq):
    t = np.arange(seq)
    rows, cols = t // GRID_W, t % GRID_W
    tab_a = _rope_table(rows, cols, A_HEAD_DIM, lambda lane: lane % A_HEAD_DIM)
    tab_b = _rope_table(rows, cols, B_ROPE,
                        lambda lane: np.where((lane >= B_NOPE) & (lane < B_NOPE + B_ROPE), lane - B_NOPE, -1))
    ident = np.stack([np.ones((TM, LANES), np.float32), np.zeros((TM, LANES), np.float32),
                      np.zeros((TM, LANES), np.float32)])
    return jnp.asarray(np.concatenate([tab_a, ident], axis=1)), jnp.asarray(np.concatenate([tab_b, ident], axis=1))


def _window_band():
    r = np.arange(BLOCK)[:, None]
    c = np.arange(3 * BLOCK)[None, :]
    dist = c - r
    return jnp.asarray(np.where((dist >= BLOCK - WINDOW) & (dist <= BLOCK + WINDOW), 0.0, NEG).astype(np.float32))


def _layer_weights(w_in, w_uq, w_ukv, q_norm, kv_norm, w_o_a, w_o_b, w_out):
    d = w_in.shape[0]
    o = 0
    qa_w = w_in[:, o:o + 512]; o += 512
    ka_w = w_in[:, o:o + 128]; o += 128
    va_w = w_in[:, o:o + 128]; o += 128
    cq_w = w_in[:, o:o + B_Q_RANK]; o += B_Q_RANK
    ckv_w = w_in[:, o:o + B_KV_RANK]; o += B_KV_RANK
    kr_w = w_in[:, o:o + B_ROPE]; o += B_ROPE
    g_w = w_in[:, o:]

    def dup(wk):
        wk = wk.reshape(d, A_KV_HEADS, 1, A_HEAD_DIM)
        return jnp.broadcast_to(wk, (d, A_KV_HEADS, 2, A_HEAD_DIM)).reshape(d, A_KV_HEADS * 2 * A_HEAD_DIM)

    w1 = jnp.concatenate([qa_w, dup(ka_w), dup(va_w), cq_w, ckv_w, jnp.pad(kr_w, ((0, 0), (0, LANES - B_ROPE)))], axis=1)
    wuq = jnp.pad(w_uq.reshape(B_Q_RANK, B_HEADS, B_NOPE + B_ROPE),
                  ((0, 0), (0, 0), (0, LANES - B_NOPE - B_ROPE))).reshape(B_Q_RANK, B_HEADS * LANES)
    ukv = w_ukv.reshape(B_KV_RANK, B_HEADS, B_NOPE + B_V)
    wukn = jnp.pad(ukv[:, :, :B_NOPE], ((0, 0), (0, 0), (0, LANES - B_NOPE))).reshape(B_KV_RANK, B_HEADS * LANES)
    wuv = ukv[:, :, B_NOPE:].reshape(B_KV_RANK, B_HEADS * B_V)
    src = jnp.arange(LANES)[:, None]
    dst = jnp.arange(B_HEADS * LANES)[None, :]
    rp = ((src < B_ROPE) & (dst % LANES == src + B_NOPE)).astype(BF16)
    return dict(w1=w1.astype(BF16), wg=g_w.astype(BF16), qn=q_norm.reshape(1, -1), wuq=wuq.astype(BF16),
                kvn=kv_norm.reshape(1, -1), wukn=wukn.astype(BF16), wuv=wuv.astype(BF16), rp=rp,
                woa=w_o_a.astype(BF16), wob=w_o_b.astype(BF16), wout=w_out.astype(BF16))


def kernel(x, c, ctx, c_ctx, w_mod, b_mod, norm_mix, norm_ffn, w_in, attn_sink, mla_q_norm, w_uq, mla_kv_norm, w_ukv,
           w_o_a, w_o_b, w_out, w_router, router_bias, w_expert_gate, w_expert_up, w_expert_down, final_norm):
    batch, seq, d = x.shape
    ctx_len = ctx.shape[1]
    depth = w_mod.shape[0]
    n_lat = batch * seq
    n_all = n_lat + batch * ctx_len
    for tile in (TM, TM_MERGE):
        assert seq % tile == 0 and (batch * ctx_len) % tile == 0
    assert d == SUB * LANES and seq % TQ == 0 and seq % GRID_W == 0 and batch + 1 <= MOD_ROWS

    c_all = jnp.concatenate([c, c_ctx[None, :], jnp.zeros((MOD_ROWS - batch - 1, d), F32)], axis=0)
    mod = _mod_call(c_all, w_mod, b_mod)
    mod = jnp.pad(mod.transpose(0, 2, 1, 3), ((0, 0), (0, 0), (0, 2), (0, 0)))

    tab_a, tab_b = _tables(seq)
    wr = jnp.pad(w_router.astype(F32), ((0, 0), (0, LANES - N_EXPERTS)))
    wr_hi = wr.astype(BF16)
    wr2 = jnp.concatenate([wr_hi, (wr - wr_hi.astype(F32)).astype(BF16)], axis=1)
    rb = router_bias.astype(F32).reshape(N_EXPERTS, 1)

    xs = (x.reshape(n_lat, d), ctx.reshape(batch * ctx_len, d))
    f = None
    for l in range(depth):
        need_ctx = l < depth - 1
        w = _layer_weights(w_in[l], w_uq[l], w_ukv[l], mla_q_norm[l], mla_kv_norm[l], w_o_a[l], w_o_b[l], w_out[l])
        xs, qa, ka, va, qb, kb, vb, ga, gb = _proj_call(
            xs, f, mod[l - 1] if l else None, mod[l], norm_mix[l].reshape(1, d), tab_a, tab_b, w, n_lat=n_lat, seq=seq)
        ya = _window_call(attn_sink[l].astype(F32), qa, ka, va, batch=batch, seq=seq, ctx_len=ctx_len, need_ctx=need_ctx)
        yb = _mla_call(qb, kb, vb, batch=batch, seq=seq, ctx_len=ctx_len, latent=True)
        yb_ctx = _mla_call(qb, kb, vb, batch=batch, seq=seq, ctx_len=ctx_len, latent=False) if need_ctx else None
        n_rows = n_all if need_ctx else n_lat
        xs, h3, meta = _merge_call(xs, ya, yb, yb_ctx, ga, gb, mod[l], norm_ffn[l].reshape(1, d), w, wr2, rb,
                                   n_rows=n_rows, n_lat=n_lat, seq=seq)
        tok, e1, e2, nv, wts = _route_tables(meta, n_rows)
        f = _moe_call(tok, e1, e2, nv, wts, h3, w_expert_gate, w_expert_up, w_expert_down, layer=l, n_tok=n_rows)
    out = _final_call(xs, f, mod[depth - 1], final_norm.reshape(1, d), n_lat=n_lat, seq=seq)
    return out.reshape(batch, seq, d)
```

```python
import functools

import numpy as np
import jax
import jax.numpy as jnp
from jax import lax
from jax.experimental import pallas as pl
from jax.experimental.pallas import tpu as pltpu

F32 = jnp.float32
BF16 = jnp.bfloat16

EPS = 1e-6
ROPE_BASE = 10000.0
GRID_W = 64
BLOCK = 128
A_HEADS, A_KV_HEADS, A_HEAD_DIM = 8, 2, 64
A_GROUP = A_HEADS // A_KV_HEADS
WINDOW = 128
B_HEADS, B_NOPE, B_ROPE, B_V = 8, 64, 32, 64
B_Q_RANK, B_KV_RANK = 256, 128
N_EXPERTS, N_GROUPS = 16, 4
EPG = N_EXPERTS // N_GROUPS
D_EXPERT = 512
PAIRS = ((0, 1), (0, 2), (0, 3), (1, 2), (1, 3), (2, 3))
N_BUCKETS = N_GROUPS * len(PAIRS)

LANES = 128
SUB = 8
TM = 512
TM_MERGE = 512
TQ = 512
MLA_PAIRS = 4
TE = 256
MOE_BUFS = 3
MOD_ROWS = 24
NEG = -1e30
LOG2E = 1.4426950408889634
VMEM_LIMIT = 56 * 1024 * 1024

W1_QA, W1_KA, W1_VA, W1_CQ, W1_CKV, W1_KR = 0, 512, 640, 768, 1024, 1152
W1_COLS = 1280


def _params(sem, vmem=VMEM_LIMIT):
    return pltpu.CompilerParams(dimension_semantics=sem, vmem_limit_bytes=vmem)


def _dot(a, b):
    return jnp.dot(a, b, preferred_element_type=F32)


def _dot_nt(a, b):
    return lax.dot_general(a, b, (((1,), (1,)), ((), ())), preferred_element_type=F32)


def _rms(x, g):
    return x * lax.rsqrt(jnp.mean(x * x, axis=-1, keepdims=True) + EPS) * g


def _lane_tile(t, width):
    return jnp.concatenate([t] * (width // LANES), axis=1)


def _rope(t, tab_ref, shift):
    w = t.shape[-1]
    up = pltpu.roll(t, w - shift, 1)
    dn = pltpu.roll(t, shift, 1)
    return (t * _lane_tile(tab_ref[0], w) + up * _lane_tile(tab_ref[1], w)
            + dn * _lane_tile(tab_ref[2], w))


def _dup_kv_heads(t):
    swapped = pltpu.roll(t, A_HEAD_DIM, 1)
    lo = lax.broadcasted_iota(jnp.int32, t.shape, 1) < A_HEAD_DIM
    return jnp.concatenate([jnp.where(lo, t, swapped), jnp.where(lo, swapped, t)], axis=1)


def _load_token_rows(ref, rows):
    return jnp.concatenate([ref[pl.ds(s, rows, stride=SUB), :] for s in range(SUB)], axis=1)


def _store_token_rows(ref, val):
    rows = val.shape[0]
    for s in range(SUB):
        ref[pl.ds(s, rows, stride=SUB), :] = val[:, s * LANES:(s + 1) * LANES]


def _mod_kernel(c_ref, w_ref, b_ref, o_ref):
    c = c_ref[...]
    a = (c * jax.nn.sigmoid(c)).astype(BF16)
    o_ref[0, 0] = _dot(a, w_ref[0].astype(BF16)) + b_ref[0, 0]


def _mod_call(c_all, w_mod, b_mod):
    depth, d, _ = w_mod.shape
    return pl.pallas_call(
        _mod_kernel,
        grid=(depth, 6),
        in_specs=[pl.BlockSpec((MOD_ROWS, d), lambda l, k: (0, 0)),
                  pl.BlockSpec((1, d, d), lambda l, k: (l, 0, k)),
                  pl.BlockSpec((1, 1, 1, d), lambda l, k: (l, k, 0, 0))],
        out_specs=pl.BlockSpec((1, 1, MOD_ROWS, d), lambda l, k: (l, k, 0, 0)),
        out_shape=jax.ShapeDtypeStruct((depth, 6, MOD_ROWS, d), F32),
        compiler_params=_params(("arbitrary", "arbitrary")),
        name="mod",
    )(c_all, w_mod, b_mod.reshape(depth, 6, 1, d))


def _proj_kernel(*refs, has_f, n_lat_tiles):
    if has_f:
        x_ref, f_ref, modp_ref, refs = refs[0], refs[1], refs[2], refs[3:]
    else:
        x_ref, xc_ref, refs = refs[0], refs[1], refs[2:]
    (mod_ref, nw_ref, ta_ref, tb_ref, w1_ref, wg_ref, qn_ref, wuq_ref, kvn_ref, wukn_ref, wuv_ref,
     rp_ref), refs = refs[:12], refs[12:]
    xo_ref, qa_ref, ka_ref, va_ref, qb_ref, kb_ref, vb_ref, ga_ref, gb_ref = refs

    if has_f:
        x = x_ref[...] + modp_ref[0, 5:6, :] * _load_token_rows(f_ref, x_ref.shape[0])
    else:
        x = jnp.where(pl.program_id(0) < n_lat_tiles, x_ref[...], xc_ref[...])
    xo_ref[...] = x
    h = (_rms(x, nw_ref[...]) * (1.0 + mod_ref[0, 1:2, :]) + mod_ref[0, 0:1, :]).astype(BF16)

    t = _dot(h, w1_ref[...])
    qa_ref[...] = (_rope(t[:, W1_QA:W1_KA], ta_ref, 16) * (A_HEAD_DIM ** -0.5 * LOG2E)).astype(BF16)
    ka_ref[...] = _dup_kv_heads(_rope(t[:, W1_KA:W1_VA], ta_ref, 16)).astype(BF16)
    va_ref[...] = _dup_kv_heads(t[:, W1_VA:W1_CQ]).astype(BF16)

    cq = _rms(t[:, W1_CQ:W1_CKV], qn_ref[...]).astype(BF16)
    qb = _rope(_dot(cq, wuq_ref[...]), tb_ref, 8)
    qb_ref[...] = (qb * ((B_NOPE + B_ROPE) ** -0.5 * LOG2E)).astype(BF16)

    ckv = _rms(t[:, W1_CKV:W1_KR], kvn_ref[...]).astype(BF16)
    vb_ref[...] = _dot(ckv, wuv_ref[...]).astype(BF16)
    kr = t[:, W1_KR:W1_COLS]
    kr_hi = kr.astype(BF16)
    kr_lo = (kr - kr_hi.astype(F32)).astype(BF16)
    kb = _dot(ckv, wukn_ref[...]) + _dot(kr_hi, rp_ref[...]) + _dot(kr_lo, rp_ref[...])
    kb_ref[...] = _rope(kb, tb_ref, 8).astype(BF16)

    d = ga_ref.shape[-1]
    ga_ref[...] = jax.nn.sigmoid(_dot(h, wg_ref[:, :d])).astype(BF16)
    gb_ref[...] = jax.nn.sigmoid(_dot(h, wg_ref[:, d:])).astype(BF16)


def _proj_call(x, f, modp, mod, nw, tab_a, tab_b, w, *, n_lat, seq):
    has_f = f is not None
    n = x.shape[0] if has_f else x[0].shape[0] + x[1].shape[0]
    d = mod.shape[-1]
    nt = n // TM
    n_lat_t = n_lat // TM
    tps = seq // TM
    ctx_row = n_lat // seq

    def tok(i):
        return (i, 0)

    def modi(i):
        return (jnp.where(i < n_lat_t, i // tps, ctx_row), 0, 0)

    def tabi(i):
        return (0, jnp.where(i < n_lat_t, i % tps, tps), 0)

    def full(a):
        return pl.BlockSpec(a.shape, lambda i: (0,) * a.ndim, pipeline_mode=pl.Buffered(1))

    mod_spec = pl.BlockSpec((1, 8, d), modi)
    if has_f:
        in_specs = [pl.BlockSpec((TM, d), tok), pl.BlockSpec((TM * SUB, LANES), tok), mod_spec]
        args = [x, f, modp]
    else:
        in_specs = [pl.BlockSpec((TM, d), lambda i: (jnp.minimum(i, n_lat_t - 1), 0)),
                    pl.BlockSpec((TM, d), lambda i: (jnp.maximum(i - n_lat_t, 0), 0))]
        args = list(x)
    in_specs += [mod_spec, full(nw), pl.BlockSpec((3, TM, LANES), tabi), pl.BlockSpec((3, TM, LANES), tabi)]
    args += [mod, nw, tab_a, tab_b]
    for k in ("w1", "wg", "qn", "wuq", "kvn", "wukn", "wuv", "rp"):
        in_specs.append(full(w[k]))
        args.append(w[k])

    widths = (512, 256, 256, 1024, 1024, 512, d, d)
    out_shape = [jax.ShapeDtypeStruct((n, d), F32)] + [jax.ShapeDtypeStruct((n, wd), BF16) for wd in widths]
    out_specs = [pl.BlockSpec((TM, d), tok)] + [pl.BlockSpec((TM, wd), tok) for wd in widths]
    return pl.pallas_call(
        functools.partial(_proj_kernel, has_f=has_f, n_lat_tiles=n_lat_t),
        grid=(nt,), in_specs=in_specs, out_specs=out_specs, out_shape=out_shape,
        compiler_params=_params(("arbitrary",)),
        name="proj",
    )(*args)


def _window_kernel(sink_ref, band_ref, q_ref, kp_ref, kc_ref, kn_ref, kx_ref, vp_ref, vc_ref, vn_ref, vx_ref, o_ref,
                   *, nb):
    rows = A_GROUP * BLOCK
    lo = lax.broadcasted_iota(jnp.int32, (BLOCK, LANES), 1) < A_HEAD_DIM
    col_blk = lax.broadcasted_iota(jnp.int32, (1, 3 * BLOCK), 1) // BLOCK
    row_head = lax.broadcasted_iota(jnp.int32, (rows, 1), 0) // BLOCK
    zero = jnp.zeros((BLOCK, LANES), BF16)
    first, second = slice(0, BLOCK), slice(BLOCK, 2 * BLOCK)
    for sub in range(2):
        n = 2 * pl.program_id(1) + sub
        qrows = (first, second)[sub]
        off_prev = jnp.where(jnp.logical_or(n == 0, n >= nb), NEG, 0.0)
        off_cur = jnp.where(n >= nb, NEG, 0.0)
        off_next = jnp.where(n >= nb - 1, NEG, 0.0)
        bias = band_ref[...] + jnp.where(col_blk == 0, off_prev, jnp.where(col_blk == 1, off_cur, off_next))
        bias = jnp.concatenate([bias] * A_GROUP, axis=0)
        for k in range(A_KV_HEADS):
            ksl = slice(k * LANES, (k + 1) * LANES)
            parts = []
            for j in range(A_GROUP // 2):
                blk = q_ref[qrows, k * 2 * LANES + j * LANES:k * 2 * LANES + (j + 1) * LANES]
                parts += [jnp.where(lo, blk, zero), jnp.where(lo, zero, blk)]
            qs = jnp.concatenate(parts, axis=0)
            if sub == 0:
                k_lat = jnp.concatenate([kp_ref[:, ksl], kc_ref[:, ksl]], axis=0)
                v_lat = jnp.concatenate([vp_ref[:, ksl], vc_ref[:, ksl]], axis=0)
            else:
                k_lat = jnp.concatenate([kc_ref[:, ksl], kn_ref[:, ksl]], axis=0)
                v_lat = jnp.concatenate([vc_ref[:, ksl], vn_ref[:, ksl]], axis=0)
            s_l = _dot_nt(qs, k_lat) + bias
            s_x = _dot_nt(qs, kx_ref[:, ksl])
            sink = jnp.zeros((rows, 1), F32)
            for g in range(A_GROUP):
                sink = jnp.where(row_head == g, sink_ref[k * A_GROUP + g] * LOG2E, sink)
            m = jnp.maximum(jnp.maximum(jnp.max(s_l, axis=-1, keepdims=True), jnp.max(s_x, axis=-1, keepdims=True)),
                            sink)
            e_l = jnp.exp2(s_l - m)
            e_x = jnp.exp2(s_x - m)
            den = jnp.sum(e_l, axis=-1, keepdims=True) + jnp.sum(e_x, axis=-1, keepdims=True) + jnp.exp2(sink - m)
            o = (_dot(e_l.astype(BF16), v_lat) + _dot(e_x.astype(BF16), vx_ref[:, ksl])) / den
            for j in range(A_GROUP // 2):
                even = o[(2 * j) * BLOCK:(2 * j + 1) * BLOCK]
                odd = o[(2 * j + 1) * BLOCK:(2 * j + 2) * BLOCK]
                o_ref[qrows, k * 2 * LANES + j * LANES:k * 2 * LANES + (j + 1) * LANES] = (
                    jnp.where(lo, even, odd).astype(BF16))


def _window_call(sink, qa, ka, va, *, batch, seq, ctx_len, need_ctx):
    n = qa.shape[0] if need_ctx else batch * seq
    nb = seq // BLOCK
    ncb = ctx_len // BLOCK
    assert nb % 2 == 0 and ncb % 2 == 0
    nb2, ncb2 = nb // 2, ncb // 2
    nq = nb2 + (ncb2 if need_ctx else 0)

    def qi(b, i):
        return (jnp.where(i < nb2, b * nb2 + i, batch * nb2 + b * ncb2 + (i - nb2)), 0)

    def ki(off):
        return lambda b, i: (b * nb + jnp.clip(2 * i + off, 0, nb - 1), 0)

    def kc(b, i):
        return (b * nb2 + jnp.minimum(i, nb2 - 1), 0)

    def xi(b, i):
        return (batch * seq // ctx_len + b, 0)

    kv_specs = [pl.BlockSpec((BLOCK, 2 * LANES), ki(-1)), pl.BlockSpec((2 * BLOCK, 2 * LANES), kc),
                pl.BlockSpec((BLOCK, 2 * LANES), ki(2)), pl.BlockSpec((ctx_len, 2 * LANES), xi)]
    return pl.pallas_call(
        functools.partial(_window_kernel, nb=nb),
        grid=(batch, nq),
        in_specs=[pl.BlockSpec(memory_space=pltpu.SMEM), pl.BlockSpec((BLOCK, 3 * BLOCK), lambda b, i: (0, 0)),
                  pl.BlockSpec((2 * BLOCK, 4 * LANES), qi)] + kv_specs + kv_specs,
        out_specs=pl.BlockSpec((2 * BLOCK, 4 * LANES), qi),
        out_shape=jax.ShapeDtypeStruct((n, 4 * LANES), BF16),
        compiler_params=_params(("arbitrary", "arbitrary")),
        name="window",
    )(sink, _window_band(), qa, ka, ka, ka, ka, va, va, va, va)


def _mla_kernel(q_ref, *refs, with_lat):
    if with_lat:
        kl_ref, kx_ref, vl_ref, vx_ref, o_ref, vaug_ref = refs
    else:
        kx_ref, vx_ref, o_ref, vaug_ref = refs
    n_ctx = vx_ref.shape[0]
    n_pairs = o_ref.shape[1] // LANES
    lo = lax.broadcasted_iota(jnp.int32, (o_ref.shape[0], LANES), 1) < B_V

    @pl.when(pl.program_id(2) == 0)
    def _():
        one = jnp.ones((1, LANES), BF16)
        for v_ref, start in ((vx_ref, 0),) + (((vl_ref, n_ctx),) if with_lat else ()):
            n = v_ref.shape[0]
            keep = lax.broadcasted_iota(jnp.int32, (n, LANES), 1) < B_V
            for pp in range(n_pairs):
                v = v_ref[:, pp * LANES:(pp + 1) * LANES]
                vaug_ref[2 * pp, start:start + n, :] = jnp.where(keep, v, one)
                vaug_ref[2 * pp + 1, start:start + n, :] = jnp.where(keep, one, v)

    def head(h):
        sl = slice(h * LANES, (h + 1) * LANES)
        q = q_ref[:, sl]
        s_x = _dot_nt(q, kx_ref[:, sl])
        m = jnp.max(s_x, axis=-1, keepdims=True)
        if with_lat:
            s_l = _dot_nt(q, kl_ref[:, sl])
            m = jnp.maximum(m, jnp.max(s_l, axis=-1, keepdims=True))
        o = _dot(jnp.exp2(s_x - m).astype(BF16), vaug_ref[h, :n_ctx, :])
        if with_lat:
            o = o + _dot(jnp.exp2(s_l - m).astype(BF16), vaug_ref[h, n_ctx:, :])
        return o / pltpu.roll(o, B_V, 1)

    for pp in range(n_pairs):
        o_ref[:, pp * LANES:(pp + 1) * LANES] = jnp.where(lo, head(2 * pp), head(2 * pp + 1)).astype(BF16)


def _mla_call(qb, kb, vb, *, batch, seq, ctx_len, latent):
    ctx0 = batch * seq // ctx_len
    groups = B_HEADS // (2 * MLA_PAIRS)
    qk_w = MLA_PAIRS * 2 * LANES
    v_w = MLA_PAIRS * LANES
    tq = TQ if latent else ctx_len
    nq = seq // TQ if latent else 1
    kx_spec = pl.BlockSpec((ctx_len, qk_w), lambda b, p, j: (ctx0 + b, p))
    vx_spec = pl.BlockSpec((ctx_len, v_w), lambda b, p, j: (ctx0 + b, p))
    if latent:
        in_specs = [pl.BlockSpec((tq, qk_w), lambda b, p, j: (b * nq + j, p)),
                    pl.BlockSpec((seq, qk_w), lambda b, p, j: (b, p)), kx_spec,
                    pl.BlockSpec((seq, v_w), lambda b, p, j: (b, p)), vx_spec]
        args = (qb, kb, kb, vb, vb)
    else:
        in_specs = [pl.BlockSpec((tq, qk_w), lambda b, p, j: (ctx0 + b, p)), kx_spec, vx_spec]
        args = (qb, kb, vb)
    return pl.pallas_call(
        functools.partial(_mla_kernel, with_lat=latent),
        grid=(batch, groups, nq),
        in_specs=in_specs,
        out_specs=pl.BlockSpec((tq, v_w), lambda b, p, j: (b * nq + j, p)),
        out_shape=jax.ShapeDtypeStruct((batch * nq * tq, groups * v_w), BF16),
        scratch_shapes=[pltpu.VMEM((2 * MLA_PAIRS, ctx_len + (seq if latent else 0), LANES), BF16)],
        compiler_params=_params(("arbitrary", "arbitrary", "arbitrary")),
        name="mla" if latent else "mla_ctx",
    )(*args)


def _top2(v):
    i1 = jnp.zeros_like(v[0])
    m1 = v[0]
    for i in range(1, EPG):
        u = v[i] > m1
        i1 = jnp.where(u, float(i), i1)
        m1 = jnp.where(u, v[i], m1)
    i2 = jnp.zeros_like(v[0])
    m2 = jnp.full_like(v[0], -jnp.inf)
    for i in range(EPG):
        cand = jnp.where(i1 == float(i), -jnp.inf, v[i])
        u = cand > m2
        i2 = jnp.where(u, float(i), i2)
        m2 = jnp.where(u, cand, m2)
    return i1, i2, m1, m2


def _pick(idx, vals):
    out = vals[0]
    for i in range(1, len(vals)):
        out = jnp.where(idx == float(i), vals[i], out)
    return out


def _merge_kernel(x_ref, ya_ref, yb_ref, ybx_ref, ga_ref, gb_ref, mod_ref, nw_ref, woa_ref, wob_ref, wout_ref, wr2_ref,
                  rb_ref, xo_ref, h_ref, meta_ref, *, n_lat_tiles):
    yb = yb_ref[...] if ybx_ref is None else jnp.where(pl.program_id(0) < n_lat_tiles, yb_ref[...], ybx_ref[...])
    a = _dot(ya_ref[...], woa_ref[...])
    b = _dot(yb, wob_ref[...])
    mix = (ga_ref[...].astype(F32) * a + gb_ref[...].astype(F32) * b).astype(BF16)
    x = x_ref[...] + mod_ref[0, 2:3, :] * _dot(mix, wout_ref[...])
    xo_ref[...] = x
    h = _rms(x, nw_ref[...]) * (1.0 + mod_ref[0, 4:5, :]) + mod_ref[0, 3:4, :]
    _store_token_rows(h_ref, h)

    h_hi = h.astype(BF16)
    h_lo = (h - h_hi.astype(F32)).astype(BF16)
    l_hi = _dot(h_hi, wr2_ref[...])
    logits = l_hi[:, :LANES] + l_hi[:, LANES:] + _dot(h_lo, wr2_ref[:, :LANES])
    sc = jax.nn.sigmoid(logits.T[:N_EXPERTS, :])
    sel = sc + rb_ref[...]
    sel_rows = [sel[e:e + 1, :] for e in range(N_EXPERTS)]
    sc_rows = [sc[e:e + 1, :] for e in range(N_EXPERTS)]
    best = jnp.zeros_like(sel_rows[0])
    best_v = None
    for g in range(N_GROUPS):
        _, _, m1, m2 = _top2(sel_rows[g * EPG:(g + 1) * EPG])
        gv = m1 + m2
        if best_v is None:
            best_v = gv
        else:
            u = gv > best_v
            best = jnp.where(u, float(g), best)
            best_v = jnp.where(u, gv, best_v)
    sel_g = [_pick(best, [sel_rows[g * EPG + i] for g in range(N_GROUPS)]) for i in range(EPG)]
    sc_g = [_pick(best, [sc_rows[g * EPG + i] for g in range(N_GROUPS)]) for i in range(EPG)]
    i1, i2, _, _ = _top2(sel_g)
    s1 = _pick(i1, sc_g)
    s2 = _pick(i2, sc_g)
    tot = s1 + s2
    first_low = i1 < i2
    e_lo = jnp.where(first_low, i1, i2)
    e_hi = jnp.where(first_low, i2, i1)
    w_lo = jnp.where(first_low, s1, s2) / tot
    w_hi = jnp.where(first_low, s2, s1) / tot
    pid = jnp.where(e_lo == 0.0, e_hi - 1.0, jnp.where(e_lo == 1.0, e_hi + 1.0, 5.0))
    bucket = best * float(len(PAIRS)) + pid
    t = sel.shape[1]
    meta_ref[0] = jnp.concatenate([w_lo, w_hi, bucket, jnp.zeros((SUB - 3, t), F32)], axis=0)


def _merge_call(x, ya, yb, yb_ctx, ga, gb, mod, nw, w, wr2, rb, *, n_rows, n_lat, seq):
    d = x.shape[1]
    tm = TM_MERGE
    nt = n_rows // tm
    n_lat_t = n_lat // tm
    tps = seq // tm
    ctx_row = n_lat // seq

    def tok(i):
        return (i, 0)

    def full(a):
        return pl.BlockSpec(a.shape, lambda i: (0,) * a.ndim)

    yb_specs = [pl.BlockSpec((tm, 512), lambda i: (jnp.minimum(i, n_lat_t - 1), 0))]
    yb_args = [yb]
    if yb_ctx is not None:
        yb_specs.append(pl.BlockSpec((tm, 512), lambda i: (jnp.maximum(i - n_lat_t, 0), 0)))
        yb_args.append(yb_ctx)

    def body(x_ref, ya_ref, *refs):
        if yb_ctx is None:
            refs = refs[:1] + (None,) + refs[1:]
        _merge_kernel(x_ref, ya_ref, *refs, n_lat_tiles=n_lat_t)

    return pl.pallas_call(
        body,
        grid=(nt,),
        in_specs=[pl.BlockSpec((tm, d), tok), pl.BlockSpec((tm, 512), tok)] + yb_specs + [
                  pl.BlockSpec((tm, d), tok), pl.BlockSpec((tm, d), tok),
                  pl.BlockSpec((1, 8, d), lambda i: (jnp.where(i < n_lat_t, i // tps, ctx_row), 0, 0)),
                  full(nw), full(w["woa"]), full(w["wob"]), full(w["wout"]), full(wr2), full(rb)],
        out_specs=[pl.BlockSpec((tm, d), tok), pl.BlockSpec((tm * SUB, LANES), tok),
                   pl.BlockSpec((1, SUB, tm), lambda i: (i, 0, 0))],
        out_shape=[jax.ShapeDtypeStruct((n_rows, d), F32), jax.ShapeDtypeStruct((n_rows * SUB, LANES), F32),
                   jax.ShapeDtypeStruct((nt, SUB, tm), F32)],
        compiler_params=_params(("arbitrary",)),
        name="merge",
    )(x, ya, *yb_args, ga, gb, mod, nw, w["woa"], w["wob"], w["wout"], wr2, rb)


def _moe_kernel(tok_ref, e1_ref, e2_ref, nv_ref, wts_ref, h_hbm, wg1_ref, wu1_ref, wd1_ref, wg2_ref, wu2_ref, wd2_ref,
                f_hbm, hbuf, obuf, wgu_ref, wdn_ref, gsem, ssem, *, n_tok, n_tiles):
    t = pl.program_id(0)
    slot = t % MOE_BUFS
    slot1 = (t + 1) % MOE_BUFS
    slot2 = (t + 2) % MOE_BUFS
    live = nv_ref[t] > 0
    prev_live = jnp.logical_and(t >= 1, nv_ref[jnp.maximum(t - 1, 0)] > 0)

    t_prev = jnp.maximum(t - 1, 0)
    new_pair = jnp.logical_or(t == 0, jnp.logical_or(e1_ref[t] != e1_ref[t_prev], e2_ref[t] != e2_ref[t_prev]))

    @pl.when(jnp.logical_and(live, new_pair))
    def _():
        for i, w_ref in enumerate((wg1_ref, wu1_ref, wg2_ref, wu2_ref)):
            wgu_ref[i] = w_ref[0, 0].astype(BF16)
        for i, w_ref in enumerate((wd1_ref, wd2_ref)):
            wdn_ref[i] = w_ref[0, 0].astype(BF16)

    def gather_row(tile, s, r):
        tk = jnp.minimum(tok_ref[tile * TE + r], n_tok - 1)
        return pltpu.make_async_copy(h_hbm.at[pl.ds(tk * SUB, SUB)], hbuf.at[s, pl.ds(r * SUB, SUB)], gsem.at[s])

    def scatter_row(tile, s, r):
        tk = tok_ref[tile * TE + r]
        return pltpu.make_async_copy(obuf.at[s, pl.ds(r * SUB, SUB)], f_hbm.at[pl.ds(tk * SUB, SUB)], ssem.at[s])

    def wait_gather(s):
        pltpu.make_async_copy(h_hbm.at[pl.ds(0, TE * SUB)], hbuf.at[s], gsem.at[s]).wait()

    def wait_scatter(s):
        pltpu.make_async_copy(obuf.at[s], f_hbm.at[pl.ds(0, TE * SUB)], ssem.at[s]).wait()

    def start_rows(make_row, tile, s):
        def body(i, carry):
            make_row(tile, s, 2 * i).start(priority=0)
            make_row(tile, s, 2 * i + 1).start(priority=1)
            return carry
        lax.fori_loop(0, TE // 2, body, 0, unroll=4)

    @pl.when(t == 0)
    def _():
        obuf[...] = jnp.zeros(obuf.shape, F32)
        for half in range(2):
            fill = pltpu.make_async_copy(obuf.at[0], f_hbm.at[pl.ds((n_tok + half * TE) * SUB, TE * SUB)], ssem.at[0])
            fill.start()
            fill.wait()
        start_rows(gather_row, 0, 0)
        start_rows(gather_row, 1, 1)

    @pl.when(live)
    def _():
        wait_gather(slot)

        @pl.when(t >= 2)
        def _():
            wait_scatter(slot)

    @pl.when(live)
    def _():
        g_tile = jnp.minimum(t + 2, n_tiles - 1)
        s_tile = jnp.where(t == 0, n_tiles - 1, t - 1)
        issue = ([functools.partial(gather_row, g_tile, slot2, r) for r in range(TE)]
                 + [functools.partial(scatter_row, s_tile, slot2, r) for r in range(TE)])
        n_stage = 6
        per_stage = -(-len(issue) // n_stage)

        def issue_stage(k):
            for i, make in enumerate(issue[k * per_stage:(k + 1) * per_stage]):
                make().start(priority=i % 2)

        h = _load_token_rows(hbuf.at[slot], TE).astype(BF16)
        w_lo = jnp.broadcast_to(wts_ref[0, 0:1, :], (LANES, TE)).T[:, :1]
        w_hi = jnp.broadcast_to(wts_ref[0, 1:2, :], (LANES, TE)).T[:, :1]
        outs = []
        stage = 0
        for which, wt in ((0, w_lo), (1, w_hi)):
            hg = _dot(h, wgu_ref[2 * which])
            issue_stage(stage)
            hu = _dot(h, wgu_ref[2 * which + 1])
            issue_stage(stage + 1)
            act = (hg * jax.nn.sigmoid(hg) * hu).astype(BF16)
            outs.append(wt * _dot(act, wdn_ref[which]))
            issue_stage(stage + 2)
            stage += 3
        _store_token_rows(obuf.at[slot], outs[0] + outs[1])

    @pl.when(jnp.logical_and(jnp.logical_not(live), prev_live))
    def _():
        wait_gather(slot)
        wait_gather(slot1)
        wait_scatter(slot1)

        @pl.when(t >= 2)
        def _():
            wait_scatter(slot)
        start_rows(scatter_row, t - 1, slot2)
        wait_scatter(slot2)


def _moe_call(tok, e1, e2, nv, wts, h3, wg, wu, wd, *, layer, n_tok):
    d = wg.shape[2]
    de = wg.shape[3]
    n_tiles = e1.shape[0]

    def wspec(shape, which):
        return pl.BlockSpec((1, 1) + shape, lambda t, tok, e1, e2, nv: (layer, (e1, e2)[which][t], 0, 0))

    grid_spec = pltpu.PrefetchScalarGridSpec(
        num_scalar_prefetch=4,
        grid=(n_tiles,),
        in_specs=[pl.BlockSpec((1, 2, TE), lambda t, tok, e1, e2, nv: (t, 0, 0)),
                  pl.BlockSpec(memory_space=pl.ANY),
                  wspec((d, de), 0), wspec((d, de), 0), wspec((de, d), 0),
                  wspec((d, de), 1), wspec((d, de), 1), wspec((de, d), 1)],
        out_specs=pl.BlockSpec(memory_space=pl.ANY),
        scratch_shapes=[pltpu.VMEM((MOE_BUFS, TE * SUB, LANES), F32), pltpu.VMEM((MOE_BUFS, TE * SUB, LANES), F32),
                        pltpu.VMEM((4, d, de), BF16), pltpu.VMEM((2, de, d), BF16),
                        pltpu.SemaphoreType.DMA((MOE_BUFS,)), pltpu.SemaphoreType.DMA((MOE_BUFS,))],
    )
    return pl.pallas_call(
        functools.partial(_moe_kernel, n_tok=n_tok, n_tiles=n_tiles),
        grid_spec=grid_spec,
        out_shape=jax.ShapeDtypeStruct(((n_tok + 2 * TE) * SUB, LANES), F32),
        compiler_params=_params(("arbitrary",)),
        name="moe",
    )(tok, e1, e2, nv, wts, h3, wg, wu, wd, wg, wu, wd)


def _route_tables(meta, n_tok):
    w_lo = meta[:, 0, :].reshape(-1)
    w_hi = meta[:, 1, :].reshape(-1)
    bucket = meta[:, 2, :].reshape(-1).astype(jnp.int32)
    n_slots = n_tok + N_BUCKETS * TE
    n_tiles = n_slots // TE
    buckets = jnp.arange(N_BUCKETS, dtype=jnp.int32)
    counts = jnp.sum(bucket[:, None] == buckets[None, :], axis=0, dtype=jnp.int32)
    padded = ((counts + TE - 1) // TE) * TE
    pad_end = jnp.cumsum(padded)
    pad_start = pad_end - padded
    fill_i = jnp.arange(TE, dtype=jnp.int32)[None, :]
    fill_key = jnp.where(fill_i < (padded - counts)[:, None], 2 * buckets[:, None] + 1, 2 * N_BUCKETS).reshape(-1)
    keys = jnp.concatenate([2 * bucket, fill_key])
    ids = jnp.concatenate([jnp.arange(n_tok, dtype=jnp.int32), jnp.full((N_BUCKETS * TE,), -1, jnp.int32)])
    zeros = jnp.zeros((N_BUCKETS * TE,), F32)
    _, ids, w_lo, w_hi = lax.sort((keys, ids, jnp.concatenate([w_lo, zeros]), jnp.concatenate([w_hi, zeros])),
                                  num_keys=1, is_stable=True)
    slot = jnp.arange(n_slots, dtype=jnp.int32)
    tok = jnp.where(ids < 0, n_tok + ((slot // TE) % 2) * TE + slot % TE, ids)
    wts = jnp.stack([w_lo.reshape(n_tiles, TE), w_hi.reshape(n_tiles, TE)], axis=1)

    tile_start = jnp.arange(n_tiles, dtype=jnp.int32) * TE
    tb = jnp.sum(tile_start[:, None] >= pad_end[None, :], axis=1, dtype=jnp.int32)
    used = tb < N_BUCKETS
    onehot = (tb[:, None] == buckets[None, :]).astype(jnp.int32)
    nv = jnp.clip(jnp.sum(onehot * (counts + pad_start)[None, :], axis=1) - tile_start, 0, TE) * used
    tbe = jnp.where(used, tb, jnp.max(jnp.where(used, tb, 0)))
    pid = tbe % len(PAIRS)
    pair_lo = (pid >= 3).astype(jnp.int32) + (pid >= 5).astype(jnp.int32)
    pair_hi = pid + 1 - 2 * (pid >= 3).astype(jnp.int32) - (pid >= 5).astype(jnp.int32)
    e1 = (tbe // len(PAIRS)) * EPG + pair_lo
    e2 = (tbe // len(PAIRS)) * EPG + pair_hi
    return tok.astype(jnp.int32), e1.astype(jnp.int32), e2.astype(jnp.int32), nv.astype(jnp.int32), wts


def _final_kernel(x_ref, f_ref, mod_ref, nw_ref, o_ref):
    x = x_ref[...] + mod_ref[0, 5:6, :] * _load_token_rows(f_ref, x_ref.shape[0])
    o_ref[...] = _rms(x, nw_ref[...])


def _final_call(x, f, mod, nw, *, n_lat, seq):
    d = x.shape[1]
    tm = TM_MERGE
    tps = seq // tm
    return pl.pallas_call(
        _final_kernel,
        grid=(n_lat // tm,),
        in_specs=[pl.BlockSpec((tm, d), lambda i: (i, 0)), pl.BlockSpec((tm * SUB, LANES), lambda i: (i, 0)),
                  pl.BlockSpec((1, 8, d), lambda i: (i // tps, 0, 0)), pl.BlockSpec((1, d), lambda i: (0, 0))],
        out_specs=pl.BlockSpec((tm, d), lambda i: (i, 0)),
        out_shape=jax.ShapeDtypeStruct((n_lat, d), F32),
        compiler_params=_params(("arbitrary",)),
        name="final",
    )(x, f, mod, nw)


def _rope_table(pos_r, pos_c, dim, lane_of):
    d = dim // 2
    half = d // 2
    lane = np.arange(LANES)
    rl = lane_of(lane)
    is_rope = rl >= 0
    rl = np.maximum(rl, 0)
    use_col = rl >= d
    j = rl % half
    first = (rl % d) < half
    inv = (np.float32(ROPE_BASE) ** (-(2.0 * j).astype(np.float32) / np.float32(d))).astype(np.float32)
    pos = np.where(use_col[None, :], pos_c[:, None], pos_r[:, None]).astype(np.float32)
    ang = pos * inv[None, :]
    cos = np.where(is_rope[None, :], np.cos(ang), 1.0)
    sin = np.where(is_rope[None, :], np.sin(ang), 0.0)
    return np.stack([cos, np.where(first[None, :], -sin, 0.0), np.where(first[None, :], 0.0, sin)]).astype(np.float32)


def _tables(seq):
    t = np.arange(seq)
    rows, cols = t // GRID_W, t % GRID_W
    tab_a = _rope_table(rows, cols, A_HEAD_DIM, lambda lane: lane % A_HEAD_DIM)
    tab_b = _rope_table(rows, cols, B_ROPE,
                        lambda lane: np.where((lane >= B_NOPE) & (lane < B_NOPE + B_ROPE), lane - B_NOPE, -1))
    ident = np.stack([np.ones((TM, LANES), np.float32), np.zeros((TM, LANES), np.float32),
                      np.zeros((TM, LANES), np.float32)])
    return jnp.asarray(np.concatenate([tab_a, ident], axis=1)), jnp.asarray(np.concatenate([tab_b, ident], axis=1))


def _window_band():
    r = np.arange(BLOCK)[:, None]
    c = np.arange(3 * BLOCK)[None, :]
    dist = c - r
    return jnp.asarray(np.where((dist >= BLOCK - WINDOW) & (dist <= BLOCK + WINDOW), 0.0, NEG).astype(np.float32))


def _layer_weights(w_in, w_uq, w_ukv, q_norm, kv_norm, w_o_a, w_o_b, w_out):
    n_mix = W1_KR + B_ROPE
    w1 = jnp.pad(w_in[:, :n_mix], ((0, 0), (0, W1_COLS - n_mix)))
    g_w = w_in[:, n_mix:]
    wuq = jnp.pad(w_uq.reshape(B_Q_RANK, B_HEADS, B_NOPE + B_ROPE),
                  ((0, 0), (0, 0), (0, LANES - B_NOPE - B_ROPE))).reshape(B_Q_RANK, B_HEADS * LANES)
    ukv = w_ukv.reshape(B_KV_RANK, B_HEADS, B_NOPE + B_V)
    wukn = jnp.pad(ukv[:, :, :B_NOPE], ((0, 0), (0, 0), (0, LANES - B_NOPE))).reshape(B_KV_RANK, B_HEADS * LANES)
    wuv = ukv[:, :, B_NOPE:].reshape(B_KV_RANK, B_HEADS * B_V)
    src = jnp.arange(LANES)[:, None]
    dst = jnp.arange(B_HEADS * LANES)[None, :]
    rp = ((src < B_ROPE) & (dst % LANES == src + B_NOPE)).astype(BF16)
    return dict(w1=w1.astype(BF16), wg=g_w.astype(BF16), qn=q_norm.reshape(1, -1), wuq=wuq.astype(BF16),
                kvn=kv_norm.reshape(1, -1), wukn=wukn.astype(BF16), wuv=wuv.astype(BF16), rp=rp,
                woa=w_o_a.astype(BF16), wob=w_o_b.astype(BF16), wout=w_out.astype(BF16))


def kernel(x, c, ctx, c_ctx, w_mod, b_mod, norm_mix, norm_ffn, w_in, attn_sink, mla_q_norm, w_uq, mla_kv_norm, w_ukv,
           w_o_a, w_o_b, w_out, w_router, router_bias, w_expert_gate, w_expert_up, w_expert_down, final_norm):
    batch, seq, d = x.shape
    ctx_len = ctx.shape[1]
    depth = w_mod.shape[0]
    n_lat = batch * seq
    n_all = n_lat + batch * ctx_len
    for tile in (TM, TM_MERGE):
        assert seq % tile == 0 and (batch * ctx_len) % tile == 0
    assert d == SUB * LANES and seq % TQ == 0 and seq % GRID_W == 0 and batch + 1 <= MOD_ROWS

    c_all = jnp.concatenate([c, c_ctx[None, :], jnp.zeros((MOD_ROWS - batch - 1, d), F32)], axis=0)
    mod = _mod_call(c_all, w_mod, b_mod)
    mod = jnp.pad(mod.transpose(0, 2, 1, 3), ((0, 0), (0, 0), (0, 2), (0, 0)))

    tab_a, tab_b = _tables(seq)
    wr = jnp.pad(w_router.astype(F32), ((0, 0), (0, LANES - N_EXPERTS)))
    wr_hi = wr.astype(BF16)
    wr2 = jnp.concatenate([wr_hi, (wr - wr_hi.astype(F32)).astype(BF16)], axis=1)
    rb = router_bias.astype(F32).reshape(N_EXPERTS, 1)

    xs = (x.reshape(n_lat, d), ctx.reshape(batch * ctx_len, d))
    f = None
    for l in range(depth):
        need_ctx = l < depth - 1
        w = _layer_weights(w_in[l], w_uq[l], w_ukv[l], mla_q_norm[l], mla_kv_norm[l], w_o_a[l], w_o_b[l], w_out[l])
        xs, qa, ka, va, qb, kb, vb, ga, gb = _proj_call(
            xs, f, mod[l - 1] if l else None, mod[l], norm_mix[l].reshape(1, d), tab_a, tab_b, w, n_lat=n_lat, seq=seq)
        ya = _window_call(attn_sink[l].astype(F32), qa, ka, va, batch=batch, seq=seq, ctx_len=ctx_len, need_ctx=need_ctx)
        yb = _mla_call(qb, kb, vb, batch=batch, seq=seq, ctx_len=ctx_len, latent=True)
        yb_ctx = _mla_call(qb, kb, vb, batch=batch, seq=seq, ctx_len=ctx_len, latent=False) if need_ctx else None
        n_rows = n_all if need_ctx else n_lat
        xs, h3, meta = _merge_call(xs, ya, yb, yb_ctx, ga, gb, mod[l], norm_ffn[l].reshape(1, d), w, wr2, rb,
                                   n_rows=n_rows, n_lat=n_lat, seq=seq)
        tok, e1, e2, nv, wts = _route_tables(meta, n_rows)
        f = _moe_call(tok, e1, e2, nv, wts, h3, w_expert_gate, w_expert_up, w_expert_down, layer=l, n_tok=n_rows)
    out = _final_call(xs, f, mod[depth - 1], final_norm.reshape(1, d), n_lat=n_lat, seq=seq)
    return out.reshape(batch, seq, d)
```

```python
import functools

import numpy as np
import jax
import jax.numpy as jnp
from jax import lax
from jax.experimental import pallas as pl
from jax.experimental.pallas import tpu as pltpu

F32 = jnp.float32
BF16 = jnp.bfloat16

EPS = 1e-6
ROPE_BASE = 10000.0
GRID_W = 64
BLOCK = 128
A_HEADS, A_KV_HEADS, A_HEAD_DIM = 8, 2, 64
A_GROUP = A_HEADS // A_KV_HEADS
WINDOW = 128
B_HEADS, B_NOPE, B_ROPE, B_V = 8, 64, 32, 64
B_Q_RANK, B_KV_RANK = 256, 128
N_EXPERTS, N_GROUPS = 16, 4
EPG = N_EXPERTS // N_GROUPS
D_EXPERT = 512
PAIRS = ((0, 1), (0, 2), (0, 3), (1, 2), (1, 3), (2, 3))
N_BUCKETS = N_GROUPS * len(PAIRS)

LANES = 128
SUB = 8
TM = 512
TM_MERGE = 512
TQ = 512
MLA_PAIRS = 4
WIN_QB = 4
TE = 256
MOE_BUFS = 3
MOD_ROWS = 24
NEG = -1e30
LOG2E = 1.4426950408889634
VMEM_LIMIT = 56 * 1024 * 1024

W1_QA, W1_KA, W1_VA, W1_CQ, W1_CKV, W1_KR = 0, 512, 640, 768, 1024, 1152
W1_COLS = 1280


def _params(sem, vmem=VMEM_LIMIT):
    return pltpu.CompilerParams(dimension_semantics=sem, vmem_limit_bytes=vmem)


def _dot(a, b):
    return jnp.dot(a, b, preferred_element_type=F32)


def _dot_nt(a, b):
    return lax.dot_general(a, b, (((1,), (1,)), ((), ())), preferred_element_type=F32)


def _rms(x, g):
    return x * lax.rsqrt(jnp.mean(x * x, axis=-1, keepdims=True) + EPS) * g


def _lane_tile(t, width):
    return jnp.concatenate([t] * (width // LANES), axis=1)


def _rope(t, tab_ref, shift):
    w = t.shape[-1]
    up = pltpu.roll(t, w - shift, 1)
    dn = pltpu.roll(t, shift, 1)
    return (t * _lane_tile(tab_ref[0], w) + up * _lane_tile(tab_ref[1], w)
            + dn * _lane_tile(tab_ref[2], w))


def _dup_kv_heads(t):
    swapped = pltpu.roll(t, A_HEAD_DIM, 1)
    lo = lax.broadcasted_iota(jnp.int32, t.shape, 1) < A_HEAD_DIM
    return jnp.concatenate([jnp.where(lo, t, swapped), jnp.where(lo, swapped, t)], axis=1)


def _load_token_rows(ref, rows):
    return jnp.concatenate([ref[pl.ds(s, rows, stride=SUB), :] for s in range(SUB)], axis=1)


def _store_token_rows(ref, val):
    rows = val.shape[0]
    for s in range(SUB):
        ref[pl.ds(s, rows, stride=SUB), :] = val[:, s * LANES:(s + 1) * LANES]


def _mod_kernel(c_ref, w_ref, b_ref, o_ref):
    c = c_ref[...]
    a = (c * jax.nn.sigmoid(c)).astype(BF16)
    o_ref[0, 0] = _dot(a, w_ref[0].astype(BF16)) + b_ref[0, 0]


def _mod_call(c_all, w_mod, b_mod):
    depth, d, _ = w_mod.shape
    return pl.pallas_call(
        _mod_kernel,
        grid=(depth, 6),
        in_specs=[pl.BlockSpec((MOD_ROWS, d), lambda l, k: (0, 0)),
                  pl.BlockSpec((1, d, d), lambda l, k: (l, 0, k)),
                  pl.BlockSpec((1, 1, 1, d), lambda l, k: (l, k, 0, 0))],
        out_specs=pl.BlockSpec((1, 1, MOD_ROWS, d), lambda l, k: (l, k, 0, 0)),
        out_shape=jax.ShapeDtypeStruct((depth, 6, MOD_ROWS, d), F32),
        compiler_params=_params(("arbitrary", "arbitrary")),
        name="mod",
    )(c_all, w_mod, b_mod.reshape(depth, 6, 1, d))


def _proj_kernel(*refs, has_f, n_lat_tiles):
    if has_f:
        x_ref, f_ref, modp_ref, refs = refs[0], refs[1], refs[2], refs[3:]
    else:
        x_ref, xc_ref, refs = refs[0], refs[1], refs[2:]
    (mod_ref, nw_ref, ta_ref, tb_ref, w1_ref, wg_ref, qn_ref, wuq_ref, kvn_ref, wukn_ref, wuv_ref,
     rp_ref), refs = refs[:12], refs[12:]
    xo_ref, qa_ref, ka_ref, va_ref, qb_ref, kb_ref, vb_ref, ga_ref, gb_ref = refs

    if has_f:
        x = x_ref[...] + modp_ref[0, 5:6, :] * _load_token_rows(f_ref, x_ref.shape[0])
    else:
        x = jnp.where(pl.program_id(0) < n_lat_tiles, x_ref[...], xc_ref[...])
    xo_ref[...] = x
    h = (_rms(x, nw_ref[...]) * (1.0 + mod_ref[0, 1:2, :]) + mod_ref[0, 0:1, :]).astype(BF16)

    t = _dot(h, w1_ref[...])
    qa_ref[...] = (_rope(t[:, W1_QA:W1_KA], ta_ref, 16) * (A_HEAD_DIM ** -0.5 * LOG2E)).astype(BF16)
    ka_ref[...] = _dup_kv_heads(_rope(t[:, W1_KA:W1_VA], ta_ref, 16)).astype(BF16)
    va_ref[...] = _dup_kv_heads(t[:, W1_VA:W1_CQ]).astype(BF16)

    cq = _rms(t[:, W1_CQ:W1_CKV], qn_ref[...]).astype(BF16)
    qb = _rope(_dot(cq, wuq_ref[...]), tb_ref, 8)
    qb_ref[...] = (qb * ((B_NOPE + B_ROPE) ** -0.5 * LOG2E)).astype(BF16)

    ckv = _rms(t[:, W1_CKV:W1_KR], kvn_ref[...]).astype(BF16)
    vb_ref[...] = _dot(ckv, wuv_ref[...]).astype(BF16)
    kr = t[:, W1_KR:W1_COLS]
    kr_hi = kr.astype(BF16)
    kr_lo = (kr - kr_hi.astype(F32)).astype(BF16)
    kb = _dot(ckv, wukn_ref[...]) + _dot(kr_hi, rp_ref[...]) + _dot(kr_lo, rp_ref[...])
    kb_ref[...] = _rope(kb, tb_ref, 8).astype(BF16)

    d = ga_ref.shape[-1]
    ga_ref[...] = jax.nn.sigmoid(_dot(h, wg_ref[:, :d])).astype(BF16)
    gb_ref[...] = jax.nn.sigmoid(_dot(h, wg_ref[:, d:])).astype(BF16)


def _proj_call(x, f, modp, mod, nw, tab_a, tab_b, w, *, n_lat, seq):
    has_f = f is not None
    n = x.shape[0] if has_f else x[0].shape[0] + x[1].shape[0]
    d = mod.shape[-1]
    nt = n // TM
    n_lat_t = n_lat // TM
    tps = seq // TM
    ctx_row = n_lat // seq

    def tok(i):
        return (i, 0)

    def modi(i):
        return (jnp.where(i < n_lat_t, i // tps, ctx_row), 0, 0)

    def tabi(i):
        return (0, jnp.where(i < n_lat_t, i % tps, tps), 0)

    def full(a):
        return pl.BlockSpec(a.shape, lambda i: (0,) * a.ndim, pipeline_mode=pl.Buffered(1))

    mod_spec = pl.BlockSpec((1, 8, d), modi)
    if has_f:
        in_specs = [pl.BlockSpec((TM, d), tok), pl.BlockSpec((TM * SUB, LANES), tok), mod_spec]
        args = [x, f, modp]
    else:
        in_specs = [pl.BlockSpec((TM, d), lambda i: (jnp.minimum(i, n_lat_t - 1), 0)),
                    pl.BlockSpec((TM, d), lambda i: (jnp.maximum(i - n_lat_t, 0), 0))]
        args = list(x)
    in_specs += [mod_spec, full(nw), pl.BlockSpec((3, TM, LANES), tabi), pl.BlockSpec((3, TM, LANES), tabi)]
    args += [mod, nw, tab_a, tab_b]
    for k in ("w1", "wg", "qn", "wuq", "kvn", "wukn", "wuv", "rp"):
        in_specs.append(full(w[k]))
        args.append(w[k])

    widths = (512, 256, 256, 1024, 1024, 512, d, d)
    out_shape = [jax.ShapeDtypeStruct((n, d), F32)] + [jax.ShapeDtypeStruct((n, wd), BF16) for wd in widths]
    out_specs = [pl.BlockSpec((TM, d), tok)] + [pl.BlockSpec((TM, wd), tok) for wd in widths]
    return pl.pallas_call(
        functools.partial(_proj_kernel, has_f=has_f, n_lat_tiles=n_lat_t),
        grid=(nt,), in_specs=in_specs, out_specs=out_specs, out_shape=out_shape,
        compiler_params=_params(("arbitrary",)),
        name="proj",
    )(*args)


def _window_kernel(sink_ref, band_ref, q_ref, kp_ref, kc_ref, kn_ref, kx_ref, vp_ref, vc_ref, vn_ref, vx_ref, o_ref,
                   *, nb, n_off):
    rows = A_GROUP * BLOCK
    wq = q_ref.shape[0] // BLOCK
    lo = lax.broadcasted_iota(jnp.int32, (BLOCK, LANES), 1) < A_HEAD_DIM
    col_blk = lax.broadcasted_iota(jnp.int32, (1, 3 * BLOCK), 1) // BLOCK
    row_head = lax.broadcasted_iota(jnp.int32, (rows, 1), 0) // BLOCK
    zero = jnp.zeros((BLOCK, LANES), BF16)

    def key_block(sub, rel, lanes, p_ref, c_ref, n_ref):
        j = sub + rel
        if j < 0:
            return p_ref[:, lanes]
        if j >= wq:
            return n_ref[:, lanes]
        return c_ref[j * BLOCK:(j + 1) * BLOCK, lanes]

    for sub in range(wq):
        n = wq * pl.program_id(1) + n_off + sub
        qrows = slice(sub * BLOCK, (sub + 1) * BLOCK)
        off_prev = jnp.where(jnp.logical_or(n == 0, n >= nb), NEG, 0.0)
        off_cur = jnp.where(n >= nb, NEG, 0.0)
        off_next = jnp.where(n >= nb - 1, NEG, 0.0)
        bias = band_ref[...] + jnp.where(col_blk == 0, off_prev, jnp.where(col_blk == 1, off_cur, off_next))
        bias = jnp.concatenate([bias] * A_GROUP, axis=0)
        for k in range(A_KV_HEADS):
            ksl = slice(k * LANES, (k + 1) * LANES)
            parts = []
            for j in range(A_GROUP // 2):
                blk = q_ref[qrows, k * 2 * LANES + j * LANES:k * 2 * LANES + (j + 1) * LANES]
                parts += [jnp.where(lo, blk, zero), jnp.where(lo, zero, blk)]
            qs = jnp.concatenate(parts, axis=0)
            k_lat = jnp.concatenate([key_block(sub, rel, ksl, kp_ref, kc_ref, kn_ref) for rel in (-1, 0, 1)], axis=0)
            v_lat = jnp.concatenate([key_block(sub, rel, ksl, vp_ref, vc_ref, vn_ref) for rel in (-1, 0, 1)], axis=0)
            s_l = _dot_nt(qs, k_lat) + bias
            s_x = _dot_nt(qs, kx_ref[:, ksl])
            sink = jnp.zeros((rows, 1), F32)
            for g in range(A_GROUP):
                sink = jnp.where(row_head == g, sink_ref[k * A_GROUP + g] * LOG2E, sink)
            m = jnp.maximum(jnp.maximum(jnp.max(s_l, axis=-1, keepdims=True), jnp.max(s_x, axis=-1, keepdims=True)),
                            sink)
            e_l = jnp.exp2(s_l - m)
            e_x = jnp.exp2(s_x - m)
            den = jnp.sum(e_l, axis=-1, keepdims=True) + jnp.sum(e_x, axis=-1, keepdims=True) + jnp.exp2(sink - m)
            o = (_dot(e_l.astype(BF16), v_lat) + _dot(e_x.astype(BF16), vx_ref[:, ksl])) / den
            for j in range(A_GROUP // 2):
                even = o[(2 * j) * BLOCK:(2 * j + 1) * BLOCK]
                odd = o[(2 * j + 1) * BLOCK:(2 * j + 2) * BLOCK]
                o_ref[qrows, k * 2 * LANES + j * LANES:k * 2 * LANES + (j + 1) * LANES] = (
                    jnp.where(lo, even, odd).astype(BF16))


def _window_call(sink, qa, ka, va, *, batch, seq, ctx_len, latent):
    nb = seq // BLOCK
    wq = WIN_QB if latent else ctx_len // BLOCK
    steps = nb // wq if latent else 1
    assert nb % wq == 0
    ctx0 = batch * seq // ctx_len
    q0 = 0 if latent else batch * seq // (wq * BLOCK)

    def qi(b, i):
        return (q0 + b * steps + i, 0)

    def edge(off):
        return lambda b, i: (b * nb + (jnp.clip(wq * i + off, 0, nb - 1) if latent else 0), 0)

    def run(b, i):
        return (b * (nb // wq) + (i if latent else 0), 0)

    def xi(b, i):
        return (ctx0 + b, 0)

    kv_specs = [pl.BlockSpec((BLOCK, 2 * LANES), edge(-1)), pl.BlockSpec((wq * BLOCK, 2 * LANES), run),
                pl.BlockSpec((BLOCK, 2 * LANES), edge(wq)), pl.BlockSpec((ctx_len, 2 * LANES), xi)]
    return pl.pallas_call(
        functools.partial(_window_kernel, nb=nb, n_off=0 if latent else nb),
        grid=(batch, steps),
        in_specs=[pl.BlockSpec(memory_space=pltpu.SMEM), pl.BlockSpec((BLOCK, 3 * BLOCK), lambda b, i: (0, 0)),
                  pl.BlockSpec((wq * BLOCK, 4 * LANES), qi)] + kv_specs + kv_specs,
        out_specs=pl.BlockSpec((wq * BLOCK, 4 * LANES), lambda b, i: (b * steps + i, 0)),
        out_shape=jax.ShapeDtypeStruct((batch * steps * wq * BLOCK, 4 * LANES), BF16),
        compiler_params=_params(("arbitrary", "arbitrary")),
        name="window" if latent else "window_ctx",
    )(sink, _window_band(), qa, ka, ka, ka, ka, va, va, va, va)


def _mla_kernel(q_ref, *refs, with_lat):
    if with_lat:
        kl_ref, kx_ref, vl_ref, vx_ref, o_ref, vaug_ref = refs
    else:
        kx_ref, vx_ref, o_ref, vaug_ref = refs
    n_ctx = vx_ref.shape[0]
    n_pairs = o_ref.shape[1] // LANES
    lo = lax.broadcasted_iota(jnp.int32, (o_ref.shape[0], LANES), 1) < B_V

    @pl.when(pl.program_id(2) == 0)
    def _():
        one = jnp.ones((1, LANES), BF16)
        for v_ref, start in ((vx_ref, 0),) + (((vl_ref, n_ctx),) if with_lat else ()):
            n = v_ref.shape[0]
            keep = lax.broadcasted_iota(jnp.int32, (n, LANES), 1) < B_V
            for pp in range(n_pairs):
                v = v_ref[:, pp * LANES:(pp + 1) * LANES]
                vaug_ref[2 * pp, start:start + n, :] = jnp.where(keep, v, one)
                vaug_ref[2 * pp + 1, start:start + n, :] = jnp.where(keep, one, v)

    def head(h):
        sl = slice(h * LANES, (h + 1) * LANES)
        q = q_ref[:, sl]
        s_x = _dot_nt(q, kx_ref[:, sl])
        m = jnp.max(s_x, axis=-1, keepdims=True)
        if with_lat:
            s_l = _dot_nt(q, kl_ref[:, sl])
            m = jnp.maximum(m, jnp.max(s_l, axis=-1, keepdims=True))
        o = _dot(jnp.exp2(s_x - m).astype(BF16), vaug_ref[h, :n_ctx, :])
        if with_lat:
            o = o + _dot(jnp.exp2(s_l - m).astype(BF16), vaug_ref[h, n_ctx:, :])
        return o / pltpu.roll(o, B_V, 1)

    for pp in range(n_pairs):
        o_ref[:, pp * LANES:(pp + 1) * LANES] = jnp.where(lo, head(2 * pp), head(2 * pp + 1)).astype(BF16)


def _mla_call(qb, kb, vb, *, batch, seq, ctx_len, latent):
    ctx0 = batch * seq // ctx_len
    groups = B_HEADS // (2 * MLA_PAIRS)
    qk_w = MLA_PAIRS * 2 * LANES
    v_w = MLA_PAIRS * LANES
    tq = TQ if latent else ctx_len
    nq = seq // TQ if latent else 1
    kx_spec = pl.BlockSpec((ctx_len, qk_w), lambda b, p, j: (ctx0 + b, p))
    vx_spec = pl.BlockSpec((ctx_len, v_w), lambda b, p, j: (ctx0 + b, p))
    if latent:
        in_specs = [pl.BlockSpec((tq, qk_w), lambda b, p, j: (b * nq + j, p)),
                    pl.BlockSpec((seq, qk_w), lambda b, p, j: (b, p)), kx_spec,
                    pl.BlockSpec((seq, v_w), lambda b, p, j: (b, p)), vx_spec]
        args = (qb, kb, kb, vb, vb)
    else:
        in_specs = [pl.BlockSpec((tq, qk_w), lambda b, p, j: (ctx0 + b, p)), kx_spec, vx_spec]
        args = (qb, kb, vb)
    return pl.pallas_call(
        functools.partial(_mla_kernel, with_lat=latent),
        grid=(batch, groups, nq),
        in_specs=in_specs,
        out_specs=pl.BlockSpec((tq, v_w), lambda b, p, j: (b * nq + j, p)),
        out_shape=jax.ShapeDtypeStruct((batch * nq * tq, groups * v_w), BF16),
        scratch_shapes=[pltpu.VMEM((2 * MLA_PAIRS, ctx_len + (seq if latent else 0), LANES), BF16)],
        compiler_params=_params(("arbitrary", "arbitrary", "arbitrary")),
        name="mla" if latent else "mla_ctx",
    )(*args)


def _top2(v):
    i1 = jnp.zeros_like(v[0])
    m1 = v[0]
    for i in range(1, EPG):
        u = v[i] > m1
        i1 = jnp.where(u, float(i), i1)
        m1 = jnp.where(u, v[i], m1)
    i2 = jnp.zeros_like(v[0])
    m2 = jnp.full_like(v[0], -jnp.inf)
    for i in range(EPG):
        cand = jnp.where(i1 == float(i), -jnp.inf, v[i])
        u = cand > m2
        i2 = jnp.where(u, float(i), i2)
        m2 = jnp.where(u, cand, m2)
    return i1, i2, m1, m2


def _pick(idx, vals):
    out = vals[0]
    for i in range(1, len(vals)):
        out = jnp.where(idx == float(i), vals[i], out)
    return out


def _merge_kernel(x_ref, ya_ref, yb_ref, yax_ref, ybx_ref, ga_ref, gb_ref, mod_ref, nw_ref, woa_ref, wob_ref, wout_ref,
                  wr2_ref, rb_ref, xo_ref, h_ref, meta_ref, *, n_lat_tiles):
    ya, yb = ya_ref[...], yb_ref[...]
    if yax_ref is not None:
        is_lat = pl.program_id(0) < n_lat_tiles
        ya = jnp.where(is_lat, ya, yax_ref[...])
        yb = jnp.where(is_lat, yb, ybx_ref[...])
    a = _dot(ya, woa_ref[...])
    b = _dot(yb, wob_ref[...])
    mix = (ga_ref[...].astype(F32) * a + gb_ref[...].astype(F32) * b).astype(BF16)
    x = x_ref[...] + mod_ref[0, 2:3, :] * _dot(mix, wout_ref[...])
    xo_ref[...] = x
    h = _rms(x, nw_ref[...]) * (1.0 + mod_ref[0, 4:5, :]) + mod_ref[0, 3:4, :]
    _store_token_rows(h_ref, h)

    h_hi = h.astype(BF16)
    h_lo = (h - h_hi.astype(F32)).astype(BF16)
    l_hi = _dot(h_hi, wr2_ref[...])
    logits = l_hi[:, :LANES] + l_hi[:, LANES:] + _dot(h_lo, wr2_ref[:, :LANES])
    sc = jax.nn.sigmoid(logits.T[:N_EXPERTS, :])
    sel = sc + rb_ref[...]
    sel_rows = [sel[e:e + 1, :] for e in range(N_EXPERTS)]
    sc_rows = [sc[e:e + 1, :] for e in range(N_EXPERTS)]
    best = jnp.zeros_like(sel_rows[0])
    best_v = None
    for g in range(N_GROUPS):
        _, _, m1, m2 = _top2(sel_rows[g * EPG:(g + 1) * EPG])
        gv = m1 + m2
        if best_v is None:
            best_v = gv
        else:
            u = gv > best_v
            best = jnp.where(u, float(g), best)
            best_v = jnp.where(u, gv, best_v)
    sel_g = [_pick(best, [sel_rows[g * EPG + i] for g in range(N_GROUPS)]) for i in range(EPG)]
    sc_g = [_pick(best, [sc_rows[g * EPG + i] for g in range(N_GROUPS)]) for i in range(EPG)]
    i1, i2, _, _ = _top2(sel_g)
    s1 = _pick(i1, sc_g)
    s2 = _pick(i2, sc_g)
    tot = s1 + s2
    first_low = i1 < i2
    e_lo = jnp.where(first_low, i1, i2)
    e_hi = jnp.where(first_low, i2, i1)
    w_lo = jnp.where(first_low, s1, s2) / tot
    w_hi = jnp.where(first_low, s2, s1) / tot
    pid = jnp.where(e_lo == 0.0, e_hi - 1.0, jnp.where(e_lo == 1.0, e_hi + 1.0, 5.0))
    bucket = best * float(len(PAIRS)) + pid
    t = sel.shape[1]
    meta_ref[0] = jnp.concatenate([w_lo, w_hi, bucket, jnp.zeros((SUB - 3, t), F32)], axis=0)


def _merge_call(x, ya, yb, y_ctx, ga, gb, mod, nw, w, wr2, rb, *, n_rows, n_lat, seq):
    d = x.shape[1]
    tm = TM_MERGE
    nt = n_rows // tm
    n_lat_t = n_lat // tm
    tps = seq // tm
    ctx_row = n_lat // seq

    def tok(i):
        return (i, 0)

    def full(a):
        return pl.BlockSpec(a.shape, lambda i: (0,) * a.ndim)

    y_specs = [pl.BlockSpec((tm, 512), lambda i: (jnp.minimum(i, n_lat_t - 1), 0))] * 2
    y_args = [ya, yb]
    if y_ctx is not None:
        y_specs += [pl.BlockSpec((tm, 512), lambda i: (jnp.maximum(i - n_lat_t, 0), 0))] * 2
        y_args += list(y_ctx)

    def body(x_ref, ya_ref, yb_ref, *refs):
        if y_ctx is None:
            refs = (None, None) + refs
        _merge_kernel(x_ref, ya_ref, yb_ref, *refs, n_lat_tiles=n_lat_t)

    return pl.pallas_call(
        body,
        grid=(nt,),
        in_specs=[pl.BlockSpec((tm, d), tok)] + y_specs + [
                  pl.BlockSpec((tm, d), tok), pl.BlockSpec((tm, d), tok),
                  pl.BlockSpec((1, 8, d), lambda i: (jnp.where(i < n_lat_t, i // tps, ctx_row), 0, 0)),
                  full(nw), full(w["woa"]), full(w["wob"]), full(w["wout"]), full(wr2), full(rb)],
        out_specs=[pl.BlockSpec((tm, d), tok), pl.BlockSpec((tm * SUB, LANES), tok),
                   pl.BlockSpec((1, SUB, tm), lambda i: (i, 0, 0))],
        out_shape=[jax.ShapeDtypeStruct((n_rows, d), F32), jax.ShapeDtypeStruct((n_rows * SUB, LANES), F32),
                   jax.ShapeDtypeStruct((nt, SUB, tm), F32)],
        compiler_params=_params(("arbitrary",)),
        name="merge",
    )(x, *y_args, ga, gb, mod, nw, w["woa"], w["wob"], w["wout"], wr2, rb)


def _moe_kernel(tok_ref, e1_ref, e2_ref, nv_ref, wts_ref, h_hbm, wg1_ref, wu1_ref, wd1_ref, wg2_ref, wu2_ref, wd2_ref,
                f_hbm, hbuf, obuf, wgu_ref, wdn_ref, gsem, ssem, *, n_tok, n_tiles):
    t = pl.program_id(0)
    slot = t % MOE_BUFS
    slot1 = (t + 1) % MOE_BUFS
    slot2 = (t + 2) % MOE_BUFS
    live = nv_ref[t] > 0
    prev_live = jnp.logical_and(t >= 1, nv_ref[jnp.maximum(t - 1, 0)] > 0)

    t_prev = jnp.maximum(t - 1, 0)
    new_pair = jnp.logical_or(t == 0, jnp.logical_or(e1_ref[t] != e1_ref[t_prev], e2_ref[t] != e2_ref[t_prev]))

    @pl.when(jnp.logical_and(live, new_pair))
    def _():
        for i, w_ref in enumerate((wg1_ref, wu1_ref, wg2_ref, wu2_ref)):
            wgu_ref[i] = w_ref[0, 0].astype(BF16)
        for i, w_ref in enumerate((wd1_ref, wd2_ref)):
            wdn_ref[i] = w_ref[0, 0].astype(BF16)

    def gather_row(tile, s, r):
        tk = jnp.minimum(tok_ref[tile * TE + r], n_tok - 1)
        return pltpu.make_async_copy(h_hbm.at[pl.ds(tk * SUB, SUB)], hbuf.at[s, pl.ds(r * SUB, SUB)], gsem.at[s])

    def scatter_row(tile, s, r):
        tk = tok_ref[tile * TE + r]
        return pltpu.make_async_copy(obuf.at[s, pl.ds(r * SUB, SUB)], f_hbm.at[pl.ds(tk * SUB, SUB)], ssem.at[s])

    def wait_gather(s):
        pltpu.make_async_copy(h_hbm.at[pl.ds(0, TE * SUB)], hbuf.at[s], gsem.at[s]).wait()

    def wait_scatter(s):
        pltpu.make_async_copy(obuf.at[s], f_hbm.at[pl.ds(0, TE * SUB)], ssem.at[s]).wait()

    def start_rows(make_row, tile, s):
        def body(i, carry):
            make_row(tile, s, 2 * i).start(priority=0)
            make_row(tile, s, 2 * i + 1).start(priority=1)
            return carry
        lax.fori_loop(0, TE // 2, body, 0, unroll=4)

    @pl.when(t == 0)
    def _():
        obuf[...] = jnp.zeros(obuf.shape, F32)
        for half in range(2):
            fill = pltpu.make_async_copy(obuf.at[0], f_hbm.at[pl.ds((n_tok + half * TE) * SUB, TE * SUB)], ssem.at[0])
            fill.start()
            fill.wait()
        start_rows(gather_row, 0, 0)
        start_rows(gather_row, 1, 1)

    @pl.when(live)
    def _():
        wait_gather(slot)

        @pl.when(t >= 2)
        def _():
            wait_scatter(slot)

    @pl.when(live)
    def _():
        g_tile = jnp.minimum(t + 2, n_tiles - 1)
        s_tile = jnp.where(t == 0, n_tiles - 1, t - 1)
        issue = ([functools.partial(gather_row, g_tile, slot2, r) for r in range(TE)]
                 + [functools.partial(scatter_row, s_tile, slot2, r) for r in range(TE)])
        n_stage = 6
        per_stage = -(-len(issue) // n_stage)

        def issue_stage(k):
            for i, make in enumerate(issue[k * per_stage:(k + 1) * per_stage]):
                make().start(priority=i % 2)

        h = _load_token_rows(hbuf.at[slot], TE).astype(BF16)
        w_lo = jnp.broadcast_to(wts_ref[0, 0:1, :], (LANES, TE)).T[:, :1]
        w_hi = jnp.broadcast_to(wts_ref[0, 1:2, :], (LANES, TE)).T[:, :1]
        outs = []
        stage = 0
        for which, wt in ((0, w_lo), (1, w_hi)):
            hg = _dot(h, wgu_ref[2 * which])
            issue_stage(stage)
            hu = _dot(h, wgu_ref[2 * which + 1])
            issue_stage(stage + 1)
            act = (hg * jax.nn.sigmoid(hg) * hu).astype(BF16)
            outs.append(wt * _dot(act, wdn_ref[which]))
            issue_stage(stage + 2)
            stage += 3
        _store_token_rows(obuf.at[slot], outs[0] + outs[1])

    @pl.when(jnp.logical_and(jnp.logical_not(live), prev_live))
    def _():
        wait_gather(slot)
        wait_gather(slot1)
        wait_scatter(slot1)

        @pl.when(t >= 2)
        def _():
            wait_scatter(slot)
        start_rows(scatter_row, t - 1, slot2)
        wait_scatter(slot2)


def _moe_call(tok, e1, e2, nv, wts, h3, wg, wu, wd, *, layer, n_tok):
    d = wg.shape[2]
    de = wg.shape[3]
    n_tiles = e1.shape[0]

    def wspec(shape, which):
        return pl.BlockSpec((1, 1) + shape, lambda t, tok, e1, e2, nv: (layer, (e1, e2)[which][t], 0, 0))

    grid_spec = pltpu.PrefetchScalarGridSpec(
        num_scalar_prefetch=4,
        grid=(n_tiles,),
        in_specs=[pl.BlockSpec((1, 2, TE), lambda t, tok, e1, e2, nv: (t, 0, 0)),
                  pl.BlockSpec(memory_space=pl.ANY),
                  wspec((d, de), 0), wspec((d, de), 0), wspec((de, d), 0),
                  wspec((d, de), 1), wspec((d, de), 1), wspec((de, d), 1)],
        out_specs=pl.BlockSpec(memory_space=pl.ANY),
        scratch_shapes=[pltpu.VMEM((MOE_BUFS, TE * SUB, LANES), F32), pltpu.VMEM((MOE_BUFS, TE * SUB, LANES), F32),
                        pltpu.VMEM((4, d, de), BF16), pltpu.VMEM((2, de, d), BF16),
                        pltpu.SemaphoreType.DMA((MOE_BUFS,)), pltpu.SemaphoreType.DMA((MOE_BUFS,))],
    )
    return pl.pallas_call(
        functools.partial(_moe_kernel, n_tok=n_tok, n_tiles=n_tiles),
        grid_spec=grid_spec,
        out_shape=jax.ShapeDtypeStruct(((n_tok + 2 * TE) * SUB, LANES), F32),
        compiler_params=_params(("arbitrary",)),
        name="moe",
    )(tok, e1, e2, nv, wts, h3, wg, wu, wd, wg, wu, wd)


def _route_tables(meta, n_tok):
    w_lo = meta[:, 0, :].reshape(-1)
    w_hi = meta[:, 1, :].reshape(-1)
    bucket = meta[:, 2, :].reshape(-1).astype(jnp.int32)
    n_slots = n_tok + N_BUCKETS * TE
    n_tiles = n_slots // TE
    buckets = jnp.arange(N_BUCKETS, dtype=jnp.int32)
    counts = jnp.sum(bucket[:, None] == buckets[None, :], axis=0, dtype=jnp.int32)
    padded = ((counts + TE - 1) // TE) * TE
    pad_end = jnp.cumsum(padded)
    pad_start = pad_end - padded
    fill_i = jnp.arange(TE, dtype=jnp.int32)[None, :]
    fill_key = jnp.where(fill_i < (padded - counts)[:, None], 2 * buckets[:, None] + 1, 2 * N_BUCKETS).reshape(-1)
    keys = jnp.concatenate([2 * bucket, fill_key])
    ids = jnp.concatenate([jnp.arange(n_tok, dtype=jnp.int32), jnp.full((N_BUCKETS * TE,), -1, jnp.int32)])
    zeros = jnp.zeros((N_BUCKETS * TE,), F32)
    _, ids, w_lo, w_hi = lax.sort((keys, ids, jnp.concatenate([w_lo, zeros]), jnp.concatenate([w_hi, zeros])),
                                  num_keys=1, is_stable=True)
    slot = jnp.arange(n_slots, dtype=jnp.int32)
    tok = jnp.where(ids < 0, n_tok + ((slot // TE) % 2) * TE + slot % TE, ids)
    wts = jnp.stack([w_lo.reshape(n_tiles, TE), w_hi.reshape(n_tiles, TE)], axis=1)

    tile_start = jnp.arange(n_tiles, dtype=jnp.int32) * TE
    tb = jnp.sum(tile_start[:, None] >= pad_end[None, :], axis=1, dtype=jnp.int32)
    used = tb < N_BUCKETS
    onehot = (tb[:, None] == buckets[None, :]).astype(jnp.int32)
    nv = jnp.clip(jnp.sum(onehot * (counts + pad_start)[None, :], axis=1) - tile_start, 0, TE) * used
    tbe = jnp.where(used, tb, jnp.max(jnp.where(used, tb, 0)))
    pid = tbe % len(PAIRS)
    pair_lo = (pid >= 3).astype(jnp.int32) + (pid >= 5).astype(jnp.int32)
    pair_hi = pid + 1 - 2 * (pid >= 3).astype(jnp.int32) - (pid >= 5).astype(jnp.int32)
    e1 = (tbe // len(PAIRS)) * EPG + pair_lo
    e2 = (tbe // len(PAIRS)) * EPG + pair_hi
    return tok.astype(jnp.int32), e1.astype(jnp.int32), e2.astype(jnp.int32), nv.astype(jnp.int32), wts


def _final_kernel(x_ref, f_ref, mod_ref, nw_ref, o_ref):
    x = x_ref[...] + mod_ref[0, 5:6, :] * _load_token_rows(f_ref, x_ref.shape[0])
    o_ref[...] = _rms(x, nw_ref[...])


def _final_call(x, f, mod, nw, *, n_lat, seq):
    d = x.shape[1]
    tm = TM_MERGE
    tps = seq // tm
    return pl.pallas_call(
        _final_kernel,
        grid=(n_lat // tm,),
        in_specs=[pl.BlockSpec((tm, d), lambda i: (i, 0)), pl.BlockSpec((tm * SUB, LANES), lambda i: (i, 0)),
                  pl.BlockSpec((1, 8, d), lambda i: (i // tps, 0, 0)), pl.BlockSpec((1, d), lambda i: (0, 0))],
        out_specs=pl.BlockSpec((tm, d), lambda i: (i, 0)),
        out_shape=jax.ShapeDtypeStruct((n_lat, d), F32),
        compiler_params=_params(("arbitrary",)),
        name="final",
    )(x, f, mod, nw)


def _rope_table(pos_r, pos_c, dim, lane_of):
    d = dim // 2
    half = d // 2
    lane = np.arange(LANES)
    rl = lane_of(lane)
    is_rope = rl >= 0
    rl = np.maximum(rl, 0)
    use_col = rl >= d
    j = rl % half
    first = (rl % d) < half
    inv = (np.float32(ROPE_BASE) ** (-(2.0 * j).astype(np.float32) / np.float32(d))).astype(np.float32)
    pos = np.where(use_col[None, :], pos_c[:, None], pos_r[:, None]).astype(np.float32)
    ang = pos * inv[None, :]
    cos = np.where(is_rope[None, :], np.cos(ang), 1.0)
    sin = np.where(is_rope[None, :], np.sin(ang), 0.0)
    return np.stack([cos, np.where(first[None, :], -sin, 0.0), np.where(first[None, :], 0.0, sin)]).astype(np.float32)


def _tables(seq):
    t = np.arange(seq)
    rows, cols = t // GRID_W, t % GRID_W
    tab_a = _rope_table(rows, cols, A_HEAD_DIM, lambda lane: lane % A_HEAD_DIM)
    tab_b = _rope_table(rows, cols, B_ROPE,
                        lambda lane: np.where((lane >= B_NOPE) & (lane < B_NOPE + B_ROPE), lane - B_NOPE, -1))
    ident = np.stack([np.ones((TM, LANES), np.float32), np.zeros((TM, LANES), np.float32),
                      np.zeros((TM, LANES), np.float32)])
    return jnp.asarray(np.concatenate([tab_a, ident], axis=1)), jnp.asarray(np.concatenate([tab_b, ident], axis=1))


def _window_band():
    r = np.arange(BLOCK)[:, None]
    c = np.arange(3 * BLOCK)[None, :]
    dist = c - r
    return jnp.asarray(np.where((dist >= BLOCK - WINDOW) & (dist <= BLOCK + WINDOW), 0.0, NEG).astype(np.float32))


def _layer_weights(w_in, w_uq, w_ukv, q_norm, kv_norm, w_o_a, w_o_b, w_out):
    n_mix = W1_KR + B_ROPE
    w1 = jnp.pad(w_in[:, :n_mix], ((0, 0), (0, W1_COLS - n_mix)))
    g_w = w_in[:, n_mix:]
    wuq = jnp.pad(w_uq.reshape(B_Q_RANK, B_HEADS, B_NOPE + B_ROPE),
                  ((0, 0), (0, 0), (0, LANES - B_NOPE - B_ROPE))).reshape(B_Q_RANK, B_HEADS * LANES)
    ukv = w_ukv.reshape(B_KV_RANK, B_HEADS, B_NOPE + B_V)
    wukn = jnp.pad(ukv[:, :, :B_NOPE], ((0, 0), (0, 0), (0, LANES - B_NOPE))).reshape(B_KV_RANK, B_HEADS * LANES)
    wuv = ukv[:, :, B_NOPE:].reshape(B_KV_RANK, B_HEADS * B_V)
    src = jnp.arange(LANES)[:, None]
    dst = jnp.arange(B_HEADS * LANES)[None, :]
    rp = ((src < B_ROPE) & (dst % LANES == src + B_NOPE)).astype(BF16)
    return dict(w1=w1.astype(BF16), wg=g_w.astype(BF16), qn=q_norm.reshape(1, -1), wuq=wuq.astype(BF16),
                kvn=kv_norm.reshape(1, -1), wukn=wukn.astype(BF16), wuv=wuv.astype(BF16), rp=rp,
                woa=w_o_a.astype(BF16), wob=w_o_b.astype(BF16), wout=w_out.astype(BF16))


def kernel(x, c, ctx, c_ctx, w_mod, b_mod, norm_mix, norm_ffn, w_in, attn_sink, mla_q_norm, w_uq, mla_kv_norm, w_ukv,
           w_o_a, w_o_b, w_out, w_router, router_bias, w_expert_gate, w_expert_up, w_expert_down, final_norm):
    batch, seq, d = x.shape
    ctx_len = ctx.shape[1]
    depth = w_mod.shape[0]
    n_lat = batch * seq
    n_all = n_lat + batch * ctx_len
    for tile in (TM, TM_MERGE):
        assert seq % tile == 0 and (batch * ctx_len) % tile == 0
    assert d == SUB * LANES and seq % TQ == 0 and seq % GRID_W == 0 and batch + 1 <= MOD_ROWS

    c_all = jnp.concatenate([c, c_ctx[None, :], jnp.zeros((MOD_ROWS - batch - 1, d), F32)], axis=0)
    mod = _mod_call(c_all, w_mod, b_mod)
    mod = jnp.pad(mod.transpose(0, 2, 1, 3), ((0, 0), (0, 0), (0, 2), (0, 0)))

    tab_a, tab_b = _tables(seq)
    wr = jnp.pad(w_router.astype(F32), ((0, 0), (0, LANES - N_EXPERTS)))
    wr_hi = wr.astype(BF16)
    wr2 = jnp.concatenate([wr_hi, (wr - wr_hi.astype(F32)).astype(BF16)], axis=1)
    rb = router_bias.astype(F32).reshape(N_EXPERTS, 1)

    xs = (x.reshape(n_lat, d), ctx.reshape(batch * ctx_len, d))
    f = None
    for l in range(depth):
        need_ctx = l < depth - 1
        w = _layer_weights(w_in[l], w_uq[l], w_ukv[l], mla_q_norm[l], mla_kv_norm[l], w_o_a[l], w_o_b[l], w_out[l])
        xs, qa, ka, va, qb, kb, vb, ga, gb = _proj_call(
            xs, f, mod[l - 1] if l else None, mod[l], norm_mix[l].reshape(1, d), tab_a, tab_b, w, n_lat=n_lat, seq=seq)
        dims = dict(batch=batch, seq=seq, ctx_len=ctx_len)
        sink = attn_sink[l].astype(F32)
        ya = _window_call(sink, qa, ka, va, latent=True, **dims)
        yb = _mla_call(qb, kb, vb, latent=True, **dims)
        y_ctx = None
        if need_ctx:
            y_ctx = (_window_call(sink, qa, ka, va, latent=False, **dims), _mla_call(qb, kb, vb, latent=False, **dims))
        n_rows = n_all if need_ctx else n_lat
        xs, h3, meta = _merge_call(xs, ya, yb, y_ctx, ga, gb, mod[l], norm_ffn[l].reshape(1, d), w, wr2, rb,
                                   n_rows=n_rows, n_lat=n_lat, seq=seq)
        tok, e1, e2, nv, wts = _route_tables(meta, n_rows)
        f = _moe_call(tok, e1, e2, nv, wts, h3, w_expert_gate, w_expert_up, w_expert_down, layer=l, n_tok=n_rows)
    out = _final_call(xs, f, mod[depth - 1], final_norm.reshape(1, d), n_lat=n_lat, seq=seq)
    return out.reshape(batch, seq, d)
```

```python
import functools

import numpy as np
import jax
import jax.numpy as jnp
from jax import lax
from jax.experimental import pallas as pl
from jax.experimental.pallas import tpu as pltpu

F32 = jnp.float32
BF16 = jnp.bfloat16

EPS = 1e-6
ROPE_BASE = 10000.0
GRID_W = 64
BLOCK = 128
A_HEADS, A_KV_HEADS, A_HEAD_DIM = 8, 2, 64
A_GROUP = A_HEADS // A_KV_HEADS
WINDOW = 128
B_HEADS, B_NOPE, B_ROPE, B_V = 8, 64, 32, 64
B_Q_RANK, B_KV_RANK = 256, 128
N_EXPERTS, N_GROUPS = 16, 4
EPG = N_EXPERTS // N_GROUPS
D_EXPERT = 512
PAIRS = ((0, 1), (0, 2), (0, 3), (1, 2), (1, 3), (2, 3))
N_BUCKETS = N_GROUPS * len(PAIRS)

LANES = 128
SUB = 8
TM = 512
TM_MERGE = 512
TQ = 512
MLA_PAIRS = 4
WIN_QB = 8
TE = 256
MOE_BUFS = 3
MOD_ROWS = 24
NEG = -1e30
LOG2E = 1.4426950408889634
VMEM_LIMIT = 56 * 1024 * 1024

W1_QA, W1_KA, W1_VA, W1_CQ, W1_CKV, W1_KR = 0, 512, 640, 768, 1024, 1152
W1_COLS = 1280


def _params(sem, vmem=VMEM_LIMIT):
    return pltpu.CompilerParams(dimension_semantics=sem, vmem_limit_bytes=vmem)


def _dot(a, b):
    return jnp.dot(a, b, preferred_element_type=F32)


def _dot_nt(a, b):
    return lax.dot_general(a, b, (((1,), (1,)), ((), ())), preferred_element_type=F32)


def _rms(x, g):
    return x * lax.rsqrt(jnp.mean(x * x, axis=-1, keepdims=True) + EPS) * g


def _lane_tile(t, width):
    return jnp.concatenate([t] * (width // LANES), axis=1)


def _rope(t, tab_ref, shift):
    w = t.shape[-1]
    up = pltpu.roll(t, w - shift, 1)
    dn = pltpu.roll(t, shift, 1)
    return (t * _lane_tile(tab_ref[0], w) + up * _lane_tile(tab_ref[1], w)
            + dn * _lane_tile(tab_ref[2], w))


def _dup_kv_heads(t):
    swapped = pltpu.roll(t, A_HEAD_DIM, 1)
    lo = lax.broadcasted_iota(jnp.int32, t.shape, 1) < A_HEAD_DIM
    return jnp.concatenate([jnp.where(lo, t, swapped), jnp.where(lo, swapped, t)], axis=1)


def _load_token_rows(ref, rows):
    return jnp.concatenate([ref[pl.ds(s, rows, stride=SUB), :] for s in range(SUB)], axis=1)


def _store_token_rows(ref, val):
    rows = val.shape[0]
    for s in range(SUB):
        ref[pl.ds(s, rows, stride=SUB), :] = val[:, s * LANES:(s + 1) * LANES]


def _mod_kernel(c_ref, w_ref, b_ref, o_ref):
    c = c_ref[...]
    a = (c * jax.nn.sigmoid(c)).astype(BF16)
    o_ref[0, 0] = _dot(a, w_ref[0].astype(BF16)) + b_ref[0, 0]


def _mod_call(c_all, w_mod, b_mod):
    depth, d, _ = w_mod.shape
    return pl.pallas_call(
        _mod_kernel,
        grid=(depth, 6),
        in_specs=[pl.BlockSpec((MOD_ROWS, d), lambda l, k: (0, 0)),
                  pl.BlockSpec((1, d, d), lambda l, k: (l, 0, k)),
                  pl.BlockSpec((1, 1, 1, d), lambda l, k: (l, k, 0, 0))],
        out_specs=pl.BlockSpec((1, 1, MOD_ROWS, d), lambda l, k: (l, k, 0, 0)),
        out_shape=jax.ShapeDtypeStruct((depth, 6, MOD_ROWS, d), F32),
        compiler_params=_params(("arbitrary", "arbitrary")),
        name="mod",
    )(c_all, w_mod, b_mod.reshape(depth, 6, 1, d))


def _proj_kernel(*refs, has_f, n_lat_tiles):
    if has_f:
        x_ref, f_ref, modp_ref, refs = refs[0], refs[1], refs[2], refs[3:]
    else:
        x_ref, xc_ref, refs = refs[0], refs[1], refs[2:]
    (mod_ref, nw_ref, ta_ref, tb_ref, w1_ref, wg_ref, qn_ref, wuq_ref, kvn_ref, wukn_ref, wuv_ref,
     rp_ref), refs = refs[:12], refs[12:]
    xo_ref, qa_ref, ka_ref, va_ref, qb_ref, kb_ref, vb_ref, ga_ref, gb_ref = refs

    if has_f:
        x = x_ref[...] + modp_ref[0, 5:6, :] * _load_token_rows(f_ref, x_ref.shape[0])
    else:
        x = jnp.where(pl.program_id(0) < n_lat_tiles, x_ref[...], xc_ref[...])
    xo_ref[...] = x
    h = (_rms(x, nw_ref[...]) * (1.0 + mod_ref[0, 1:2, :]) + mod_ref[0, 0:1, :]).astype(BF16)

    t = _dot(h, w1_ref[...])
    qa_ref[...] = (_rope(t[:, W1_QA:W1_KA], ta_ref, 16) * (A_HEAD_DIM ** -0.5 * LOG2E)).astype(BF16)
    ka_ref[...] = _dup_kv_heads(_rope(t[:, W1_KA:W1_VA], ta_ref, 16)).astype(BF16)
    va_ref[...] = _dup_kv_heads(t[:, W1_VA:W1_CQ]).astype(BF16)

    cq = _rms(t[:, W1_CQ:W1_CKV], qn_ref[...]).astype(BF16)
    qb = _rope(_dot(cq, wuq_ref[...]), tb_ref, 8)
    qb_ref[...] = (qb * ((B_NOPE + B_ROPE) ** -0.5 * LOG2E)).astype(BF16)

    ckv = _rms(t[:, W1_CKV:W1_KR], kvn_ref[...]).astype(BF16)
    vb_ref[...] = _dot(ckv, wuv_ref[...]).astype(BF16)
    kr = t[:, W1_KR:W1_COLS]
    kr_hi = kr.astype(BF16)
    kr_lo = (kr - kr_hi.astype(F32)).astype(BF16)
    kb = _dot(ckv, wukn_ref[...]) + _dot(kr_hi, rp_ref[...]) + _dot(kr_lo, rp_ref[...])
    kb_ref[...] = _rope(kb, tb_ref, 8).astype(BF16)

    d = ga_ref.shape[-1]
    ga_ref[...] = jax.nn.sigmoid(_dot(h, wg_ref[:, :d])).astype(BF16)
    gb_ref[...] = jax.nn.sigmoid(_dot(h, wg_ref[:, d:])).astype(BF16)


def _proj_call(x, f, modp, mod, nw, tab_a, tab_b, w, *, n_lat, seq):
    has_f = f is not None
    n = x.shape[0] if has_f else x[0].shape[0] + x[1].shape[0]
    d = mod.shape[-1]
    nt = n // TM
    n_lat_t = n_lat // TM
    tps = seq // TM
    ctx_row = n_lat // seq

    def tok(i):
        return (i, 0)

    def modi(i):
        return (jnp.where(i < n_lat_t, i // tps, ctx_row), 0, 0)

    def tabi(i):
        return (0, jnp.where(i < n_lat_t, i % tps, tps), 0)

    def full(a):
        return pl.BlockSpec(a.shape, lambda i: (0,) * a.ndim, pipeline_mode=pl.Buffered(1))

    mod_spec = pl.BlockSpec((1, 8, d), modi)
    if has_f:
        in_specs = [pl.BlockSpec((TM, d), tok), pl.BlockSpec((TM * SUB, LANES), tok), mod_spec]
        args = [x, f, modp]
    else:
        in_specs = [pl.BlockSpec((TM, d), lambda i: (jnp.minimum(i, n_lat_t - 1), 0)),
                    pl.BlockSpec((TM, d), lambda i: (jnp.maximum(i - n_lat_t, 0), 0))]
        args = list(x)
    in_specs += [mod_spec, full(nw), pl.BlockSpec((3, TM, LANES), tabi), pl.BlockSpec((3, TM, LANES), tabi)]
    args += [mod, nw, tab_a, tab_b]
    for k in ("w1", "wg", "qn", "wuq", "kvn", "wukn", "wuv", "rp"):
        in_specs.append(full(w[k]))
        args.append(w[k])

    widths = (512, 256, 256, 1024, 1024, 512, d, d)
    out_shape = [jax.ShapeDtypeStruct((n, d), F32)] + [jax.ShapeDtypeStruct((n, wd), BF16) for wd in widths]
    out_specs = [pl.BlockSpec((TM, d), tok)] + [pl.BlockSpec((TM, wd), tok) for wd in widths]
    return pl.pallas_call(
        functools.partial(_proj_kernel, has_f=has_f, n_lat_tiles=n_lat_t),
        grid=(nt,), in_specs=in_specs, out_specs=out_specs, out_shape=out_shape,
        compiler_params=_params(("arbitrary",)),
        name="proj",
    )(*args)


def _window_kernel(sink_ref, band_ref, q_ref, kp_ref, kc_ref, kn_ref, kx_ref, vp_ref, vc_ref, vn_ref, vx_ref, o_ref,
                   *, nb, n_off):
    rows = A_GROUP * BLOCK
    wq = q_ref.shape[0] // BLOCK
    lo = lax.broadcasted_iota(jnp.int32, (BLOCK, LANES), 1) < A_HEAD_DIM
    col_blk = lax.broadcasted_iota(jnp.int32, (1, 3 * BLOCK), 1) // BLOCK
    row_head = lax.broadcasted_iota(jnp.int32, (rows, 1), 0) // BLOCK
    zero = jnp.zeros((BLOCK, LANES), BF16)

    def key_block(sub, rel, lanes, p_ref, c_ref, n_ref):
        j = sub + rel
        if j < 0:
            return p_ref[:, lanes]
        if j >= wq:
            return n_ref[:, lanes]
        return c_ref[j * BLOCK:(j + 1) * BLOCK, lanes]

    for sub in range(wq):
        n = wq * pl.program_id(1) + n_off + sub
        qrows = slice(sub * BLOCK, (sub + 1) * BLOCK)
        off_prev = jnp.where(jnp.logical_or(n == 0, n >= nb), NEG, 0.0)
        off_cur = jnp.where(n >= nb, NEG, 0.0)
        off_next = jnp.where(n >= nb - 1, NEG, 0.0)
        bias = band_ref[...] + jnp.where(col_blk == 0, off_prev, jnp.where(col_blk == 1, off_cur, off_next))
        bias = jnp.concatenate([bias] * A_GROUP, axis=0)
        for k in range(A_KV_HEADS):
            ksl = slice(k * LANES, (k + 1) * LANES)
            parts = []
            for j in range(A_GROUP // 2):
                blk = q_ref[qrows, k * 2 * LANES + j * LANES:k * 2 * LANES + (j + 1) * LANES]
                parts += [jnp.where(lo, blk, zero), jnp.where(lo, zero, blk)]
            qs = jnp.concatenate(parts, axis=0)
            k_lat = jnp.concatenate([key_block(sub, rel, ksl, kp_ref, kc_ref, kn_ref) for rel in (-1, 0, 1)], axis=0)
            v_lat = jnp.concatenate([key_block(sub, rel, ksl, vp_ref, vc_ref, vn_ref) for rel in (-1, 0, 1)], axis=0)
            s_l = _dot_nt(qs, k_lat) + bias
            s_x = _dot_nt(qs, kx_ref[:, ksl])
            sink = jnp.zeros((rows, 1), F32)
            for g in range(A_GROUP):
                sink = jnp.where(row_head == g, sink_ref[k * A_GROUP + g] * LOG2E, sink)
            m = jnp.maximum(jnp.maximum(jnp.max(s_l, axis=-1, keepdims=True), jnp.max(s_x, axis=-1, keepdims=True)),
                            sink)
            e_l = jnp.exp2(s_l - m)
            e_x = jnp.exp2(s_x - m)
            den = jnp.sum(e_l, axis=-1, keepdims=True) + jnp.sum(e_x, axis=-1, keepdims=True) + jnp.exp2(sink - m)
            o = (_dot(e_l.astype(BF16), v_lat) + _dot(e_x.astype(BF16), vx_ref[:, ksl])) / den
            for j in range(A_GROUP // 2):
                even = o[(2 * j) * BLOCK:(2 * j + 1) * BLOCK]
                odd = o[(2 * j + 1) * BLOCK:(2 * j + 2) * BLOCK]
                o_ref[qrows, k * 2 * LANES + j * LANES:k * 2 * LANES + (j + 1) * LANES] = (
                    jnp.where(lo, even, odd).astype(BF16))


def _window_call(sink, qa, ka, va, *, batch, seq, ctx_len, latent):
    nb = seq // BLOCK
    wq = min(WIN_QB, nb) if latent else ctx_len // BLOCK
    steps = nb // wq if latent else 1
    assert nb % wq == 0
    ctx0 = batch * seq // ctx_len
    q0 = 0 if latent else batch * seq // (wq * BLOCK)

    def qi(b, i):
        return (q0 + b * steps + i, 0)

    def edge(off):
        return lambda b, i: (b * nb + (jnp.clip(wq * i + off, 0, nb - 1) if latent else 0), 0)

    def run(b, i):
        return (b * (nb // wq) + (i if latent else 0), 0)

    def xi(b, i):
        return (ctx0 + b, 0)

    kv_specs = [pl.BlockSpec((BLOCK, 2 * LANES), edge(-1)), pl.BlockSpec((wq * BLOCK, 2 * LANES), run),
                pl.BlockSpec((BLOCK, 2 * LANES), edge(wq)), pl.BlockSpec((ctx_len, 2 * LANES), xi)]
    return pl.pallas_call(
        functools.partial(_window_kernel, nb=nb, n_off=0 if latent else nb),
        grid=(batch, steps),
        in_specs=[pl.BlockSpec(memory_space=pltpu.SMEM), pl.BlockSpec((BLOCK, 3 * BLOCK), lambda b, i: (0, 0)),
                  pl.BlockSpec((wq * BLOCK, 4 * LANES), qi)] + kv_specs + kv_specs,
        out_specs=pl.BlockSpec((wq * BLOCK, 4 * LANES), lambda b, i: (b * steps + i, 0)),
        out_shape=jax.ShapeDtypeStruct((batch * steps * wq * BLOCK, 4 * LANES), BF16),
        compiler_params=_params(("arbitrary", "arbitrary")),
        name="window" if latent else "window_ctx",
    )(sink, _window_band(), qa, ka, ka, ka, ka, va, va, va, va)


def _mla_kernel(q_ref, *refs, with_lat):
    if with_lat:
        kl_ref, kx_ref, vl_ref, vx_ref, o_ref, vaug_ref = refs
    else:
        kx_ref, vx_ref, o_ref, vaug_ref = refs
    n_ctx = vx_ref.shape[0]
    n_pairs = o_ref.shape[1] // LANES
    lo = lax.broadcasted_iota(jnp.int32, (o_ref.shape[0], LANES), 1) < B_V

    @pl.when(pl.program_id(2) == 0)
    def _():
        one = jnp.ones((1, LANES), BF16)
        for v_ref, start in ((vx_ref, 0),) + (((vl_ref, n_ctx),) if with_lat else ()):
            n = v_ref.shape[0]
            keep = lax.broadcasted_iota(jnp.int32, (n, LANES), 1) < B_V
            for pp in range(n_pairs):
                v = v_ref[:, pp * LANES:(pp + 1) * LANES]
                vaug_ref[2 * pp, start:start + n, :] = jnp.where(keep, v, one)
                vaug_ref[2 * pp + 1, start:start + n, :] = jnp.where(keep, one, v)

    def head(h):
        sl = slice(h * LANES, (h + 1) * LANES)
        q = q_ref[:, sl]
        s_x = _dot_nt(q, kx_ref[:, sl])
        m = jnp.max(s_x, axis=-1, keepdims=True)
        if with_lat:
            s_l = _dot_nt(q, kl_ref[:, sl])
            m = jnp.maximum(m, jnp.max(s_l, axis=-1, keepdims=True))
        o = _dot(jnp.exp2(s_x - m).astype(BF16), vaug_ref[h, :n_ctx, :])
        if with_lat:
            o = o + _dot(jnp.exp2(s_l - m).astype(BF16), vaug_ref[h, n_ctx:, :])
        return o / pltpu.roll(o, B_V, 1)

    for pp in range(n_pairs):
        o_ref[:, pp * LANES:(pp + 1) * LANES] = jnp.where(lo, head(2 * pp), head(2 * pp + 1)).astype(BF16)


def _mla_call(qb, kb, vb, *, batch, seq, ctx_len, latent):
    ctx0 = batch * seq // ctx_len
    groups = B_HEADS // (2 * MLA_PAIRS)
    qk_w = MLA_PAIRS * 2 * LANES
    v_w = MLA_PAIRS * LANES
    tq = TQ if latent else ctx_len
    nq = seq // TQ if latent else 1
    kx_spec = pl.BlockSpec((ctx_len, qk_w), lambda b, p, j: (ctx0 + b, p))
    vx_spec = pl.BlockSpec((ctx_len, v_w), lambda b, p, j: (ctx0 + b, p))
    if latent:
        in_specs = [pl.BlockSpec((tq, qk_w), lambda b, p, j: (b * nq + j, p)),
                    pl.BlockSpec((seq, qk_w), lambda b, p, j: (b, p)), kx_spec,
                    pl.BlockSpec((seq, v_w), lambda b, p, j: (b, p)), vx_spec]
        args = (qb, kb, kb, vb, vb)
    else:
        in_specs = [pl.BlockSpec((tq, qk_w), lambda b, p, j: (ctx0 + b, p)), kx_spec, vx_spec]
        args = (qb, kb, vb)
    return pl.pallas_call(
        functools.partial(_mla_kernel, with_lat=latent),
        grid=(batch, groups, nq),
        in_specs=in_specs,
        out_specs=pl.BlockSpec((tq, v_w), lambda b, p, j: (b * nq + j, p)),
        out_shape=jax.ShapeDtypeStruct((batch * nq * tq, groups * v_w), BF16),
        scratch_shapes=[pltpu.VMEM((2 * MLA_PAIRS, ctx_len + (seq if latent else 0), LANES), BF16)],
        compiler_params=_params(("arbitrary", "arbitrary", "arbitrary")),
        name="mla" if latent else "mla_ctx",
    )(*args)


def _top2(v):
    i1 = jnp.zeros_like(v[0])
    m1 = v[0]
    for i in range(1, EPG):
        u = v[i] > m1
        i1 = jnp.where(u, float(i), i1)
        m1 = jnp.where(u, v[i], m1)
    i2 = jnp.zeros_like(v[0])
    m2 = jnp.full_like(v[0], -jnp.inf)
    for i in range(EPG):
        cand = jnp.where(i1 == float(i), -jnp.inf, v[i])
        u = cand > m2
        i2 = jnp.where(u, float(i), i2)
        m2 = jnp.where(u, cand, m2)
    return i1, i2, m1, m2


def _pick(idx, vals):
    out = vals[0]
    for i in range(1, len(vals)):
        out = jnp.where(idx == float(i), vals[i], out)
    return out


def _merge_kernel(x_ref, ya_ref, yb_ref, yax_ref, ybx_ref, ga_ref, gb_ref, mod_ref, nw_ref, woa_ref, wob_ref, wout_ref,
                  wr2_ref, rb_ref, xo_ref, h_ref, meta_ref, *, n_lat_tiles):
    ya, yb = ya_ref[...], yb_ref[...]
    if yax_ref is not None:
        is_lat = pl.program_id(0) < n_lat_tiles
        ya = jnp.where(is_lat, ya, yax_ref[...])
        yb = jnp.where(is_lat, yb, ybx_ref[...])
    a = _dot(ya, woa_ref[...])
    b = _dot(yb, wob_ref[...])
    mix = (ga_ref[...].astype(F32) * a + gb_ref[...].astype(F32) * b).astype(BF16)
    x = x_ref[...] + mod_ref[0, 2:3, :] * _dot(mix, wout_ref[...])
    xo_ref[...] = x
    h = _rms(x, nw_ref[...]) * (1.0 + mod_ref[0, 4:5, :]) + mod_ref[0, 3:4, :]
    _store_token_rows(h_ref, h)

    h_hi = h.astype(BF16)
    h_lo = (h - h_hi.astype(F32)).astype(BF16)
    l_hi = _dot(h_hi, wr2_ref[...])
    logits = l_hi[:, :LANES] + l_hi[:, LANES:] + _dot(h_lo, wr2_ref[:, :LANES])
    sc = jax.nn.sigmoid(logits.T[:N_EXPERTS, :])
    sel = sc + rb_ref[...]
    sel_rows = [sel[e:e + 1, :] for e in range(N_EXPERTS)]
    sc_rows = [sc[e:e + 1, :] for e in range(N_EXPERTS)]
    best = jnp.zeros_like(sel_rows[0])
    best_v = None
    for g in range(N_GROUPS):
        _, _, m1, m2 = _top2(sel_rows[g * EPG:(g + 1) * EPG])
        gv = m1 + m2
        if best_v is None:
            best_v = gv
        else:
            u = gv > best_v
            best = jnp.where(u, float(g), best)
            best_v = jnp.where(u, gv, best_v)
    sel_g = [_pick(best, [sel_rows[g * EPG + i] for g in range(N_GROUPS)]) for i in range(EPG)]
    sc_g = [_pick(best, [sc_rows[g * EPG + i] for g in range(N_GROUPS)]) for i in range(EPG)]
    i1, i2, _, _ = _top2(sel_g)
    s1 = _pick(i1, sc_g)
    s2 = _pick(i2, sc_g)
    tot = s1 + s2
    first_low = i1 < i2
    e_lo = jnp.where(first_low, i1, i2)
    e_hi = jnp.where(first_low, i2, i1)
    w_lo = jnp.where(first_low, s1, s2) / tot
    w_hi = jnp.where(first_low, s2, s1) / tot
    pid = jnp.where(e_lo == 0.0, e_hi - 1.0, jnp.where(e_lo == 1.0, e_hi + 1.0, 5.0))
    bucket = best * float(len(PAIRS)) + pid
    t = sel.shape[1]
    meta_ref[0] = jnp.concatenate([w_lo, w_hi, bucket, jnp.zeros((SUB - 3, t), F32)], axis=0)


def _merge_call(x, ya, yb, y_ctx, ga, gb, mod, nw, w, wr2, rb, *, n_rows, n_lat, seq):
    d = x.shape[1]
    tm = TM_MERGE
    nt = n_rows // tm
    n_lat_t = n_lat // tm
    tps = seq // tm
    ctx_row = n_lat // seq

    def tok(i):
        return (i, 0)

    def full(a):
        return pl.BlockSpec(a.shape, lambda i: (0,) * a.ndim)

    y_specs = [pl.BlockSpec((tm, 512), lambda i: (jnp.minimum(i, n_lat_t - 1), 0))] * 2
    y_args = [ya, yb]
    if y_ctx is not None:
        y_specs += [pl.BlockSpec((tm, 512), lambda i: (jnp.maximum(i - n_lat_t, 0), 0))] * 2
        y_args += list(y_ctx)

    def body(x_ref, ya_ref, yb_ref, *refs):
        if y_ctx is None:
            refs = (None, None) + refs
        _merge_kernel(x_ref, ya_ref, yb_ref, *refs, n_lat_tiles=n_lat_t)

    return pl.pallas_call(
        body,
        grid=(nt,),
        in_specs=[pl.BlockSpec((tm, d), tok)] + y_specs + [
                  pl.BlockSpec((tm, d), tok), pl.BlockSpec((tm, d), tok),
                  pl.BlockSpec((1, 8, d), lambda i: (jnp.where(i < n_lat_t, i // tps, ctx_row), 0, 0)),
                  full(nw), full(w["woa"]), full(w["wob"]), full(w["wout"]), full(wr2), full(rb)],
        out_specs=[pl.BlockSpec((tm, d), tok), pl.BlockSpec((tm * SUB, LANES), tok),
                   pl.BlockSpec((1, SUB, tm), lambda i: (i, 0, 0))],
        out_shape=[jax.ShapeDtypeStruct((n_rows, d), F32), jax.ShapeDtypeStruct((n_rows * SUB, LANES), F32),
                   jax.ShapeDtypeStruct((nt, SUB, tm), F32)],
        compiler_params=_params(("arbitrary",)),
        name="merge",
    )(x, *y_args, ga, gb, mod, nw, w["woa"], w["wob"], w["wout"], wr2, rb)


def _moe_kernel(tok_ref, e1_ref, e2_ref, nv_ref, wts_ref, h_hbm, wg1_ref, wu1_ref, wd1_ref, wg2_ref, wu2_ref, wd2_ref,
                f_hbm, hbuf, obuf, wgu_ref, wdn_ref, gsem, ssem, *, n_tok, n_tiles):
    t = pl.program_id(0)
    slot = t % MOE_BUFS
    slot1 = (t + 1) % MOE_BUFS
    slot2 = (t + 2) % MOE_BUFS
    live = nv_ref[t] > 0
    prev_live = jnp.logical_and(t >= 1, nv_ref[jnp.maximum(t - 1, 0)] > 0)

    t_prev = jnp.maximum(t - 1, 0)
    new_pair = jnp.logical_or(t == 0, jnp.logical_or(e1_ref[t] != e1_ref[t_prev], e2_ref[t] != e2_ref[t_prev]))

    @pl.when(jnp.logical_and(live, new_pair))
    def _():
        for i, w_ref in enumerate((wg1_ref, wu1_ref, wg2_ref, wu2_ref)):
            wgu_ref[i] = w_ref[0, 0].astype(BF16)
        for i, w_ref in enumerate((wd1_ref, wd2_ref)):
            wdn_ref[i] = w_ref[0, 0].astype(BF16)

    def gather_row(tile, s, r):
        tk = jnp.minimum(tok_ref[tile * TE + r], n_tok - 1)
        return pltpu.make_async_copy(h_hbm.at[pl.ds(tk * SUB, SUB)], hbuf.at[s, pl.ds(r * SUB, SUB)], gsem.at[s])

    def scatter_row(tile, s, r):
        tk = tok_ref[tile * TE + r]
        return pltpu.make_async_copy(obuf.at[s, pl.ds(r * SUB, SUB)], f_hbm.at[pl.ds(tk * SUB, SUB)], ssem.at[s])

    def wait_gather(s):
        pltpu.make_async_copy(h_hbm.at[pl.ds(0, TE * SUB)], hbuf.at[s], gsem.at[s]).wait()

    def wait_scatter(s):
        pltpu.make_async_copy(obuf.at[s], f_hbm.at[pl.ds(0, TE * SUB)], ssem.at[s]).wait()

    def start_rows(make_row, tile, s):
        def body(i, carry):
            make_row(tile, s, 2 * i).start(priority=0)
            make_row(tile, s, 2 * i + 1).start(priority=1)
            return carry
        lax.fori_loop(0, TE // 2, body, 0, unroll=4)

    @pl.when(t == 0)
    def _():
        obuf[...] = jnp.zeros(obuf.shape, F32)
        for half in range(2):
            fill = pltpu.make_async_copy(obuf.at[0], f_hbm.at[pl.ds((n_tok + half * TE) * SUB, TE * SUB)], ssem.at[0])
            fill.start()
            fill.wait()
        start_rows(gather_row, 0, 0)
        start_rows(gather_row, 1, 1)

    @pl.when(live)
    def _():
        wait_gather(slot)

        @pl.when(t >= 2)
        def _():
            wait_scatter(slot)

    @pl.when(live)
    def _():
        g_tile = jnp.minimum(t + 2, n_tiles - 1)
        s_tile = jnp.where(t == 0, n_tiles - 1, t - 1)
        issue = ([functools.partial(gather_row, g_tile, slot2, r) for r in range(TE)]
                 + [functools.partial(scatter_row, s_tile, slot2, r) for r in range(TE)])
        n_stage = 6
        per_stage = -(-len(issue) // n_stage)

        def issue_stage(k):
            for i, make in enumerate(issue[k * per_stage:(k + 1) * per_stage]):
                make().start(priority=i % 2)

        h = _load_token_rows(hbuf.at[slot], TE).astype(BF16)
        w_lo = jnp.broadcast_to(wts_ref[0, 0:1, :], (LANES, TE)).T[:, :1]
        w_hi = jnp.broadcast_to(wts_ref[0, 1:2, :], (LANES, TE)).T[:, :1]
        outs = []
        stage = 0
        for which, wt in ((0, w_lo), (1, w_hi)):
            hg = _dot(h, wgu_ref[2 * which])
            issue_stage(stage)
            hu = _dot(h, wgu_ref[2 * which + 1])
            issue_stage(stage + 1)
            act = (hg * jax.nn.sigmoid(hg) * hu).astype(BF16)
            outs.append(wt * _dot(act, wdn_ref[which]))
            issue_stage(stage + 2)
            stage += 3
        _store_token_rows(obuf.at[slot], outs[0] + outs[1])

    @pl.when(jnp.logical_and(jnp.logical_not(live), prev_live))
    def _():
        wait_gather(slot)
        wait_gather(slot1)
        wait_scatter(slot1)

        @pl.when(t >= 2)
        def _():
            wait_scatter(slot)
        start_rows(scatter_row, t - 1, slot2)
        wait_scatter(slot2)


def _moe_call(tok, e1, e2, nv, wts, h3, wg, wu, wd, *, layer, n_tok):
    d = wg.shape[2]
    de = wg.shape[3]
    n_tiles = e1.shape[0]

    def wspec(shape, which):
        return pl.BlockSpec((1, 1) + shape, lambda t, tok, e1, e2, nv: (layer, (e1, e2)[which][t], 0, 0))

    grid_spec = pltpu.PrefetchScalarGridSpec(
        num_scalar_prefetch=4,
        grid=(n_tiles,),
        in_specs=[pl.BlockSpec((1, 2, TE), lambda t, tok, e1, e2, nv: (t, 0, 0)),
                  pl.BlockSpec(memory_space=pl.ANY),
                  wspec((d, de), 0), wspec((d, de), 0), wspec((de, d), 0),
                  wspec((d, de), 1), wspec((d, de), 1), wspec((de, d), 1)],
        out_specs=pl.BlockSpec(memory_space=pl.ANY),
        scratch_shapes=[pltpu.VMEM((MOE_BUFS, TE * SUB, LANES), F32), pltpu.VMEM((MOE_BUFS, TE * SUB, LANES), F32),
                        pltpu.VMEM((4, d, de), BF16), pltpu.VMEM((2, de, d), BF16),
                        pltpu.SemaphoreType.DMA((MOE_BUFS,)), pltpu.SemaphoreType.DMA((MOE_BUFS,))],
    )
    return pl.pallas_call(
        functools.partial(_moe_kernel, n_tok=n_tok, n_tiles=n_tiles),
        grid_spec=grid_spec,
        out_shape=jax.ShapeDtypeStruct(((n_tok + 2 * TE) * SUB, LANES), F32),
        compiler_params=_params(("arbitrary",)),
        name="moe",
    )(tok, e1, e2, nv, wts, h3, wg, wu, wd, wg, wu, wd)


def _route_tables(meta, n_tok):
    w_lo = meta[:, 0, :].reshape(-1)
    w_hi = meta[:, 1, :].reshape(-1)
    bucket = meta[:, 2, :].reshape(-1).astype(jnp.int32)
    n_slots = n_tok + N_BUCKETS * TE
    n_tiles = n_slots // TE
    buckets = jnp.arange(N_BUCKETS, dtype=jnp.int32)
    counts = jnp.sum(bucket[:, None] == buckets[None, :], axis=0, dtype=jnp.int32)
    padded = ((counts + TE - 1) // TE) * TE
    pad_end = jnp.cumsum(padded)
    pad_start = pad_end - padded
    fill_i = jnp.arange(TE, dtype=jnp.int32)[None, :]
    fill_key = jnp.where(fill_i < (padded - counts)[:, None], 2 * buckets[:, None] + 1, 2 * N_BUCKETS).reshape(-1)
    keys = jnp.concatenate([2 * bucket, fill_key])
    ids = jnp.concatenate([jnp.arange(n_tok, dtype=jnp.int32), jnp.full((N_BUCKETS * TE,), -1, jnp.int32)])
    zeros = jnp.zeros((N_BUCKETS * TE,), F32)
    _, ids, w_lo, w_hi = lax.sort((keys, ids, jnp.concatenate([w_lo, zeros]), jnp.concatenate([w_hi, zeros])),
                                  num_keys=1, is_stable=True)
    slot = jnp.arange(n_slots, dtype=jnp.int32)
    tok = jnp.where(ids < 0, n_tok + ((slot // TE) % 2) * TE + slot % TE, ids)
    wts = jnp.stack([w_lo.reshape(n_tiles, TE), w_hi.reshape(n_tiles, TE)], axis=1)

    tile_start = jnp.arange(n_tiles, dtype=jnp.int32) * TE
    tb = jnp.sum(tile_start[:, None] >= pad_end[None, :], axis=1, dtype=jnp.int32)
    used = tb < N_BUCKETS
    onehot = (tb[:, None] == buckets[None, :]).astype(jnp.int32)
    nv = jnp.clip(jnp.sum(onehot * (counts + pad_start)[None, :], axis=1) - tile_start, 0, TE) * used
    tbe = jnp.where(used, tb, jnp.max(jnp.where(used, tb, 0)))
    pid = tbe % len(PAIRS)
    pair_lo = (pid >= 3).astype(jnp.int32) + (pid >= 5).astype(jnp.int32)
    pair_hi = pid + 1 - 2 * (pid >= 3).astype(jnp.int32) - (pid >= 5).astype(jnp.int32)
    e1 = (tbe // len(PAIRS)) * EPG + pair_lo
    e2 = (tbe // len(PAIRS)) * EPG + pair_hi
    return tok.astype(jnp.int32), e1.astype(jnp.int32), e2.astype(jnp.int32), nv.astype(jnp.int32), wts


def _final_kernel(x_ref, f_ref, mod_ref, nw_ref, o_ref):
    x = x_ref[...] + mod_ref[0, 5:6, :] * _load_token_rows(f_ref, x_ref.shape[0])
    o_ref[...] = _rms(x, nw_ref[...])


def _final_call(x, f, mod, nw, *, n_lat, seq):
    d = x.shape[1]
    tm = TM_MERGE
    tps = seq // tm
    return pl.pallas_call(
        _final_kernel,
        grid=(n_lat // tm,),
        in_specs=[pl.BlockSpec((tm, d), lambda i: (i, 0)), pl.BlockSpec((tm * SUB, LANES), lambda i: (i, 0)),
                  pl.BlockSpec((1, 8, d), lambda i: (i // tps, 0, 0)), pl.BlockSpec((1, d), lambda i: (0, 0))],
        out_specs=pl.BlockSpec((tm, d), lambda i: (i, 0)),
        out_shape=jax.ShapeDtypeStruct((n_lat, d), F32),
        compiler_params=_params(("arbitrary",)),
        name="final",
    )(x, f, mod, nw)


def _rope_table(pos_r, pos_c, dim, lane_of):
    d = dim // 2
    half = d // 2
    lane = np.arange(LANES)
    rl = lane_of(lane)
    is_rope = rl >= 0
    rl = np.maximum(rl, 0)
    use_col = rl >= d
    j = rl % half
    first = (rl % d) < half
    inv = (np.float32(ROPE_BASE) ** (-(2.0 * j).astype(np.float32) / np.float32(d))).astype(np.float32)
    pos = np.where(use_col[None, :], pos_c[:, None], pos_r[:, None]).astype(np.float32)
    ang = pos * inv[None, :]
    cos = np.where(is_rope[None, :], np.cos(ang), 1.0)
    sin = np.where(is_rope[None, :], np.sin(ang), 0.0)
    return np.stack([cos, np.where(first[None, :], -sin, 0.0), np.where(first[None, :], 0.0, sin)]).astype(np.float32)


def _tables(seq):
    t = np.arange(seq)
    rows, cols = t // GRID_W, t % GRID_W
    tab_a = _rope_table(rows, cols, A_HEAD_DIM, lambda lane: lane % A_HEAD_DIM)
    tab_b = _rope_table(rows, cols, B_ROPE,
                        lambda lane: np.where((lane >= B_NOPE) & (lane < B_NOPE + B_ROPE), lane - B_NOPE, -1))
    ident = np.stack([np.ones((TM, LANES), np.float32), np.zeros((TM, LANES), np.float32),
                      np.zeros((TM, LANES), np.float32)])
    return jnp.asarray(np.concatenate([tab_a, ident], axis=1)), jnp.asarray(np.concatenate([tab_b, ident], axis=1))


def _window_band():
    r = np.arange(BLOCK)[:, None]
    c = np.arange(3 * BLOCK)[None, :]
    dist = c - r
    return jnp.asarray(np.where((dist >= BLOCK - WINDOW) & (dist <= BLOCK + WINDOW), 0.0, NEG).astype(np.float32))


def _layer_weights(w_in, w_uq, w_ukv, q_norm, kv_norm, w_o_a, w_o_b, w_out):
    n_mix = W1_KR + B_ROPE
    w1 = jnp.pad(w_in[:, :n_mix], ((0, 0), (0, W1_COLS - n_mix)))
    g_w = w_in[:, n_mix:]
    wuq = jnp.pad(w_uq.reshape(B_Q_RANK, B_HEADS, B_NOPE + B_ROPE),
                  ((0, 0), (0, 0), (0, LANES - B_NOPE - B_ROPE))).reshape(B_Q_RANK, B_HEADS * LANES)
    ukv = w_ukv.reshape(B_KV_RANK, B_HEADS, B_NOPE + B_V)
    wukn = jnp.pad(ukv[:, :, :B_NOPE], ((0, 0), (0, 0), (0, LANES - B_NOPE))).reshape(B_KV_RANK, B_HEADS * LANES)
    wuv = ukv[:, :, B_NOPE:].reshape(B_KV_RANK, B_HEADS * B_V)
    src = jnp.arange(LANES)[:, None]
    dst = jnp.arange(B_HEADS * LANES)[None, :]
    rp = ((src < B_ROPE) & (dst % LANES == src + B_NOPE)).astype(BF16)
    return dict(w1=w1.astype(BF16), wg=g_w.astype(BF16), qn=q_norm.reshape(1, -1), wuq=wuq.astype(BF16),
                kvn=kv_norm.reshape(1, -1), wukn=wukn.astype(BF16), wuv=wuv.astype(BF16), rp=rp,
                woa=w_o_a.astype(BF16), wob=w_o_b.astype(BF16), wout=w_out.astype(BF16))


def kernel(x, c, ctx, c_ctx, w_mod, b_mod, norm_mix, norm_ffn, w_in, attn_sink, mla_q_norm, w_uq, mla_kv_norm, w_ukv,
           w_o_a, w_o_b, w_out, w_router, router_bias, w_expert_gate, w_expert_up, w_expert_down, final_norm):
    batch, seq, d = x.shape
    ctx_len = ctx.shape[1]
    depth = w_mod.shape[0]
    n_lat = batch * seq
    n_all = n_lat + batch * ctx_len
    for tile in (TM, TM_MERGE):
        assert seq % tile == 0 and (batch * ctx_len) % tile == 0
    assert d == SUB * LANES and seq % TQ == 0 and seq % GRID_W == 0 and batch + 1 <= MOD_ROWS

    c_all = jnp.concatenate([c, c_ctx[None, :], jnp.zeros((MOD_ROWS - batch - 1, d), F32)], axis=0)
    mod = _mod_call(c_all, w_mod, b_mod)
    mod = jnp.pad(mod.transpose(0, 2, 1, 3), ((0, 0), (0, 0), (0, 2), (0, 0)))

    tab_a, tab_b = _tables(seq)
    wr = jnp.pad(w_router.astype(F32), ((0, 0), (0, LANES - N_EXPERTS)))
    wr_hi = wr.astype(BF16)
    wr2 = jnp.concatenate([wr_hi, (wr - wr_hi.astype(F32)).astype(BF16)], axis=1)
    rb = router_bias.astype(F32).reshape(N_EXPERTS, 1)

    xs = (x.reshape(n_lat, d), ctx.reshape(batch * ctx_len, d))
    f = None
    for l in range(depth):
        need_ctx = l < depth - 1
        w = _layer_weights(w_in[l], w_uq[l], w_ukv[l], mla_q_norm[l], mla_kv_norm[l], w_o_a[l], w_o_b[l], w_out[l])
        xs, qa, ka, va, qb, kb, vb, ga, gb = _proj_call(
            xs, f, mod[l - 1] if l else None, mod[l], norm_mix[l].reshape(1, d), tab_a, tab_b, w, n_lat=n_lat, seq=seq)
        dims = dict(batch=batch, seq=seq, ctx_len=ctx_len)
        sink = attn_sink[l].astype(F32)
        ya = _window_call(sink, qa, ka, va, latent=True, **dims)
        yb = _mla_call(qb, kb, vb, latent=True, **dims)
        y_ctx = None
        if need_ctx:
            y_ctx = (_window_call(sink, qa, ka, va, latent=False, **dims), _mla_call(qb, kb, vb, latent=False, **dims))
        n_rows = n_all if need_ctx else n_lat
        xs, h3, meta = _merge_call(xs, ya, yb, y_ctx, ga, gb, mod[l], norm_ffn[l].reshape(1, d), w, wr2, rb,
                                   n_rows=n_rows, n_lat=n_lat, seq=seq)
        tok, e1, e2, nv, wts = _route_tables(meta, n_rows)
        f = _moe_call(tok, e1, e2, nv, wts, h3, w_expert_gate, w_expert_up, w_expert_down, layer=l, n_tok=n_rows)
    out = _final_call(xs, f, mod[depth - 1], final_norm.reshape(1, d), n_lat=n_lat, seq=seq)
    return out.reshape(batch, seq, d)
```

```python
import functools

import numpy as np
import jax
import jax.numpy as jnp
from jax import lax
from jax.experimental import pallas as pl
from jax.experimental.pallas import tpu as pltpu

F32 = jnp.float32
BF16 = jnp.bfloat16

EPS = 1e-6
ROPE_BASE = 10000.0
GRID_W = 64
BLOCK = 128
A_HEADS, A_KV_HEADS, A_HEAD_DIM = 8, 2, 64
A_GROUP = A_HEADS // A_KV_HEADS
WINDOW = 128
B_HEADS, B_NOPE, B_ROPE, B_V = 8, 64, 32, 64
B_Q_RANK, B_KV_RANK = 256, 128
N_EXPERTS, N_GROUPS = 16, 4
EPG = N_EXPERTS // N_GROUPS
D_EXPERT = 512
PAIRS = ((0, 1), (0, 2), (0, 3), (1, 2), (1, 3), (2, 3))
N_BUCKETS = N_GROUPS * len(PAIRS)

LANES = 128
SUB = 8
TM = 512
TM_MERGE = 512
TQ = 512
MLA_PAIRS = 4
WIN_QB = 8
TE = 512
MOE_BUFS = 3
MOD_ROWS = 24
NEG = -1e30
LOG2E = 1.4426950408889634
VMEM_LIMIT = 56 * 1024 * 1024

W1_QA, W1_KA, W1_VA, W1_CQ, W1_CKV, W1_KR = 0, 512, 640, 768, 1024, 1152
W1_COLS = 1280


def _params(sem, vmem=VMEM_LIMIT):
    return pltpu.CompilerParams(dimension_semantics=sem, vmem_limit_bytes=vmem)


def _dot(a, b):
    return jnp.dot(a, b, preferred_element_type=F32)


def _dot_nt(a, b):
    return lax.dot_general(a, b, (((1,), (1,)), ((), ())), preferred_element_type=F32)


def _rms(x, g):
    return x * lax.rsqrt(jnp.mean(x * x, axis=-1, keepdims=True) + EPS) * g


def _lane_tile(t, width):
    return jnp.concatenate([t] * (width // LANES), axis=1)


def _rope(t, tab_ref, shift):
    w = t.shape[-1]
    up = pltpu.roll(t, w - shift, 1)
    dn = pltpu.roll(t, shift, 1)
    return (t * _lane_tile(tab_ref[0], w) + up * _lane_tile(tab_ref[1], w)
            + dn * _lane_tile(tab_ref[2], w))


def _dup_kv_heads(t):
    swapped = pltpu.roll(t, A_HEAD_DIM, 1)
    lo = lax.broadcasted_iota(jnp.int32, t.shape, 1) < A_HEAD_DIM
    return jnp.concatenate([jnp.where(lo, t, swapped), jnp.where(lo, swapped, t)], axis=1)


def _load_token_rows(ref, rows):
    return jnp.concatenate([ref[pl.ds(s, rows, stride=SUB), :] for s in range(SUB)], axis=1)


def _store_token_rows(ref, val):
    rows = val.shape[0]
    for s in range(SUB):
        ref[pl.ds(s, rows, stride=SUB), :] = val[:, s * LANES:(s + 1) * LANES]


def _mod_kernel(c_ref, w_ref, b_ref, o_ref):
    c = c_ref[...]
    a = (c * jax.nn.sigmoid(c)).astype(BF16)
    o_ref[0, 0] = _dot(a, w_ref[0].astype(BF16)) + b_ref[0, 0]


def _mod_call(c_all, w_mod, b_mod):
    depth, d, _ = w_mod.shape
    return pl.pallas_call(
        _mod_kernel,
        grid=(depth, 6),
        in_specs=[pl.BlockSpec((MOD_ROWS, d), lambda l, k: (0, 0)),
                  pl.BlockSpec((1, d, d), lambda l, k: (l, 0, k)),
                  pl.BlockSpec((1, 1, 1, d), lambda l, k: (l, k, 0, 0))],
        out_specs=pl.BlockSpec((1, 1, MOD_ROWS, d), lambda l, k: (l, k, 0, 0)),
        out_shape=jax.ShapeDtypeStruct((depth, 6, MOD_ROWS, d), F32),
        compiler_params=_params(("arbitrary", "arbitrary")),
        name="mod",
    )(c_all, w_mod, b_mod.reshape(depth, 6, 1, d))


def _proj_kernel(*refs, has_f, n_lat_tiles):
    if has_f:
        x_ref, f_ref, modp_ref, refs = refs[0], refs[1], refs[2], refs[3:]
    else:
        x_ref, xc_ref, refs = refs[0], refs[1], refs[2:]
    (mod_ref, nw_ref, ta_ref, tb_ref, w1_ref, wg_ref, qn_ref, wuq_ref, kvn_ref, wukn_ref, wuv_ref,
     rp_ref), refs = refs[:12], refs[12:]
    xo_ref, qa_ref, ka_ref, va_ref, qb_ref, kb_ref, vb_ref, ga_ref, gb_ref = refs

    if has_f:
        x = x_ref[...] + modp_ref[0, 5:6, :] * _load_token_rows(f_ref, x_ref.shape[0])
    else:
        x = jnp.where(pl.program_id(0) < n_lat_tiles, x_ref[...], xc_ref[...])
    xo_ref[...] = x
    h = (_rms(x, nw_ref[...]) * (1.0 + mod_ref[0, 1:2, :]) + mod_ref[0, 0:1, :]).astype(BF16)

    t = _dot(h, w1_ref[...])
    qa_ref[...] = (_rope(t[:, W1_QA:W1_KA], ta_ref, 16) * (A_HEAD_DIM ** -0.5 * LOG2E)).astype(BF16)
    ka_ref[...] = _dup_kv_heads(_rope(t[:, W1_KA:W1_VA], ta_ref, 16)).astype(BF16)
    va_ref[...] = _dup_kv_heads(t[:, W1_VA:W1_CQ]).astype(BF16)

    cq = _rms(t[:, W1_CQ:W1_CKV], qn_ref[...]).astype(BF16)
    qb = _rope(_dot(cq, wuq_ref[...]), tb_ref, 8)
    qb_ref[...] = (qb * ((B_NOPE + B_ROPE) ** -0.5 * LOG2E)).astype(BF16)

    ckv = _rms(t[:, W1_CKV:W1_KR], kvn_ref[...]).astype(BF16)
    vb_ref[...] = _dot(ckv, wuv_ref[...]).astype(BF16)
    kr = t[:, W1_KR:W1_COLS]
    kr_hi = kr.astype(BF16)
    kr_lo = (kr - kr_hi.astype(F32)).astype(BF16)
    kb = _dot(ckv, wukn_ref[...]) + _dot(kr_hi, rp_ref[...]) + _dot(kr_lo, rp_ref[...])
    kb_ref[...] = _rope(kb, tb_ref, 8).astype(BF16)

    d = ga_ref.shape[-1]
    ga_ref[...] = jax.nn.sigmoid(_dot(h, wg_ref[:, :d])).astype(BF16)
    gb_ref[...] = jax.nn.sigmoid(_dot(h, wg_ref[:, d:])).astype(BF16)


def _proj_call(x, f, modp, mod, nw, tab_a, tab_b, w, *, n_lat, seq):
    has_f = f is not None
    n = x.shape[0] if has_f else x[0].shape[0] + x[1].shape[0]
    d = mod.shape[-1]
    nt = n // TM
    n_lat_t = n_lat // TM
    tps = seq // TM
    ctx_row = n_lat // seq

    def tok(i):
        return (i, 0)

    def modi(i):
        return (jnp.where(i < n_lat_t, i // tps, ctx_row), 0, 0)

    def tabi(i):
        return (0, jnp.where(i < n_lat_t, i % tps, tps), 0)

    def full(a):
        return pl.BlockSpec(a.shape, lambda i: (0,) * a.ndim, pipeline_mode=pl.Buffered(1))

    mod_spec = pl.BlockSpec((1, 8, d), modi)
    if has_f:
        in_specs = [pl.BlockSpec((TM, d), tok), pl.BlockSpec((TM * SUB, LANES), tok), mod_spec]
        args = [x, f, modp]
    else:
        in_specs = [pl.BlockSpec((TM, d), lambda i: (jnp.minimum(i, n_lat_t - 1), 0)),
                    pl.BlockSpec((TM, d), lambda i: (jnp.maximum(i - n_lat_t, 0), 0))]
        args = list(x)
    in_specs += [mod_spec, full(nw), pl.BlockSpec((3, TM, LANES), tabi), pl.BlockSpec((3, TM, LANES), tabi)]
    args += [mod, nw, tab_a, tab_b]
    for k in ("w1", "wg", "qn", "wuq", "kvn", "wukn", "wuv", "rp"):
        in_specs.append(full(w[k]))
        args.append(w[k])

    widths = (512, 256, 256, 1024, 1024, 512, d, d)
    out_shape = [jax.ShapeDtypeStruct((n, d), F32)] + [jax.ShapeDtypeStruct((n, wd), BF16) for wd in widths]
    out_specs = [pl.BlockSpec((TM, d), tok)] + [pl.BlockSpec((TM, wd), tok) for wd in widths]
    return pl.pallas_call(
        functools.partial(_proj_kernel, has_f=has_f, n_lat_tiles=n_lat_t),
        grid=(nt,), in_specs=in_specs, out_specs=out_specs, out_shape=out_shape,
        compiler_params=_params(("arbitrary",)),
        name="proj",
    )(*args)


def _window_kernel(sink_ref, band_ref, q_ref, kp_ref, kc_ref, kn_ref, kx_ref, vp_ref, vc_ref, vn_ref, vx_ref, o_ref,
                   *, nb, n_off):
    rows = A_GROUP * BLOCK
    wq = q_ref.shape[0] // BLOCK
    lo = lax.broadcasted_iota(jnp.int32, (BLOCK, LANES), 1) < A_HEAD_DIM
    col_blk = lax.broadcasted_iota(jnp.int32, (1, 3 * BLOCK), 1) // BLOCK
    row_head = lax.broadcasted_iota(jnp.int32, (rows, 1), 0) // BLOCK
    zero = jnp.zeros((BLOCK, LANES), BF16)

    def key_block(sub, rel, lanes, p_ref, c_ref, n_ref):
        j = sub + rel
        if j < 0:
            return p_ref[:, lanes]
        if j >= wq:
            return n_ref[:, lanes]
        return c_ref[j * BLOCK:(j + 1) * BLOCK, lanes]

    for sub in range(wq):
        n = wq * pl.program_id(1) + n_off + sub
        qrows = slice(sub * BLOCK, (sub + 1) * BLOCK)
        off_prev = jnp.where(jnp.logical_or(n == 0, n >= nb), NEG, 0.0)
        off_cur = jnp.where(n >= nb, NEG, 0.0)
        off_next = jnp.where(n >= nb - 1, NEG, 0.0)
        bias = band_ref[...] + jnp.where(col_blk == 0, off_prev, jnp.where(col_blk == 1, off_cur, off_next))
        bias = jnp.concatenate([bias] * A_GROUP, axis=0)
        for k in range(A_KV_HEADS):
            ksl = slice(k * LANES, (k + 1) * LANES)
            parts = []
            for j in range(A_GROUP // 2):
                blk = q_ref[qrows, k * 2 * LANES + j * LANES:k * 2 * LANES + (j + 1) * LANES]
                parts += [jnp.where(lo, blk, zero), jnp.where(lo, zero, blk)]
            qs = jnp.concatenate(parts, axis=0)
            k_lat = jnp.concatenate([key_block(sub, rel, ksl, kp_ref, kc_ref, kn_ref) for rel in (-1, 0, 1)], axis=0)
            v_lat = jnp.concatenate([key_block(sub, rel, ksl, vp_ref, vc_ref, vn_ref) for rel in (-1, 0, 1)], axis=0)
            s_l = _dot_nt(qs, k_lat) + bias
            s_x = _dot_nt(qs, kx_ref[:, ksl])
            sink = jnp.zeros((rows, 1), F32)
            for g in range(A_GROUP):
                sink = jnp.where(row_head == g, sink_ref[k * A_GROUP + g] * LOG2E, sink)
            m = jnp.maximum(jnp.maximum(jnp.max(s_l, axis=-1, keepdims=True), jnp.max(s_x, axis=-1, keepdims=True)),
                            sink)
            e_l = jnp.exp2(s_l - m)
            e_x = jnp.exp2(s_x - m)
            den = jnp.sum(e_l, axis=-1, keepdims=True) + jnp.sum(e_x, axis=-1, keepdims=True) + jnp.exp2(sink - m)
            o = (_dot(e_l.astype(BF16), v_lat) + _dot(e_x.astype(BF16), vx_ref[:, ksl])) / den
            for j in range(A_GROUP // 2):
                even = o[(2 * j) * BLOCK:(2 * j + 1) * BLOCK]
                odd = o[(2 * j + 1) * BLOCK:(2 * j + 2) * BLOCK]
                o_ref[qrows, k * 2 * LANES + j * LANES:k * 2 * LANES + (j + 1) * LANES] = (
                    jnp.where(lo, even, odd).astype(BF16))


def _window_call(sink, qa, ka, va, *, batch, seq, ctx_len, latent):
    nb = seq // BLOCK
    wq = min(WIN_QB, nb) if latent else ctx_len // BLOCK
    steps = nb // wq if latent else 1
    assert nb % wq == 0
    ctx0 = batch * seq // ctx_len
    q0 = 0 if latent else batch * seq // (wq * BLOCK)

    def qi(b, i):
        return (q0 + b * steps + i, 0)

    def edge(off):
        return lambda b, i: (b * nb + (jnp.clip(wq * i + off, 0, nb - 1) if latent else 0), 0)

    def run(b, i):
        return (b * (nb // wq) + (i if latent else 0), 0)

    def xi(b, i):
        return (ctx0 + b, 0)

    kv_specs = [pl.BlockSpec((BLOCK, 2 * LANES), edge(-1)), pl.BlockSpec((wq * BLOCK, 2 * LANES), run),
                pl.BlockSpec((BLOCK, 2 * LANES), edge(wq)), pl.BlockSpec((ctx_len, 2 * LANES), xi)]
    return pl.pallas_call(
        functools.partial(_window_kernel, nb=nb, n_off=0 if latent else nb),
        grid=(batch, steps),
        in_specs=[pl.BlockSpec(memory_space=pltpu.SMEM), pl.BlockSpec((BLOCK, 3 * BLOCK), lambda b, i: (0, 0)),
                  pl.BlockSpec((wq * BLOCK, 4 * LANES), qi)] + kv_specs + kv_specs,
        out_specs=pl.BlockSpec((wq * BLOCK, 4 * LANES), lambda b, i: (b * steps + i, 0)),
        out_shape=jax.ShapeDtypeStruct((batch * steps * wq * BLOCK, 4 * LANES), BF16),
        compiler_params=_params(("arbitrary", "arbitrary")),
        name="window" if latent else "window_ctx",
    )(sink, _window_band(), qa, ka, ka, ka, ka, va, va, va, va)


def _mla_kernel(q_ref, *refs, with_lat):
    if with_lat:
        kl_ref, kx_ref, vl_ref, vx_ref, o_ref, vaug_ref = refs
    else:
        kx_ref, vx_ref, o_ref, vaug_ref = refs
    n_ctx = vx_ref.shape[0]
    n_pairs = o_ref.shape[1] // LANES
    lo = lax.broadcasted_iota(jnp.int32, (o_ref.shape[0], LANES), 1) < B_V

    @pl.when(pl.program_id(2) == 0)
    def _():
        one = jnp.ones((1, LANES), BF16)
        for v_ref, start in ((vx_ref, 0),) + (((vl_ref, n_ctx),) if with_lat else ()):
            n = v_ref.shape[0]
            keep = lax.broadcasted_iota(jnp.int32, (n, LANES), 1) < B_V
            for pp in range(n_pairs):
                v = v_ref[:, pp * LANES:(pp + 1) * LANES]
                vaug_ref[2 * pp, start:start + n, :] = jnp.where(keep, v, one)
                vaug_ref[2 * pp + 1, start:start + n, :] = jnp.where(keep, one, v)

    def head(h):
        sl = slice(h * LANES, (h + 1) * LANES)
        q = q_ref[:, sl]
        s_x = _dot_nt(q, kx_ref[:, sl])
        m = jnp.max(s_x, axis=-1, keepdims=True)
        if with_lat:
            s_l = _dot_nt(q, kl_ref[:, sl])
            m = jnp.maximum(m, jnp.max(s_l, axis=-1, keepdims=True))
        o = _dot(jnp.exp2(s_x - m).astype(BF16), vaug_ref[h, :n_ctx, :])
        if with_lat:
            o = o + _dot(jnp.exp2(s_l - m).astype(BF16), vaug_ref[h, n_ctx:, :])
        return o / pltpu.roll(o, B_V, 1)

    for pp in range(n_pairs):
        o_ref[:, pp * LANES:(pp + 1) * LANES] = jnp.where(lo, head(2 * pp), head(2 * pp + 1)).astype(BF16)


def _mla_call(qb, kb, vb, *, batch, seq, ctx_len, latent):
    ctx0 = batch * seq // ctx_len
    groups = B_HEADS // (2 * MLA_PAIRS)
    qk_w = MLA_PAIRS * 2 * LANES
    v_w = MLA_PAIRS * LANES
    tq = TQ if latent else ctx_len
    nq = seq // TQ if latent else 1
    kx_spec = pl.BlockSpec((ctx_len, qk_w), lambda b, p, j: (ctx0 + b, p))
    vx_spec = pl.BlockSpec((ctx_len, v_w), lambda b, p, j: (ctx0 + b, p))
    if latent:
        in_specs = [pl.BlockSpec((tq, qk_w), lambda b, p, j: (b * nq + j, p)),
                    pl.BlockSpec((seq, qk_w), lambda b, p, j: (b, p)), kx_spec,
                    pl.BlockSpec((seq, v_w), lambda b, p, j: (b, p)), vx_spec]
        args = (qb, kb, kb, vb, vb)
    else:
        in_specs = [pl.BlockSpec((tq, qk_w), lambda b, p, j: (ctx0 + b, p)), kx_spec, vx_spec]
        args = (qb, kb, vb)
    return pl.pallas_call(
        functools.partial(_mla_kernel, with_lat=latent),
        grid=(batch, groups, nq),
        in_specs=in_specs,
        out_specs=pl.BlockSpec((tq, v_w), lambda b, p, j: (b * nq + j, p)),
        out_shape=jax.ShapeDtypeStruct((batch * nq * tq, groups * v_w), BF16),
        scratch_shapes=[pltpu.VMEM((2 * MLA_PAIRS, ctx_len + (seq if latent else 0), LANES), BF16)],
        compiler_params=_params(("arbitrary", "arbitrary", "arbitrary")),
        name="mla" if latent else "mla_ctx",
    )(*args)


def _top2(v):
    i1 = jnp.zeros_like(v[0])
    m1 = v[0]
    for i in range(1, EPG):
        u = v[i] > m1
        i1 = jnp.where(u, float(i), i1)
        m1 = jnp.where(u, v[i], m1)
    i2 = jnp.zeros_like(v[0])
    m2 = jnp.full_like(v[0], -jnp.inf)
    for i in range(EPG):
        cand = jnp.where(i1 == float(i), -jnp.inf, v[i])
        u = cand > m2
        i2 = jnp.where(u, float(i), i2)
        m2 = jnp.where(u, cand, m2)
    return i1, i2, m1, m2


def _pick(idx, vals):
    out = vals[0]
    for i in range(1, len(vals)):
        out = jnp.where(idx == float(i), vals[i], out)
    return out


def _merge_kernel(x_ref, ya_ref, yb_ref, yax_ref, ybx_ref, ga_ref, gb_ref, mod_ref, nw_ref, woa_ref, wob_ref, wout_ref,
                  wr2_ref, rb_ref, xo_ref, h_ref, meta_ref, *, n_lat_tiles):
    ya, yb = ya_ref[...], yb_ref[...]
    if yax_ref is not None:
        is_lat = pl.program_id(0) < n_lat_tiles
        ya = jnp.where(is_lat, ya, yax_ref[...])
        yb = jnp.where(is_lat, yb, ybx_ref[...])
    a = _dot(ya, woa_ref[...])
    b = _dot(yb, wob_ref[...])
    mix = (ga_ref[...].astype(F32) * a + gb_ref[...].astype(F32) * b).astype(BF16)
    x = x_ref[...] + mod_ref[0, 2:3, :] * _dot(mix, wout_ref[...])
    xo_ref[...] = x
    h = _rms(x, nw_ref[...]) * (1.0 + mod_ref[0, 4:5, :]) + mod_ref[0, 3:4, :]
    _store_token_rows(h_ref, h)

    h_hi = h.astype(BF16)
    h_lo = (h - h_hi.astype(F32)).astype(BF16)
    l_hi = _dot(h_hi, wr2_ref[...])
    logits = l_hi[:, :LANES] + l_hi[:, LANES:] + _dot(h_lo, wr2_ref[:, :LANES])
    sc = jax.nn.sigmoid(logits.T[:N_EXPERTS, :])
    sel = sc + rb_ref[...]
    sel_rows = [sel[e:e + 1, :] for e in range(N_EXPERTS)]
    sc_rows = [sc[e:e + 1, :] for e in range(N_EXPERTS)]
    best = jnp.zeros_like(sel_rows[0])
    best_v = None
    for g in range(N_GROUPS):
        _, _, m1, m2 = _top2(sel_rows[g * EPG:(g + 1) * EPG])
        gv = m1 + m2
        if best_v is None:
            best_v = gv
        else:
            u = gv > best_v
            best = jnp.where(u, float(g), best)
            best_v = jnp.where(u, gv, best_v)
    sel_g = [_pick(best, [sel_rows[g * EPG + i] for g in range(N_GROUPS)]) for i in range(EPG)]
    sc_g = [_pick(best, [sc_rows[g * EPG + i] for g in range(N_GROUPS)]) for i in range(EPG)]
    i1, i2, _, _ = _top2(sel_g)
    s1 = _pick(i1, sc_g)
    s2 = _pick(i2, sc_g)
    tot = s1 + s2
    first_low = i1 < i2
    e_lo = jnp.where(first_low, i1, i2)
    e_hi = jnp.where(first_low, i2, i1)
    w_lo = jnp.where(first_low, s1, s2) / tot
    w_hi = jnp.where(first_low, s2, s1) / tot
    pid = jnp.where(e_lo == 0.0, e_hi - 1.0, jnp.where(e_lo == 1.0, e_hi + 1.0, 5.0))
    bucket = best * float(len(PAIRS)) + pid
    t = sel.shape[1]
    meta_ref[0] = jnp.concatenate([w_lo, w_hi, bucket, jnp.zeros((SUB - 3, t), F32)], axis=0)


def _merge_call(x, ya, yb, y_ctx, ga, gb, mod, nw, w, wr2, rb, *, n_rows, n_lat, seq):
    d = x.shape[1]
    tm = TM_MERGE
    nt = n_rows // tm
    n_lat_t = n_lat // tm
    tps = seq // tm
    ctx_row = n_lat // seq

    def tok(i):
        return (i, 0)

    def full(a):
        return pl.BlockSpec(a.shape, lambda i: (0,) * a.ndim)

    y_specs = [pl.BlockSpec((tm, 512), lambda i: (jnp.minimum(i, n_lat_t - 1), 0))] * 2
    y_args = [ya, yb]
    if y_ctx is not None:
        y_specs += [pl.BlockSpec((tm, 512), lambda i: (jnp.maximum(i - n_lat_t, 0), 0))] * 2
        y_args += list(y_ctx)

    def body(x_ref, ya_ref, yb_ref, *refs):
        if y_ctx is None:
            refs = (None, None) + refs
        _merge_kernel(x_ref, ya_ref, yb_ref, *refs, n_lat_tiles=n_lat_t)

    return pl.pallas_call(
        body,
        grid=(nt,),
        in_specs=[pl.BlockSpec((tm, d), tok)] + y_specs + [
                  pl.BlockSpec((tm, d), tok), pl.BlockSpec((tm, d), tok),
                  pl.BlockSpec((1, 8, d), lambda i: (jnp.where(i < n_lat_t, i // tps, ctx_row), 0, 0)),
                  full(nw), full(w["woa"]), full(w["wob"]), full(w["wout"]), full(wr2), full(rb)],
        out_specs=[pl.BlockSpec((tm, d), tok), pl.BlockSpec((tm * SUB, LANES), tok),
                   pl.BlockSpec((1, SUB, tm), lambda i: (i, 0, 0))],
        out_shape=[jax.ShapeDtypeStruct((n_rows, d), F32), jax.ShapeDtypeStruct((n_rows * SUB, LANES), F32),
                   jax.ShapeDtypeStruct((nt, SUB, tm), F32)],
        compiler_params=_params(("arbitrary",)),
        name="merge",
    )(x, *y_args, ga, gb, mod, nw, w["woa"], w["wob"], w["wout"], wr2, rb)


def _moe_kernel(tok_ref, e1_ref, e2_ref, nv_ref, wts_ref, h_hbm, wg1_ref, wu1_ref, wd1_ref, wg2_ref, wu2_ref, wd2_ref,
                f_hbm, hbuf, obuf, wgu_ref, wdn_ref, gsem, ssem, *, n_tok, n_tiles):
    t = pl.program_id(0)
    slot = t % MOE_BUFS
    slot1 = (t + 1) % MOE_BUFS
    slot2 = (t + 2) % MOE_BUFS
    live = nv_ref[t] > 0
    prev_live = jnp.logical_and(t >= 1, nv_ref[jnp.maximum(t - 1, 0)] > 0)

    t_prev = jnp.maximum(t - 1, 0)
    new_pair = jnp.logical_or(t == 0, jnp.logical_or(e1_ref[t] != e1_ref[t_prev], e2_ref[t] != e2_ref[t_prev]))

    @pl.when(jnp.logical_and(live, new_pair))
    def _():
        for i, w_ref in enumerate((wg1_ref, wu1_ref, wg2_ref, wu2_ref)):
            wgu_ref[i] = w_ref[0, 0].astype(BF16)
        for i, w_ref in enumerate((wd1_ref, wd2_ref)):
            wdn_ref[i] = w_ref[0, 0].astype(BF16)

    def gather_row(tile, s, r):
        tk = jnp.minimum(tok_ref[tile * TE + r], n_tok - 1)
        return pltpu.make_async_copy(h_hbm.at[pl.ds(tk * SUB, SUB)], hbuf.at[s, pl.ds(r * SUB, SUB)], gsem.at[s])

    def scatter_row(tile, s, r):
        tk = tok_ref[tile * TE + r]
        return pltpu.make_async_copy(obuf.at[s, pl.ds(r * SUB, SUB)], f_hbm.at[pl.ds(tk * SUB, SUB)], ssem.at[s])

    def wait_gather(s):
        pltpu.make_async_copy(h_hbm.at[pl.ds(0, TE * SUB)], hbuf.at[s], gsem.at[s]).wait()

    def wait_scatter(s):
        pltpu.make_async_copy(obuf.at[s], f_hbm.at[pl.ds(0, TE * SUB)], ssem.at[s]).wait()

    def start_rows(make_row, tile, s):
        def body(i, carry):
            make_row(tile, s, 2 * i).start(priority=0)
            make_row(tile, s, 2 * i + 1).start(priority=1)
            return carry
        lax.fori_loop(0, TE // 2, body, 0, unroll=4)

    @pl.when(t == 0)
    def _():
        obuf[...] = jnp.zeros(obuf.shape, F32)
        for half in range(2):
            fill = pltpu.make_async_copy(obuf.at[0], f_hbm.at[pl.ds((n_tok + half * TE) * SUB, TE * SUB)], ssem.at[0])
            fill.start()
            fill.wait()
        start_rows(gather_row, 0, 0)
        start_rows(gather_row, 1, 1)

    @pl.when(live)
    def _():
        wait_gather(slot)

        @pl.when(t >= 2)
        def _():
            wait_scatter(slot)

    @pl.when(live)
    def _():
        g_tile = jnp.minimum(t + 2, n_tiles - 1)
        s_tile = jnp.where(t == 0, n_tiles - 1, t - 1)
        issue = ([functools.partial(gather_row, g_tile, slot2, r) for r in range(TE)]
                 + [functools.partial(scatter_row, s_tile, slot2, r) for r in range(TE)])
        n_stage = 6
        per_stage = -(-len(issue) // n_stage)

        def issue_stage(k):
            for i, make in enumerate(issue[k * per_stage:(k + 1) * per_stage]):
                make().start(priority=i % 2)

        h = _load_token_rows(hbuf.at[slot], TE).astype(BF16)
        w_lo = jnp.broadcast_to(wts_ref[0, 0:1, :], (LANES, TE)).T[:, :1]
        w_hi = jnp.broadcast_to(wts_ref[0, 1:2, :], (LANES, TE)).T[:, :1]
        outs = []
        stage = 0
        for which, wt in ((0, w_lo), (1, w_hi)):
            hg = _dot(h, wgu_ref[2 * which])
            issue_stage(stage)
            hu = _dot(h, wgu_ref[2 * which + 1])
            issue_stage(stage + 1)
            act = (hg * jax.nn.sigmoid(hg) * hu).astype(BF16)
            outs.append(wt * _dot(act, wdn_ref[which]))
            issue_stage(stage + 2)
            stage += 3
        _store_token_rows(obuf.at[slot], outs[0] + outs[1])

    @pl.when(jnp.logical_and(jnp.logical_not(live), prev_live))
    def _():
        wait_gather(slot)
        wait_gather(slot1)
        wait_scatter(slot1)

        @pl.when(t >= 2)
        def _():
            wait_scatter(slot)
        start_rows(scatter_row, t - 1, slot2)
        wait_scatter(slot2)


def _moe_call(tok, e1, e2, nv, wts, h3, wg, wu, wd, *, layer, n_tok):
    d = wg.shape[2]
    de = wg.shape[3]
    n_tiles = e1.shape[0]

    def wspec(shape, which):
        return pl.BlockSpec((1, 1) + shape, lambda t, tok, e1, e2, nv: (layer, (e1, e2)[which][t], 0, 0))

    grid_spec = pltpu.PrefetchScalarGridSpec(
        num_scalar_prefetch=4,
        grid=(n_tiles,),
        in_specs=[pl.BlockSpec((1, 2, TE), lambda t, tok, e1, e2, nv: (t, 0, 0)),
                  pl.BlockSpec(memory_space=pl.ANY),
                  wspec((d, de), 0), wspec((d, de), 0), wspec((de, d), 0),
                  wspec((d, de), 1), wspec((d, de), 1), wspec((de, d), 1)],
        out_specs=pl.BlockSpec(memory_space=pl.ANY),
        scratch_shapes=[pltpu.VMEM((MOE_BUFS, TE * SUB, LANES), F32), pltpu.VMEM((MOE_BUFS, TE * SUB, LANES), F32),
                        pltpu.VMEM((4, d, de), BF16), pltpu.VMEM((2, de, d), BF16),
                        pltpu.SemaphoreType.DMA((MOE_BUFS,)), pltpu.SemaphoreType.DMA((MOE_BUFS,))],
    )
    return pl.pallas_call(
        functools.partial(_moe_kernel, n_tok=n_tok, n_tiles=n_tiles),
        grid_spec=grid_spec,
        out_shape=jax.ShapeDtypeStruct(((n_tok + 2 * TE) * SUB, LANES), F32),
        compiler_params=_params(("arbitrary",)),
        name="moe",
    )(tok, e1, e2, nv, wts, h3, wg, wu, wd, wg, wu, wd)


def _route_tables(meta, n_tok):
    w_lo = meta[:, 0, :].reshape(-1)
    w_hi = meta[:, 1, :].reshape(-1)
    bucket = meta[:, 2, :].reshape(-1).astype(jnp.int32)
    n_slots = n_tok + N_BUCKETS * TE
    n_tiles = n_slots // TE
    buckets = jnp.arange(N_BUCKETS, dtype=jnp.int32)
    counts = jnp.sum(bucket[:, None] == buckets[None, :], axis=0, dtype=jnp.int32)
    padded = ((counts + TE - 1) // TE) * TE
    pad_end = jnp.cumsum(padded)
    pad_start = pad_end - padded
    fill_i = jnp.arange(TE, dtype=jnp.int32)[None, :]
    fill_key = jnp.where(fill_i < (padded - counts)[:, None], 2 * buckets[:, None] + 1, 2 * N_BUCKETS).reshape(-1)
    keys = jnp.concatenate([2 * bucket, fill_key])
    ids = jnp.concatenate([jnp.arange(n_tok, dtype=jnp.int32), jnp.full((N_BUCKETS * TE,), -1, jnp.int32)])
    zeros = jnp.zeros((N_BUCKETS * TE,), F32)
    _, ids, w_lo, w_hi = lax.sort((keys, ids, jnp.concatenate([w_lo, zeros]), jnp.concatenate([w_hi, zeros])),
                                  num_keys=1, is_stable=True)
    slot = jnp.arange(n_slots, dtype=jnp.int32)
    tok = jnp.where(ids < 0, n_tok + ((slot // TE) % 2) * TE + slot % TE, ids)
    wts = jnp.stack([w_lo.reshape(n_tiles, TE), w_hi.reshape(n_tiles, TE)], axis=1)

    tile_start = jnp.arange(n_tiles, dtype=jnp.int32) * TE
    tb = jnp.sum(tile_start[:, None] >= pad_end[None, :], axis=1, dtype=jnp.int32)
    used = tb < N_BUCKETS
    onehot = (tb[:, None] == buckets[None, :]).astype(jnp.int32)
    nv = jnp.clip(jnp.sum(onehot * (counts + pad_start)[None, :], axis=1) - tile_start, 0, TE) * used
    tbe = jnp.where(used, tb, jnp.max(jnp.where(used, tb, 0)))
    pid = tbe % len(PAIRS)
    pair_lo = (pid >= 3).astype(jnp.int32) + (pid >= 5).astype(jnp.int32)
    pair_hi = pid + 1 - 2 * (pid >= 3).astype(jnp.int32) - (pid >= 5).astype(jnp.int32)
    e1 = (tbe // len(PAIRS)) * EPG + pair_lo
    e2 = (tbe // len(PAIRS)) * EPG + pair_hi
    return tok.astype(jnp.int32), e1.astype(jnp.int32), e2.astype(jnp.int32), nv.astype(jnp.int32), wts


def _final_kernel(x_ref, f_ref, mod_ref, nw_ref, o_ref):
    x = x_ref[...] + mod_ref[0, 5:6, :] * _load_token_rows(f_ref, x_ref.shape[0])
    o_ref[...] = _rms(x, nw_ref[...])


def _final_call(x, f, mod, nw, *, n_lat, seq):
    d = x.shape[1]
    tm = TM_MERGE
    tps = seq // tm
    return pl.pallas_call(
        _final_kernel,
        grid=(n_lat // tm,),
        in_specs=[pl.BlockSpec((tm, d), lambda i: (i, 0)), pl.BlockSpec((tm * SUB, LANES), lambda i: (i, 0)),
                  pl.BlockSpec((1, 8, d), lambda i: (i // tps, 0, 0)), pl.BlockSpec((1, d), lambda i: (0, 0))],
        out_specs=pl.BlockSpec((tm, d), lambda i: (i, 0)),
        out_shape=jax.ShapeDtypeStruct((n_lat, d), F32),
        compiler_params=_params(("arbitrary",)),
        name="final",
    )(x, f, mod, nw)


def _rope_table(pos_r, pos_c, dim, lane_of):
    d = dim // 2
    half = d // 2
    lane = np.arange(LANES)
    rl = lane_of(lane)
    is_rope = rl >= 0
    rl = np.maximum(rl, 0)
    use_col = rl >= d
    j = rl % half
    first = (rl % d) < half
    inv = (np.float32(ROPE_BASE) ** (-(2.0 * j).astype(np.float32) / np.float32(d))).astype(np.float32)
    pos = np.where(use_col[None, :], pos_c[:, None], pos_r[:, None]).astype(np.float32)
    ang = pos * inv[None, :]
    cos = np.where(is_rope[None, :], np.cos(ang), 1.0)
    sin = np.where(is_rope[None, :], np.sin(ang), 0.0)
    return np.stack([cos, np.where(first[None, :], -sin, 0.0), np.where(first[None, :], 0.0, sin)]).astype(np.float32)


def _tables(seq):
    t = np.arange(seq)
    rows, cols = t // GRID_W, t % GRID_W
    tab_a = _rope_table(rows, cols, A_HEAD_DIM, lambda lane: lane % A_HEAD_DIM)
    tab_b = _rope_table(rows, cols, B_ROPE,
                        lambda lane: np.where((lane >= B_NOPE) & (lane < B_NOPE + B_ROPE), lane - B_NOPE, -1))
    ident = np.stack([np.ones((TM, LANES), np.float32), np.zeros((TM, LANES), np.float32),
                      np.zeros((TM, LANES), np.float32)])
    return jnp.asarray(np.concatenate([tab_a, ident], axis=1)), jnp.asarray(np.concatenate([tab_b, ident], axis=1))


def _window_band():
    r = np.arange(BLOCK)[:, None]
    c = np.arange(3 * BLOCK)[None, :]
    dist = c - r
    return jnp.asarray(np.where((dist >= BLOCK - WINDOW) & (dist <= BLOCK + WINDOW), 0.0, NEG).astype(np.float32))


def _layer_weights(w_in, w_uq, w_ukv, q_norm, kv_norm, w_o_a, w_o_b, w_out):
    n_mix = W1_KR + B_ROPE
    w1 = jnp.pad(w_in[:, :n_mix], ((0, 0), (0, W1_COLS - n_mix)))
    g_w = w_in[:, n_mix:]
    wuq = jnp.pad(w_uq.reshape(B_Q_RANK, B_HEADS, B_NOPE + B_ROPE),
                  ((0, 0), (0, 0), (0, LANES - B_NOPE - B_ROPE))).reshape(B_Q_RANK, B_HEADS * LANES)
    ukv = w_ukv.reshape(B_KV_RANK, B_HEADS, B_NOPE + B_V)
    wukn = jnp.pad(ukv[:, :, :B_NOPE], ((0, 0), (0, 0), (0, LANES - B_NOPE))).reshape(B_KV_RANK, B_HEADS * LANES)
    wuv = ukv[:, :, B_NOPE:].reshape(B_KV_RANK, B_HEADS * B_V)
    src = jnp.arange(LANES)[:, None]
    dst = jnp.arange(B_HEADS * LANES)[None, :]
    rp = ((src < B_ROPE) & (dst % LANES == src + B_NOPE)).astype(BF16)
    return dict(w1=w1.astype(BF16), wg=g_w.astype(BF16), qn=q_norm.reshape(1, -1), wuq=wuq.astype(BF16),
                kvn=kv_norm.reshape(1, -1), wukn=wukn.astype(BF16), wuv=wuv.astype(BF16), rp=rp,
                woa=w_o_a.astype(BF16), wob=w_o_b.astype(BF16), wout=w_out.astype(BF16))


def kernel(x, c, ctx, c_ctx, w_mod, b_mod, norm_mix, norm_ffn, w_in, attn_sink, mla_q_norm, w_uq, mla_kv_norm, w_ukv,
           w_o_a, w_o_b, w_out, w_router, router_bias, w_expert_gate, w_expert_up, w_expert_down, final_norm):
    batch, seq, d = x.shape
    ctx_len = ctx.shape[1]
    depth = w_mod.shape[0]
    n_lat = batch * seq
    n_all = n_lat + batch * ctx_len
    for tile in (TM, TM_MERGE):
        assert seq % tile == 0 and (batch * ctx_len) % tile == 0
    assert d == SUB * LANES and seq % TQ == 0 and seq % GRID_W == 0 and batch + 1 <= MOD_ROWS

    c_all = jnp.concatenate([c, c_ctx[None, :], jnp.zeros((MOD_ROWS - batch - 1, d), F32)], axis=0)
    mod = _mod_call(c_all, w_mod, b_mod)
    mod = jnp.pad(mod.transpose(0, 2, 1, 3), ((0, 0), (0, 0), (0, 2), (0, 0)))

    tab_a, tab_b = _tables(seq)
    wr = jnp.pad(w_router.astype(F32), ((0, 0), (0, LANES - N_EXPERTS)))
    wr_hi = wr.astype(BF16)
    wr2 = jnp.concatenate([wr_hi, (wr - wr_hi.astype(F32)).astype(BF16)], axis=1)
    rb = router_bias.astype(F32).reshape(N_EXPERTS, 1)

    xs = (x.reshape(n_lat, d), ctx.reshape(batch * ctx_len, d))
    f = None
    for l in range(depth):
        need_ctx = l < depth - 1
        w = _layer_weights(w_in[l], w_uq[l], w_ukv[l], mla_q_norm[l], mla_kv_norm[l], w_o_a[l], w_o_b[l], w_out[l])
        xs, qa, ka, va, qb, kb, vb, ga, gb = _proj_call(
            xs, f, mod[l - 1] if l else None, mod[l], norm_mix[l].reshape(1, d), tab_a, tab_b, w, n_lat=n_lat, seq=seq)
        dims = dict(batch=batch, seq=seq, ctx_len=ctx_len)
        sink = attn_sink[l].astype(F32)
        ya = _window_call(sink, qa, ka, va, latent=True, **dims)
        yb = _mla_call(qb, kb, vb, latent=True, **dims)
        y_ctx = None
        if need_ctx:
            y_ctx = (_window_call(sink, qa, ka, va, latent=False, **dims), _mla_call(qb, kb, vb, latent=False, **dims))
        n_rows = n_all if need_ctx else n_lat
        xs, h3, meta = _merge_call(xs, ya, yb, y_ctx, ga, gb, mod[l], norm_ffn[l].reshape(1, d), w, wr2, rb,
                                   n_rows=n_rows, n_lat=n_lat, seq=seq)
        tok, e1, e2, nv, wts = _route_tables(meta, n_rows)
        f = _moe_call(tok, e1, e2, nv, wts, h3, w_expert_gate, w_expert_up, w_expert_down, layer=l, n_tok=n_rows)
    out = _final_call(xs, f, mod[depth - 1], final_norm.reshape(1, d), n_lat=n_lat, seq=seq)
    return out.reshape(batch, seq, d)
```

```python
import functools

import numpy as np
import jax
import jax.numpy as jnp
from jax import lax
from jax.experimental import pallas as pl
from jax.experimental.pallas import tpu as pltpu

F32 = jnp.float32
BF16 = jnp.bfloat16

EPS = 1e-6
ROPE_BASE = 10000.0
GRID_W = 64
BLOCK = 128
A_HEADS, A_KV_HEADS, A_HEAD_DIM = 8, 2, 64
A_GROUP = A_HEADS // A_KV_HEADS
WINDOW = 128
B_HEADS, B_NOPE, B_ROPE, B_V = 8, 64, 32, 64
B_Q_RANK, B_KV_RANK = 256, 128
N_EXPERTS, N_GROUPS = 16, 4
EPG = N_EXPERTS // N_GROUPS
D_EXPERT = 512
PAIRS = ((0, 1), (0, 2), (0, 3), (1, 2), (1, 3), (2, 3))
N_BUCKETS = N_GROUPS * len(PAIRS)

LANES = 128
SUB = 8
TM = 512
TM_MERGE = 512
TQ = 512
MLA_PAIRS = 4
WIN_QB = 8
TE = 256
MOE_BUFS = 3
MOD_ROWS = 24
NEG = -1e30
LOG2E = 1.4426950408889634
VMEM_LIMIT = 56 * 1024 * 1024

W1_QA, W1_KA, W1_VA, W1_CQ, W1_CKV, W1_KR = 0, 512, 640, 768, 1024, 1152
W1_COLS = 1280


def _params(sem, vmem=VMEM_LIMIT):
    return pltpu.CompilerParams(dimension_semantics=sem, vmem_limit_bytes=vmem)


def _dot(a, b):
    return jnp.dot(a, b, preferred_element_type=F32)


def _dot_nt(a, b):
    return lax.dot_general(a, b, (((1,), (1,)), ((), ())), preferred_element_type=F32)


def _rms(x, g):
    return x * lax.rsqrt(jnp.mean(x * x, axis=-1, keepdims=True) + EPS) * g


def _lane_tile(t, width):
    return jnp.concatenate([t] * (width // LANES), axis=1)


def _rope(t, tab_ref, shift):
    w = t.shape[-1]
    up = pltpu.roll(t, w - shift, 1)
    dn = pltpu.roll(t, shift, 1)
    return (t * _lane_tile(tab_ref[0], w) + up * _lane_tile(tab_ref[1], w)
            + dn * _lane_tile(tab_ref[2], w))


def _dup_kv_heads(t):
    swapped = pltpu.roll(t, A_HEAD_DIM, 1)
    lo = lax.broadcasted_iota(jnp.int32, t.shape, 1) < A_HEAD_DIM
    return jnp.concatenate([jnp.where(lo, t, swapped), jnp.where(lo, swapped, t)], axis=1)


def _load_token_rows(ref, rows):
    return jnp.concatenate([ref[pl.ds(s, rows, stride=SUB), :] for s in range(SUB)], axis=1)


def _store_token_rows(ref, val):
    rows = val.shape[0]
    for s in range(SUB):
        ref[pl.ds(s, rows, stride=SUB), :] = val[:, s * LANES:(s + 1) * LANES]


def _mod_kernel(c_ref, w_ref, b_ref, o_ref):
    c = c_ref[...]
    a = (c * jax.nn.sigmoid(c)).astype(BF16)
    o_ref[0, 0] = _dot(a, w_ref[0].astype(BF16)) + b_ref[0, 0]


def _mod_call(c_all, w_mod, b_mod):
    depth, d, _ = w_mod.shape
    return pl.pallas_call(
        _mod_kernel,
        grid=(depth, 6),
        in_specs=[pl.BlockSpec((MOD_ROWS, d), lambda l, k: (0, 0)),
                  pl.BlockSpec((1, d, d), lambda l, k: (l, 0, k)),
                  pl.BlockSpec((1, 1, 1, d), lambda l, k: (l, k, 0, 0))],
        out_specs=pl.BlockSpec((1, 1, MOD_ROWS, d), lambda l, k: (l, k, 0, 0)),
        out_shape=jax.ShapeDtypeStruct((depth, 6, MOD_ROWS, d), F32),
        compiler_params=_params(("arbitrary", "arbitrary")),
        name="mod",
    )(c_all, w_mod, b_mod.reshape(depth, 6, 1, d))


def _proj_kernel(*refs, has_f, n_lat_tiles):
    if has_f:
        x_ref, f_ref, modp_ref, refs = refs[0], refs[1], refs[2], refs[3:]
    else:
        x_ref, xc_ref, refs = refs[0], refs[1], refs[2:]
    (mod_ref, nw_ref, ta_ref, tb_ref, w1_ref, wg_ref, qn_ref, wuq_ref, kvn_ref, wukn_ref, wuv_ref,
     rp_ref), refs = refs[:12], refs[12:]
    xo_ref, qa_ref, ka_ref, va_ref, qb_ref, kb_ref, vb_ref, ga_ref, gb_ref = refs

    if has_f:
        x = x_ref[...] + modp_ref[0, 5:6, :] * _load_token_rows(f_ref, x_ref.shape[0])
    else:
        x = jnp.where(pl.program_id(0) < n_lat_tiles, x_ref[...], xc_ref[...])
    xo_ref[...] = x
    h = (_rms(x, nw_ref[...]) * (1.0 + mod_ref[0, 1:2, :]) + mod_ref[0, 0:1, :]).astype(BF16)

    t = _dot(h, w1_ref[...])
    qa_ref[...] = (_rope(t[:, W1_QA:W1_KA], ta_ref, 16) * (A_HEAD_DIM ** -0.5 * LOG2E)).astype(BF16)
    ka_ref[...] = _dup_kv_heads(_rope(t[:, W1_KA:W1_VA], ta_ref, 16)).astype(BF16)
    va_ref[...] = _dup_kv_heads(t[:, W1_VA:W1_CQ]).astype(BF16)

    cq = _rms(t[:, W1_CQ:W1_CKV], qn_ref[...]).astype(BF16)
    qb = _rope(_dot(cq, wuq_ref[...]), tb_ref, 8)
    qb_ref[...] = (qb * ((B_NOPE + B_ROPE) ** -0.5 * LOG2E)).astype(BF16)

    ckv = _rms(t[:, W1_CKV:W1_KR], kvn_ref[...]).astype(BF16)
    vb_ref[...] = _dot(ckv, wuv_ref[...]).astype(BF16)
    kr = t[:, W1_KR:W1_COLS]
    kr_hi = kr.astype(BF16)
    kr_lo = (kr - kr_hi.astype(F32)).astype(BF16)
    kb = _dot(ckv, wukn_ref[...]) + _dot(kr_hi, rp_ref[...]) + _dot(kr_lo, rp_ref[...])
    kb_ref[...] = _rope(kb, tb_ref, 8).astype(BF16)

    d = ga_ref.shape[-1]
    ga_ref[...] = jax.nn.sigmoid(_dot(h, wg_ref[:, :d])).astype(BF16)
    gb_ref[...] = jax.nn.sigmoid(_dot(h, wg_ref[:, d:])).astype(BF16)


def _proj_call(x, f, modp, mod, nw, tab_a, tab_b, w, *, n_lat, seq):
    has_f = f is not None
    n = x.shape[0] if has_f else x[0].shape[0] + x[1].shape[0]
    d = mod.shape[-1]
    nt = n // TM
    n_lat_t = n_lat // TM
    tps = seq // TM
    ctx_row = n_lat // seq

    def tok(i):
        return (i, 0)

    def modi(i):
        return (jnp.where(i < n_lat_t, i // tps, ctx_row), 0, 0)

    def tabi(i):
        return (0, jnp.where(i < n_lat_t, i % tps, tps), 0)

    def full(a):
        return pl.BlockSpec(a.shape, lambda i: (0,) * a.ndim, pipeline_mode=pl.Buffered(1))

    mod_spec = pl.BlockSpec((1, 8, d), modi)
    if has_f:
        in_specs = [pl.BlockSpec((TM, d), tok), pl.BlockSpec((TM * SUB, LANES), tok), mod_spec]
        args = [x, f, modp]
    else:
        in_specs = [pl.BlockSpec((TM, d), lambda i: (jnp.minimum(i, n_lat_t - 1), 0)),
                    pl.BlockSpec((TM, d), lambda i: (jnp.maximum(i - n_lat_t, 0), 0))]
        args = list(x)
    in_specs += [mod_spec, full(nw), pl.BlockSpec((3, TM, LANES), tabi), pl.BlockSpec((3, TM, LANES), tabi)]
    args += [mod, nw, tab_a, tab_b]
    for k in ("w1", "wg", "qn", "wuq", "kvn", "wukn", "wuv", "rp"):
        in_specs.append(full(w[k]))
        args.append(w[k])

    widths = (512, 256, 256, 1024, 1024, 512, d, d)
    out_shape = [jax.ShapeDtypeStruct((n, d), F32)] + [jax.ShapeDtypeStruct((n, wd), BF16) for wd in widths]
    out_specs = [pl.BlockSpec((TM, d), tok)] + [pl.BlockSpec((TM, wd), tok) for wd in widths]
    return pl.pallas_call(
        functools.partial(_proj_kernel, has_f=has_f, n_lat_tiles=n_lat_t),
        grid=(nt,), in_specs=in_specs, out_specs=out_specs, out_shape=out_shape,
        compiler_params=_params(("arbitrary",)),
        name="proj",
    )(*args)


def _window_kernel(sink_ref, band_ref, q_ref, kp_ref, kc_ref, kn_ref, kx_ref, vp_ref, vc_ref, vn_ref, vx_ref, o_ref,
                   *, nb, n_off):
    rows = A_GROUP * BLOCK
    wq = q_ref.shape[0] // BLOCK
    lo = lax.broadcasted_iota(jnp.int32, (BLOCK, LANES), 1) < A_HEAD_DIM
    col_blk = lax.broadcasted_iota(jnp.int32, (1, 3 * BLOCK), 1) // BLOCK
    row_head = lax.broadcasted_iota(jnp.int32, (rows, 1), 0) // BLOCK
    zero = jnp.zeros((BLOCK, LANES), BF16)

    def key_block(sub, rel, lanes, p_ref, c_ref, n_ref):
        j = sub + rel
        if j < 0:
            return p_ref[:, lanes]
        if j >= wq:
            return n_ref[:, lanes]
        return c_ref[j * BLOCK:(j + 1) * BLOCK, lanes]

    for sub in range(wq):
        n = wq * pl.program_id(1) + n_off + sub
        qrows = slice(sub * BLOCK, (sub + 1) * BLOCK)
        off_prev = jnp.where(jnp.logical_or(n == 0, n >= nb), NEG, 0.0)
        off_cur = jnp.where(n >= nb, NEG, 0.0)
        off_next = jnp.where(n >= nb - 1, NEG, 0.0)
        bias = band_ref[...] + jnp.where(col_blk == 0, off_prev, jnp.where(col_blk == 1, off_cur, off_next))
        bias = jnp.concatenate([bias] * A_GROUP, axis=0)
        for k in range(A_KV_HEADS):
            ksl = slice(k * LANES, (k + 1) * LANES)
            parts = []
            for j in range(A_GROUP // 2):
                blk = q_ref[qrows, k * 2 * LANES + j * LANES:k * 2 * LANES + (j + 1) * LANES]
                parts += [jnp.where(lo, blk, zero), jnp.where(lo, zero, blk)]
            qs = jnp.concatenate(parts, axis=0)
            k_lat = jnp.concatenate([key_block(sub, rel, ksl, kp_ref, kc_ref, kn_ref) for rel in (-1, 0, 1)], axis=0)
            v_lat = jnp.concatenate([key_block(sub, rel, ksl, vp_ref, vc_ref, vn_ref) for rel in (-1, 0, 1)], axis=0)
            s_l = _dot_nt(qs, k_lat) + bias
            s_x = _dot_nt(qs, kx_ref[:, ksl])
            sink = jnp.zeros((rows, 1), F32)
            for g in range(A_GROUP):
                sink = jnp.where(row_head == g, sink_ref[k * A_GROUP + g] * LOG2E, sink)
            m = jnp.maximum(jnp.maximum(jnp.max(s_l, axis=-1, keepdims=True), jnp.max(s_x, axis=-1, keepdims=True)),
                            sink)
            e_l = jnp.exp2(s_l - m)
            e_x = jnp.exp2(s_x - m)
            den = jnp.sum(e_l, axis=-1, keepdims=True) + jnp.sum(e_x, axis=-1, keepdims=True) + jnp.exp2(sink - m)
            o = (_dot(e_l.astype(BF16), v_lat) + _dot(e_x.astype(BF16), vx_ref[:, ksl])) / den
            for j in range(A_GROUP // 2):
                even = o[(2 * j) * BLOCK:(2 * j + 1) * BLOCK]
                odd = o[(2 * j + 1) * BLOCK:(2 * j + 2) * BLOCK]
                o_ref[qrows, k * 2 * LANES + j * LANES:k * 2 * LANES + (j + 1) * LANES] = (
                    jnp.where(lo, even, odd).astype(BF16))


def _window_call(sink, qa, ka, va, *, batch, seq, ctx_len, latent):
    nb = seq // BLOCK
    wq = min(WIN_QB, nb) if latent else ctx_len // BLOCK
    steps = nb // wq if latent else 1
    assert nb % wq == 0
    ctx0 = batch * seq // ctx_len
    q0 = 0 if latent else batch * seq // (wq * BLOCK)

    def qi(b, i):
        return (q0 + b * steps + i, 0)

    def edge(off):
        return lambda b, i: (b * nb + (jnp.clip(wq * i + off, 0, nb - 1) if latent else 0), 0)

    def run(b, i):
        return (b * (nb // wq) + (i if latent else 0), 0)

    def xi(b, i):
        return (ctx0 + b, 0)

    kv_specs = [pl.BlockSpec((BLOCK, 2 * LANES), edge(-1)), pl.BlockSpec((wq * BLOCK, 2 * LANES), run),
                pl.BlockSpec((BLOCK, 2 * LANES), edge(wq)), pl.BlockSpec((ctx_len, 2 * LANES), xi)]
    return pl.pallas_call(
        functools.partial(_window_kernel, nb=nb, n_off=0 if latent else nb),
        grid=(batch, steps),
        in_specs=[pl.BlockSpec(memory_space=pltpu.SMEM), pl.BlockSpec((BLOCK, 3 * BLOCK), lambda b, i: (0, 0)),
                  pl.BlockSpec((wq * BLOCK, 4 * LANES), qi)] + kv_specs + kv_specs,
        out_specs=pl.BlockSpec((wq * BLOCK, 4 * LANES), lambda b, i: (b * steps + i, 0)),
        out_shape=jax.ShapeDtypeStruct((batch * steps * wq * BLOCK, 4 * LANES), BF16),
        compiler_params=_params(("arbitrary", "arbitrary")),
        name="window" if latent else "window_ctx",
    )(sink, _window_band(), qa, ka, ka, ka, ka, va, va, va, va)


def _mla_kernel(q_ref, *refs, with_lat):
    if with_lat:
        kl_ref, kx_ref, vl_ref, vx_ref, o_ref, vaug_ref = refs
    else:
        kx_ref, vx_ref, o_ref, vaug_ref = refs
    n_ctx = vx_ref.shape[0]
    n_pairs = o_ref.shape[1] // LANES
    lo = lax.broadcasted_iota(jnp.int32, (o_ref.shape[0], LANES), 1) < B_V

    @pl.when(pl.program_id(2) == 0)
    def _():
        one = jnp.ones((1, LANES), BF16)
        for v_ref, start in ((vx_ref, 0),) + (((vl_ref, n_ctx),) if with_lat else ()):
            n = v_ref.shape[0]
            keep = lax.broadcasted_iota(jnp.int32, (n, LANES), 1) < B_V
            for pp in range(n_pairs):
                v = v_ref[:, pp * LANES:(pp + 1) * LANES]
                vaug_ref[2 * pp, start:start + n, :] = jnp.where(keep, v, one)
                vaug_ref[2 * pp + 1, start:start + n, :] = jnp.where(keep, one, v)

    def head(h):
        sl = slice(h * LANES, (h + 1) * LANES)
        q = q_ref[:, sl]
        s_x = _dot_nt(q, kx_ref[:, sl])
        m = jnp.max(s_x, axis=-1, keepdims=True)
        if with_lat:
            s_l = _dot_nt(q, kl_ref[:, sl])
            m = jnp.maximum(m, jnp.max(s_l, axis=-1, keepdims=True))
        o = _dot(jnp.exp2((s_x - m).astype(BF16)), vaug_ref[h, :n_ctx, :])
        if with_lat:
            o = o + _dot(jnp.exp2((s_l - m).astype(BF16)), vaug_ref[h, n_ctx:, :])
        return o / pltpu.roll(o, B_V, 1)

    for pp in range(n_pairs):
        o_ref[:, pp * LANES:(pp + 1) * LANES] = jnp.where(lo, head(2 * pp), head(2 * pp + 1)).astype(BF16)


def _mla_call(qb, kb, vb, *, batch, seq, ctx_len, latent):
    ctx0 = batch * seq // ctx_len
    groups = B_HEADS // (2 * MLA_PAIRS)
    qk_w = MLA_PAIRS * 2 * LANES
    v_w = MLA_PAIRS * LANES
    tq = TQ if latent else ctx_len
    nq = seq // TQ if latent else 1
    kx_spec = pl.BlockSpec((ctx_len, qk_w), lambda b, p, j: (ctx0 + b, p))
    vx_spec = pl.BlockSpec((ctx_len, v_w), lambda b, p, j: (ctx0 + b, p))
    if latent:
        in_specs = [pl.BlockSpec((tq, qk_w), lambda b, p, j: (b * nq + j, p)),
                    pl.BlockSpec((seq, qk_w), lambda b, p, j: (b, p)), kx_spec,
                    pl.BlockSpec((seq, v_w), lambda b, p, j: (b, p)), vx_spec]
        args = (qb, kb, kb, vb, vb)
    else:
        in_specs = [pl.BlockSpec((tq, qk_w), lambda b, p, j: (ctx0 + b, p)), kx_spec, vx_spec]
        args = (qb, kb, vb)
    return pl.pallas_call(
        functools.partial(_mla_kernel, with_lat=latent),
        grid=(batch, groups, nq),
        in_specs=in_specs,
        out_specs=pl.BlockSpec((tq, v_w), lambda b, p, j: (b * nq + j, p)),
        out_shape=jax.ShapeDtypeStruct((batch * nq * tq, groups * v_w), BF16),
        scratch_shapes=[pltpu.VMEM((2 * MLA_PAIRS, ctx_len + (seq if latent else 0), LANES), BF16)],
        compiler_params=_params(("arbitrary", "arbitrary", "arbitrary")),
        name="mla" if latent else "mla_ctx",
    )(*args)


def _top2(v):
    i1 = jnp.zeros_like(v[0])
    m1 = v[0]
    for i in range(1, EPG):
        u = v[i] > m1
        i1 = jnp.where(u, float(i), i1)
        m1 = jnp.where(u, v[i], m1)
    i2 = jnp.zeros_like(v[0])
    m2 = jnp.full_like(v[0], -jnp.inf)
    for i in range(EPG):
        cand = jnp.where(i1 == float(i), -jnp.inf, v[i])
        u = cand > m2
        i2 = jnp.where(u, float(i), i2)
        m2 = jnp.where(u, cand, m2)
    return i1, i2, m1, m2


def _pick(idx, vals):
    out = vals[0]
    for i in range(1, len(vals)):
        out = jnp.where(idx == float(i), vals[i], out)
    return out


def _merge_kernel(x_ref, ya_ref, yb_ref, yax_ref, ybx_ref, ga_ref, gb_ref, mod_ref, nw_ref, woa_ref, wob_ref, wout_ref,
                  wr2_ref, rb_ref, xo_ref, h_ref, meta_ref, *, n_lat_tiles):
    ya, yb = ya_ref[...], yb_ref[...]
    if yax_ref is not None:
        is_lat = pl.program_id(0) < n_lat_tiles
        ya = jnp.where(is_lat, ya, yax_ref[...])
        yb = jnp.where(is_lat, yb, ybx_ref[...])
    a = _dot(ya, woa_ref[...])
    b = _dot(yb, wob_ref[...])
    mix = (ga_ref[...].astype(F32) * a + gb_ref[...].astype(F32) * b).astype(BF16)
    x = x_ref[...] + mod_ref[0, 2:3, :] * _dot(mix, wout_ref[...])
    xo_ref[...] = x
    h = _rms(x, nw_ref[...]) * (1.0 + mod_ref[0, 4:5, :]) + mod_ref[0, 3:4, :]
    _store_token_rows(h_ref, h)

    h_hi = h.astype(BF16)
    h_lo = (h - h_hi.astype(F32)).astype(BF16)
    l_hi = _dot(h_hi, wr2_ref[...])
    logits = l_hi[:, :LANES] + l_hi[:, LANES:] + _dot(h_lo, wr2_ref[:, :LANES])
    sc = jax.nn.sigmoid(logits.T[:N_EXPERTS, :])
    sel = sc + rb_ref[...]
    sel_rows = [sel[e:e + 1, :] for e in range(N_EXPERTS)]
    sc_rows = [sc[e:e + 1, :] for e in range(N_EXPERTS)]
    best = jnp.zeros_like(sel_rows[0])
    best_v = None
    for g in range(N_GROUPS):
        _, _, m1, m2 = _top2(sel_rows[g * EPG:(g + 1) * EPG])
        gv = m1 + m2
        if best_v is None:
            best_v = gv
        else:
            u = gv > best_v
            best = jnp.where(u, float(g), best)
            best_v = jnp.where(u, gv, best_v)
    sel_g = [_pick(best, [sel_rows[g * EPG + i] for g in range(N_GROUPS)]) for i in range(EPG)]
    sc_g = [_pick(best, [sc_rows[g * EPG + i] for g in range(N_GROUPS)]) for i in range(EPG)]
    i1, i2, _, _ = _top2(sel_g)
    s1 = _pick(i1, sc_g)
    s2 = _pick(i2, sc_g)
    tot = s1 + s2
    first_low = i1 < i2
    e_lo = jnp.where(first_low, i1, i2)
    e_hi = jnp.where(first_low, i2, i1)
    w_lo = jnp.where(first_low, s1, s2) / tot
    w_hi = jnp.where(first_low, s2, s1) / tot
    pid = jnp.where(e_lo == 0.0, e_hi - 1.0, jnp.where(e_lo == 1.0, e_hi + 1.0, 5.0))
    bucket = best * float(len(PAIRS)) + pid
    t = sel.shape[1]
    meta_ref[0] = jnp.concatenate([w_lo, w_hi, bucket, jnp.zeros((SUB - 3, t), F32)], axis=0)


def _merge_call(x, ya, yb, y_ctx, ga, gb, mod, nw, w, wr2, rb, *, n_rows, n_lat, seq):
    d = x.shape[1]
    tm = TM_MERGE
    nt = n_rows // tm
    n_lat_t = n_lat // tm
    tps = seq // tm
    ctx_row = n_lat // seq

    def tok(i):
        return (i, 0)

    def full(a):
        return pl.BlockSpec(a.shape, lambda i: (0,) * a.ndim)

    y_specs = [pl.BlockSpec((tm, 512), lambda i: (jnp.minimum(i, n_lat_t - 1), 0))] * 2
    y_args = [ya, yb]
    if y_ctx is not None:
        y_specs += [pl.BlockSpec((tm, 512), lambda i: (jnp.maximum(i - n_lat_t, 0), 0))] * 2
        y_args += list(y_ctx)

    def body(x_ref, ya_ref, yb_ref, *refs):
        if y_ctx is None:
            refs = (None, None) + refs
        _merge_kernel(x_ref, ya_ref, yb_ref, *refs, n_lat_tiles=n_lat_t)

    return pl.pallas_call(
        body,
        grid=(nt,),
        in_specs=[pl.BlockSpec((tm, d), tok)] + y_specs + [
                  pl.BlockSpec((tm, d), tok), pl.BlockSpec((tm, d), tok),
                  pl.BlockSpec((1, 8, d), lambda i: (jnp.where(i < n_lat_t, i // tps, ctx_row), 0, 0)),
                  full(nw), full(w["woa"]), full(w["wob"]), full(w["wout"]), full(wr2), full(rb)],
        out_specs=[pl.BlockSpec((tm, d), tok), pl.BlockSpec((tm * SUB, LANES), tok),
                   pl.BlockSpec((1, SUB, tm), lambda i: (i, 0, 0))],
        out_shape=[jax.ShapeDtypeStruct((n_rows, d), F32), jax.ShapeDtypeStruct((n_rows * SUB, LANES), F32),
                   jax.ShapeDtypeStruct((nt, SUB, tm), F32)],
        compiler_params=_params(("arbitrary",)),
        name="merge",
    )(x, *y_args, ga, gb, mod, nw, w["woa"], w["wob"], w["wout"], wr2, rb)


def _moe_kernel(tok_ref, e1_ref, e2_ref, nv_ref, wts_ref, h_hbm, wg1_ref, wu1_ref, wd1_ref, wg2_ref, wu2_ref, wd2_ref,
                f_hbm, hbuf, obuf, wgu_ref, wdn_ref, gsem, ssem, *, n_tok, n_tiles):
    t = pl.program_id(0)
    slot = t % MOE_BUFS
    slot1 = (t + 1) % MOE_BUFS
    slot2 = (t + 2) % MOE_BUFS
    live = nv_ref[t] > 0
    prev_live = jnp.logical_and(t >= 1, nv_ref[jnp.maximum(t - 1, 0)] > 0)

    t_prev = jnp.maximum(t - 1, 0)
    new_pair = jnp.logical_or(t == 0, jnp.logical_or(e1_ref[t] != e1_ref[t_prev], e2_ref[t] != e2_ref[t_prev]))

    @pl.when(jnp.logical_and(live, new_pair))
    def _():
        for i, w_ref in enumerate((wg1_ref, wu1_ref, wg2_ref, wu2_ref)):
            wgu_ref[i] = w_ref[0, 0].astype(BF16)
        for i, w_ref in enumerate((wd1_ref, wd2_ref)):
            wdn_ref[i] = w_ref[0, 0].astype(BF16)

    def gather_row(tile, s, r):
        tk = jnp.minimum(tok_ref[tile * TE + r], n_tok - 1)
        return pltpu.make_async_copy(h_hbm.at[pl.ds(tk * SUB, SUB)], hbuf.at[s, pl.ds(r * SUB, SUB)], gsem.at[s])

    def scatter_row(tile, s, r):
        tk = tok_ref[tile * TE + r]
        return pltpu.make_async_copy(obuf.at[s, pl.ds(r * SUB, SUB)], f_hbm.at[pl.ds(tk * SUB, SUB)], ssem.at[s])

    def wait_gather(s):
        pltpu.make_async_copy(h_hbm.at[pl.ds(0, TE * SUB)], hbuf.at[s], gsem.at[s]).wait()

    def wait_scatter(s):
        pltpu.make_async_copy(obuf.at[s], f_hbm.at[pl.ds(0, TE * SUB)], ssem.at[s]).wait()

    def start_rows(make_row, tile, s):
        def body(i, carry):
            make_row(tile, s, 2 * i).start(priority=0)
            make_row(tile, s, 2 * i + 1).start(priority=1)
            return carry
        lax.fori_loop(0, TE // 2, body, 0, unroll=4)

    @pl.when(t == 0)
    def _():
        obuf[...] = jnp.zeros(obuf.shape, F32)
        for half in range(2):
            fill = pltpu.make_async_copy(obuf.at[0], f_hbm.at[pl.ds((n_tok + half * TE) * SUB, TE * SUB)], ssem.at[0])
            fill.start()
            fill.wait()
        start_rows(gather_row, 0, 0)
        start_rows(gather_row, 1, 1)

    @pl.when(live)
    def _():
        wait_gather(slot)

        @pl.when(t >= 2)
        def _():
            wait_scatter(slot)

    @pl.when(live)
    def _():
        g_tile = jnp.minimum(t + 2, n_tiles - 1)
        s_tile = jnp.where(t == 0, n_tiles - 1, t - 1)
        issue = ([functools.partial(gather_row, g_tile, slot2, r) for r in range(TE)]
                 + [functools.partial(scatter_row, s_tile, slot2, r) for r in range(TE)])
        n_stage = 6
        per_stage = -(-len(issue) // n_stage)

        def issue_stage(k):
            for i, make in enumerate(issue[k * per_stage:(k + 1) * per_stage]):
                make().start(priority=i % 2)

        h = _load_token_rows(hbuf.at[slot], TE).astype(BF16)
        w_lo = jnp.broadcast_to(wts_ref[0, 0:1, :], (LANES, TE)).T[:, :1]
        w_hi = jnp.broadcast_to(wts_ref[0, 1:2, :], (LANES, TE)).T[:, :1]
        outs = []
        stage = 0
        for which, wt in ((0, w_lo), (1, w_hi)):
            hg = _dot(h, wgu_ref[2 * which])
            issue_stage(stage)
            hu = _dot(h, wgu_ref[2 * which + 1])
            issue_stage(stage + 1)
            act = (hg * jax.nn.sigmoid(hg) * hu).astype(BF16)
            outs.append(wt * _dot(act, wdn_ref[which]))
            issue_stage(stage + 2)
            stage += 3
        _store_token_rows(obuf.at[slot], outs[0] + outs[1])

    @pl.when(jnp.logical_and(jnp.logical_not(live), prev_live))
    def _():
        wait_gather(slot)
        wait_gather(slot1)
        wait_scatter(slot1)

        @pl.when(t >= 2)
        def _():
            wait_scatter(slot)
        start_rows(scatter_row, t - 1, slot2)
        wait_scatter(slot2)


def _moe_call(tok, e1, e2, nv, wts, h3, wg, wu, wd, *, layer, n_tok):
    d = wg.shape[2]
    de = wg.shape[3]
    n_tiles = e1.shape[0]

    def wspec(shape, which):
        return pl.BlockSpec((1, 1) + shape, lambda t, tok, e1, e2, nv: (layer, (e1, e2)[which][t], 0, 0))

    grid_spec = pltpu.PrefetchScalarGridSpec(
        num_scalar_prefetch=4,
        grid=(n_tiles,),
        in_specs=[pl.BlockSpec((1, 2, TE), lambda t, tok, e1, e2, nv: (t, 0, 0)),
                  pl.BlockSpec(memory_space=pl.ANY),
                  wspec((d, de), 0), wspec((d, de), 0), wspec((de, d), 0),
                  wspec((d, de), 1), wspec((d, de), 1), wspec((de, d), 1)],
        out_specs=pl.BlockSpec(memory_space=pl.ANY),
        scratch_shapes=[pltpu.VMEM((MOE_BUFS, TE * SUB, LANES), F32), pltpu.VMEM((MOE_BUFS, TE * SUB, LANES), F32),
                        pltpu.VMEM((4, d, de), BF16), pltpu.VMEM((2, de, d), BF16),
                        pltpu.SemaphoreType.DMA((MOE_BUFS,)), pltpu.SemaphoreType.DMA((MOE_BUFS,))],
    )
    return pl.pallas_call(
        functools.partial(_moe_kernel, n_tok=n_tok, n_tiles=n_tiles),
        grid_spec=grid_spec,
        out_shape=jax.ShapeDtypeStruct(((n_tok + 2 * TE) * SUB, LANES), F32),
        compiler_params=_params(("arbitrary",)),
        name="moe",
    )(tok, e1, e2, nv, wts, h3, wg, wu, wd, wg, wu, wd)


def _route_tables(meta, n_tok):
    w_lo = meta[:, 0, :].reshape(-1)
    w_hi = meta[:, 1, :].reshape(-1)
    bucket = meta[:, 2, :].reshape(-1).astype(jnp.int32)
    n_slots = n_tok + N_BUCKETS * TE
    n_tiles = n_slots // TE
    buckets = jnp.arange(N_BUCKETS, dtype=jnp.int32)
    counts = jnp.sum(bucket[:, None] == buckets[None, :], axis=0, dtype=jnp.int32)
    padded = ((counts + TE - 1) // TE) * TE
    pad_end = jnp.cumsum(padded)
    pad_start = pad_end - padded
    fill_i = jnp.arange(TE, dtype=jnp.int32)[None, :]
    fill_key = jnp.where(fill_i < (padded - counts)[:, None], 2 * buckets[:, None] + 1, 2 * N_BUCKETS).reshape(-1)
    keys = jnp.concatenate([2 * bucket, fill_key])
    ids = jnp.concatenate([jnp.arange(n_tok, dtype=jnp.int32), jnp.full((N_BUCKETS * TE,), -1, jnp.int32)])
    zeros = jnp.zeros((N_BUCKETS * TE,), F32)
    _, ids, w_lo, w_hi = lax.sort((keys, ids, jnp.concatenate([w_lo, zeros]), jnp.concatenate([w_hi, zeros])),
                                  num_keys=1, is_stable=True)
    slot = jnp.arange(n_slots, dtype=jnp.int32)
    tok = jnp.where(ids < 0, n_tok + ((slot // TE) % 2) * TE + slot % TE, ids)
    wts = jnp.stack([w_lo.reshape(n_tiles, TE), w_hi.reshape(n_tiles, TE)], axis=1)

    tile_start = jnp.arange(n_tiles, dtype=jnp.int32) * TE
    tb = jnp.sum(tile_start[:, None] >= pad_end[None, :], axis=1, dtype=jnp.int32)
    used = tb < N_BUCKETS
    onehot = (tb[:, None] == buckets[None, :]).astype(jnp.int32)
    nv = jnp.clip(jnp.sum(onehot * (counts + pad_start)[None, :], axis=1) - tile_start, 0, TE) * used
    tbe = jnp.where(used, tb, jnp.max(jnp.where(used, tb, 0)))
    pid = tbe % len(PAIRS)
    pair_lo = (pid >= 3).astype(jnp.int32) + (pid >= 5).astype(jnp.int32)
    pair_hi = pid + 1 - 2 * (pid >= 3).astype(jnp.int32) - (pid >= 5).astype(jnp.int32)
    e1 = (tbe // len(PAIRS)) * EPG + pair_lo
    e2 = (tbe // len(PAIRS)) * EPG + pair_hi
    return tok.astype(jnp.int32), e1.astype(jnp.int32), e2.astype(jnp.int32), nv.astype(jnp.int32), wts


def _final_kernel(x_ref, f_ref, mod_ref, nw_ref, o_ref):
    x = x_ref[...] + mod_ref[0, 5:6, :] * _load_token_rows(f_ref, x_ref.shape[0])
    o_ref[...] = _rms(x, nw_ref[...])


def _final_call(x, f, mod, nw, *, n_lat, seq):
    d = x.shape[1]
    tm = TM_MERGE
    tps = seq // tm
    return pl.pallas_call(
        _final_kernel,
        grid=(n_lat // tm,),
        in_specs=[pl.BlockSpec((tm, d), lambda i: (i, 0)), pl.BlockSpec((tm * SUB, LANES), lambda i: (i, 0)),
                  pl.BlockSpec((1, 8, d), lambda i: (i // tps, 0, 0)), pl.BlockSpec((1, d), lambda i: (0, 0))],
        out_specs=pl.BlockSpec((tm, d), lambda i: (i, 0)),
        out_shape=jax.ShapeDtypeStruct((n_lat, d), F32),
        compiler_params=_params(("arbitrary",)),
        name="final",
    )(x, f, mod, nw)


def _rope_table(pos_r, pos_c, dim, lane_of):
    d = dim // 2
    half = d // 2
    lane = np.arange(LANES)
    rl = lane_of(lane)
    is_rope = rl >= 0
    rl = np.maximum(rl, 0)
    use_col = rl >= d
    j = rl % half
    first = (rl % d) < half
    inv = (np.float32(ROPE_BASE) ** (-(2.0 * j).astype(np.float32) / np.float32(d))).astype(np.float32)
    pos = np.where(use_col[None, :], pos_c[:, None], pos_r[:, None]).astype(np.float32)
    ang = pos * inv[None, :]
    cos = np.where(is_rope[None, :], np.cos(ang), 1.0)
    sin = np.where(is_rope[None, :], np.sin(ang), 0.0)
    return np.stack([cos, np.where(first[None, :], -sin, 0.0), np.where(first[None, :], 0.0, sin)]).astype(np.float32)


def _tables(seq):
    t = np.arange(seq)
    rows, cols = t // GRID_W, t % GRID_W
    tab_a = _rope_table(rows, cols, A_HEAD_DIM, lambda lane: lane % A_HEAD_DIM)
    tab_b = _rope_table(rows, cols, B_ROPE,
                        lambda lane: np.where((lane >= B_NOPE) & (lane < B_NOPE + B_ROPE), lane - B_NOPE, -1))
    ident = np.stack([np.ones((TM, LANES), np.float32), np.zeros((TM, LANES), np.float32),
                      np.zeros((TM, LANES), np.float32)])
    return jnp.asarray(np.concatenate([tab_a, ident], axis=1)), jnp.asarray(np.concatenate([tab_b, ident], axis=1))


def _window_band():
    r = np.arange(BLOCK)[:, None]
    c = np.arange(3 * BLOCK)[None, :]
    dist = c - r
    return jnp.asarray(np.where((dist >= BLOCK - WINDOW) & (dist <= BLOCK + WINDOW), 0.0, NEG).astype(np.float32))


def _layer_weights(w_in, w_uq, w_ukv, q_norm, kv_norm, w_o_a, w_o_b, w_out):
    n_mix = W1_KR + B_ROPE
    w1 = jnp.pad(w_in[:, :n_mix], ((0, 0), (0, W1_COLS - n_mix)))
    g_w = w_in[:, n_mix:]
    wuq = jnp.pad(w_uq.reshape(B_Q_RANK, B_HEADS, B_NOPE + B_ROPE),
                  ((0, 0), (0, 0), (0, LANES - B_NOPE - B_ROPE))).reshape(B_Q_RANK, B_HEADS * LANES)
    ukv = w_ukv.reshape(B_KV_RANK, B_HEADS, B_NOPE + B_V)
    wukn = jnp.pad(ukv[:, :, :B_NOPE], ((0, 0), (0, 0), (0, LANES - B_NOPE))).reshape(B_KV_RANK, B_HEADS * LANES)
    wuv = ukv[:, :, B_NOPE:].reshape(B_KV_RANK, B_HEADS * B_V)
    src = jnp.arange(LANES)[:, None]
    dst = jnp.arange(B_HEADS * LANES)[None, :]
    rp = ((src < B_ROPE) & (dst % LANES == src + B_NOPE)).astype(BF16)
    return dict(w1=w1.astype(BF16), wg=g_w.astype(BF16), qn=q_norm.reshape(1, -1), wuq=wuq.astype(BF16),
                kvn=kv_norm.reshape(1, -1), wukn=wukn.astype(BF16), wuv=wuv.astype(BF16), rp=rp,
                woa=w_o_a.astype(BF16), wob=w_o_b.astype(BF16), wout=w_out.astype(BF16))


def kernel(x, c, ctx, c_ctx, w_mod, b_mod, norm_mix, norm_ffn, w_in, attn_sink, mla_q_norm, w_uq, mla_kv_norm, w_ukv,
           w_o_a, w_o_b, w_out, w_router, router_bias, w_expert_gate, w_expert_up, w_expert_down, final_norm):
    batch, seq, d = x.shape
    ctx_len = ctx.shape[1]
    depth = w_mod.shape[0]
    n_lat = batch * seq
    n_all = n_lat + batch * ctx_len
    for tile in (TM, TM_MERGE):
        assert seq % tile == 0 and (batch * ctx_len) % tile == 0
    assert d == SUB * LANES and seq % TQ == 0 and seq % GRID_W == 0 and batch + 1 <= MOD_ROWS

    c_all = jnp.concatenate([c, c_ctx[None, :], jnp.zeros((MOD_ROWS - batch - 1, d), F32)], axis=0)
    mod = _mod_call(c_all, w_mod, b_mod)
    mod = jnp.pad(mod.transpose(0, 2, 1, 3), ((0, 0), (0, 0), (0, 2), (0, 0)))

    tab_a, tab_b = _tables(seq)
    wr = jnp.pad(w_router.astype(F32), ((0, 0), (0, LANES - N_EXPERTS)))
    wr_hi = wr.astype(BF16)
    wr2 = jnp.concatenate([wr_hi, (wr - wr_hi.astype(F32)).astype(BF16)], axis=1)
    rb = router_bias.astype(F32).reshape(N_EXPERTS, 1)

    xs = (x.reshape(n_lat, d), ctx.reshape(batch * ctx_len, d))
    f = None
    for l in range(depth):
        need_ctx = l < depth - 1
        w = _layer_weights(w_in[l], w_uq[l], w_ukv[l], mla_q_norm[l], mla_kv_norm[l], w_o_a[l], w_o_b[l], w_out[l])
        xs, qa, ka, va, qb, kb, vb, ga, gb = _proj_call(
            xs, f, mod[l - 1] if l else None, mod[l], norm_mix[l].reshape(1, d), tab_a, tab_b, w, n_lat=n_lat, seq=seq)
        dims = dict(batch=batch, seq=seq, ctx_len=ctx_len)
        sink = attn_sink[l].astype(F32)
        ya = _window_call(sink, qa, ka, va, latent=True, **dims)
        yb = _mla_call(qb, kb, vb, latent=True, **dims)
        y_ctx = None
        if need_ctx:
            y_ctx = (_window_call(sink, qa, ka, va, latent=False, **dims), _mla_call(qb, kb, vb, latent=False, **dims))
        n_rows = n_all if need_ctx else n_lat
        xs, h3, meta = _merge_call(xs, ya, yb, y_ctx, ga, gb, mod[l], norm_ffn[l].reshape(1, d), w, wr2, rb,
                                   n_rows=n_rows, n_lat=n_lat, seq=seq)
        tok, e1, e2, nv, wts = _route_tables(meta, n_rows)
        f = _moe_call(tok, e1, e2, nv, wts, h3, w_expert_gate, w_expert_up, w_expert_down, layer=l, n_tok=n_rows)
    out = _final_call(xs, f, mod[depth - 1], final_norm.reshape(1, d), n_lat=n_lat, seq=seq)
    return out.reshape(batch, seq, d)
```
